```python
import math
import jax, jax.numpy as jnp
from jax import lax
import numpy as np

D_MODEL = 1024
BATCH = 4
SEQ = 4096
DEPTH = 2

MIX_WIDTH = D_MODEL
GDN_HEADS = 4
GDN_HEAD_DIM = 128
GDN_WIDTH = GDN_HEADS * GDN_HEAD_DIM
GDN_CONV = 4
GDN_CHUNK = 64
NSA_HEADS = 8
NSA_KV_HEADS = 2
NSA_HEAD_DIM = 64
NSA_WIDTH = NSA_HEADS * NSA_HEAD_DIM
NSA_KV_WIDTH = NSA_KV_HEADS * NSA_HEAD_DIM
CMP_LEN = 32
CMP_STRIDE = 16
CMP_HIDDEN = 2 * NSA_HEAD_DIM
SEL_BLOCK = 64
N_SELECT = 16
WINDOW = 512
Q_BLOCK = 128
ROPE_THETA = 500000.0
ROT_DIM = NSA_HEAD_DIM // 4
D_FF = 2816
N_ADA = 9
NORM_EPS = 1e-6
IN_SIZES = (3 * GDN_WIDTH, GDN_WIDTH, GDN_HEADS, GDN_HEADS,
            NSA_WIDTH, NSA_KV_WIDTH, NSA_KV_WIDTH, NSA_KV_WIDTH,
            NSA_KV_WIDTH, NSA_KV_WIDTH, NSA_KV_WIDTH, 3 * NSA_HEADS)
IN_WIDTH = sum(IN_SIZES)

kernel_name = "hybrid_gdn_nsa_macaron_adaln"


def rms_norm(x, g):
    xf = x.astype(jnp.float32)
    y = xf * lax.rsqrt(jnp.mean(xf * xf, axis=-1, keepdims=True) + NORM_EPS)
    return (y * g.astype(jnp.float32)).astype(x.dtype)


def l2_norm(x):
    xf = x.astype(jnp.float32)
    return xf * lax.rsqrt(jnp.sum(xf * xf, axis=-1, keepdims=True) + NORM_EPS)


def modulate(h, shift, scale):
    return h * (1.0 + scale[:, None, :]) + shift[:, None, :]


def swiglu(h, w_in, w_out):
    gate, up = jnp.split(h @ w_in, 2, axis=-1)
    return (jax.nn.silu(gate) * up) @ w_out


def split_cols(x, sizes):
    offs = [int(o) for o in np.cumsum(sizes)[:-1]]
    return jnp.split(x, offs, axis=-1)


def masked_softmax(s, mask):
    s = jnp.where(mask, s.astype(jnp.float32), -jnp.inf)
    m = jnp.max(s, axis=-1, keepdims=True)
    m = jnp.where(jnp.isfinite(m), m, 0.0)
    p = jnp.exp(s - m)
    return p / jnp.maximum(jnp.sum(p, axis=-1, keepdims=True), 1e-30)


def partial_rope(x, cos, sin):
    half = ROT_DIM // 2
    xf = x[..., :ROT_DIM].astype(jnp.float32)
    x1, x2 = xf[..., :half], xf[..., half:]
    rot = jnp.concatenate([x1 * cos - x2 * sin, x2 * cos + x1 * sin], axis=-1).astype(x.dtype)
    return jnp.concatenate([rot, x[..., ROT_DIM:]], axis=-1)


def causal_dwconv(x, w):
    k = w.shape[0]
    return lax.conv_general_dilated(
        x, w[:, None, :].astype(x.dtype), window_strides=(1,), padding=[(k - 1, 0)],
        dimension_numbers=("NWC", "WIO", "NWC"), feature_group_count=x.shape[-1])


def chunk_gated_delta_rule(q, k, v, g, beta):
    B, T, H, Dk = q.shape
    Dv = v.shape[-1]
    C = GDN_CHUNK
    N = T // C

    def chunks(t):
        t = t.astype(jnp.float32).reshape((B, N, C, H) + t.shape[3:])
        return jnp.moveaxis(t, (1, 3), (0, 2))

    qc, kc, vc, bc = chunks(q), chunks(k), chunks(v), chunks(beta)
    gc = jnp.cumsum(chunks(g), axis=-1)
    causal = jnp.tril(jnp.ones((C, C), dtype=bool))
    strict = jnp.tril(jnp.ones((C, C), dtype=bool), -1)
    decay = jnp.exp(jnp.where(causal, gc[..., :, None] - gc[..., None, :], -jnp.inf))
    kb = kc * bc[..., None]
    lower = jnp.where(strict, jnp.einsum("nbhcd,nbhsd->nbhcs", kb, kc) * decay, 0.0)
    rhs = jnp.concatenate([vc * bc[..., None], kb * jnp.exp(gc)[..., None]], axis=-1)
    sol = lax.linalg.triangular_solve(lower, rhs, left_side=True, lower=True, unit_diagonal=True)
    u, w = sol[..., :Dv], sol[..., Dv:]

    def step(S, xs):
        q_i, k_i, u_i, w_i, g_i, d_i = xs
        v_new = u_i - jnp.einsum("bhcd,bhdv->bhcv", w_i, S)
        attn = jnp.einsum("bhcd,bhsd->bhcs", q_i, k_i) * d_i
        o = (jnp.einsum("bhcd,bhdv->bhcv", q_i * jnp.exp(g_i)[..., None], S)
             + jnp.einsum("bhcs,bhsv->bhcv", attn, v_new))
        g_last = g_i[..., -1:]
        S = (S * jnp.exp(g_last)[..., None]
             + jnp.einsum("bhcd,bhcv->bhdv", k_i * jnp.exp(g_last - g_i)[..., None], v_new))
        return S, o

    S0 = jnp.zeros((B, H, Dk, Dv), jnp.float32)
    _, o = lax.scan(step, S0, (qc, kc, u, w, gc, decay))
    return jnp.moveaxis(o, (0, 2), (1, 3)).reshape(B, T, H, Dv)


def gated_deltanet(qkv, z, b_raw, a_raw, conv_w, a_log, dt_bias, norm_g):
    B, T, _ = qkv.shape
    qkv = jax.nn.silu(causal_dwconv(qkv, conv_w))
    q, k, v = jnp.split(qkv, 3, axis=-1)
    hd = (B, T, GDN_HEADS, GDN_HEAD_DIM)
    q = l2_norm(q.reshape(hd)) * (GDN_HEAD_DIM ** -0.5)
    k = l2_norm(k.reshape(hd))
    v = v.reshape(hd)
    beta = jax.nn.sigmoid(b_raw.astype(jnp.float32))
    g = -jnp.exp(a_log.astype(jnp.float32)) * jax.nn.softplus(a_raw.astype(jnp.float32) + dt_bias.astype(jnp.float32))
    o = chunk_gated_delta_rule(q, k, v, g, beta)
    o = rms_norm(o, norm_g) * jax.nn.silu(z.reshape(hd).astype(jnp.float32))
    return o.reshape(B, T, GDN_WIDTH).astype(qkv.dtype)


def native_sparse_attention(q, kc, vc, ks, vs, kw, vw, gate_raw, cos, sin,
                            pe_k, w1_k, w2_k, pe_v, w1_v, w2_v, out_norm_g):
    B, T, _ = q.shape
    Hq, G, dh = NSA_HEADS, NSA_KV_HEADS, NSA_HEAD_DIM
    R = Hq // G
    scale = dh ** -0.5
    q = partial_rope(q.reshape(B, T, Hq, dh), cos, sin).reshape(B, T, G, R, dh)
    heads = lambda t: t.reshape(B, T, G, dh)
    kc = partial_rope(heads(kc), cos, sin)
    ks = partial_rope(heads(ks), cos, sin)
    kw = partial_rope(heads(kw), cos, sin)
    vc, vs, vw = heads(vc), heads(vs), heads(vw)

    n_cmp = (T - CMP_LEN) // CMP_STRIDE + 1
    cmp_idx = np.arange(n_cmp)[:, None] * CMP_STRIDE + np.arange(CMP_LEN)[None, :]

    def compress(t, pe, w1, w2):
        blk = t[:, cmp_idx] + pe[:, None, :]
        blk = jnp.moveaxis(blk, 3, 2).reshape(B, n_cmp, G, CMP_LEN * dh)
        return jax.nn.silu(blk @ w1) @ w2

    k_cmp = compress(kc, pe_k, w1_k, w2_k)
    v_cmp = compress(vc, pe_v, w1_v, w2_v)
    cmp_end = jnp.asarray(cmp_idx[:, -1])

    n_blk = T // SEL_BLOCK
    n_sel = min(N_SELECT, n_blk)
    sel_start = np.arange(n_blk) * SEL_BLOCK
    overlap = jnp.asarray(((cmp_idx[:, :1] < sel_start[None, :] + SEL_BLOCK)
                           & (cmp_idx[:, -1:] >= sel_start[None, :])).astype(np.float32))
    blk_ids = jnp.arange(n_blk)
    ks_t = jnp.moveaxis(ks, 2, 1)
    vs_t = jnp.moveaxis(vs, 2, 1)
    b_ix = jnp.arange(B)[:, None, None, None]
    g_ix = jnp.arange(G)[None, :, None, None]
    kw_pad = jnp.pad(kw, ((0, 0), (WINDOW, 0), (0, 0), (0, 0)))
    vw_pad = jnp.pad(vw, ((0, 0), (WINDOW, 0), (0, 0), (0, 0)))
    gates = jax.nn.sigmoid(gate_raw.astype(jnp.float32)).reshape(B, T, G, R, 3)

    def one_block(i):
        s = i * Q_BLOCK
        t = s + jnp.arange(Q_BLOCK)
        qb = lax.dynamic_slice_in_dim(q, s, Q_BLOCK, axis=1)
        gb = lax.dynamic_slice_in_dim(gates, s, Q_BLOCK, axis=1)
        sc = jnp.einsum("bqgrd,bngd->bgrqn", qb, k_cmp) * scale
        p_cmp = masked_softmax(sc, cmp_end[None, :] <= t[:, None])
        o_cmp = jnp.einsum("bgrqn,bngd->bqgrd", p_cmp.astype(v_cmp.dtype), v_cmp)
        imp = jnp.einsum("bgrqn,nj->bgqj", p_cmp, overlap)
        cur = t[:, None] // SEL_BLOCK
        forced = (blk_ids == 0) | (blk_ids == cur) | (blk_ids == cur - 1)
        valid = blk_ids * SEL_BLOCK <= t[:, None]
        imp = jnp.where(forced, jnp.inf, jnp.where(valid, imp, -jnp.inf))
        _, top = lax.top_k(imp, n_sel)
        tok = (top[..., None] * SEL_BLOCK + jnp.arange(SEL_BLOCK)).reshape(B, G, Q_BLOCK, n_sel * SEL_BLOCK)
        k_sel = ks_t[b_ix, g_ix, tok]
        v_sel = vs_t[b_ix, g_ix, tok]
        ss = jnp.einsum("bqgrd,bgqld->bgrql", qb, k_sel) * scale
        p_sel = masked_softmax(ss, (tok <= t[:, None])[:, :, None])
        o_sel = jnp.einsum("bgrql,bgqld->bqgrd", p_sel.astype(v_sel.dtype), v_sel)
        k_win = lax.dynamic_slice_in_dim(kw_pad, s, WINDOW + Q_BLOCK, axis=1)
        v_win = lax.dynamic_slice_in_dim(vw_pad, s, WINDOW + Q_BLOCK, axis=1)
        kp = s - WINDOW + jnp.arange(WINDOW + Q_BLOCK)
        win_mask = (kp[None, :] <= t[:, None]) & (kp[None, :] > t[:, None] - WINDOW) & (kp[None, :] >= 0)
        sw = jnp.einsum("bqgrd,bkgd->bgrqk", qb, k_win) * scale
        p_win = masked_softmax(sw, win_mask)
        o_win = jnp.einsum("bgrqk,bkgd->bqgrd", p_win.astype(v_win.dtype), v_win)
        o = gb[..., 0:1] * o_cmp + gb[..., 1:2] * o_sel + gb[..., 2:3] * o_win
        o = rms_norm(o, out_norm_g)
        return o.reshape(B, Q_BLOCK, NSA_WIDTH)

    out = lax.map(one_block, jnp.arange(T // Q_BLOCK))
    return jnp.moveaxis(out, 0, 1).reshape(B, T, NSA_WIDTH).astype(q.dtype)


def token_mix(h, w_in, conv_w, a_log, dt_bias, gdn_norm_g, pe_k, w1_k, w2_k,
              pe_v, w1_v, w2_v, nsa_norm_g, w_out, cos, sin):
    (qkv, z, b_raw, a_raw, q_n, kc, vc, ks, vs, kw, vw, g_raw) = split_cols(h @ w_in, IN_SIZES)
    y_gdn = gated_deltanet(qkv, z, b_raw, a_raw, conv_w, a_log, dt_bias, gdn_norm_g)
    y_nsa = native_sparse_attention(q_n, kc, vc, ks, vs, kw, vw, g_raw, cos, sin,
                                    pe_k, w1_k, w2_k, pe_v, w1_v, w2_v, nsa_norm_g)
    return jnp.concatenate([y_gdn, y_nsa.astype(y_gdn.dtype)], axis=-1) @ w_out


def setup_inputs(seed: int = 0) -> dict:
    key = jax.random.key(seed)
    ks = jax.random.split(key, 24)
    nrm = lambda k, shape, s: jax.random.normal(k, shape, jnp.float32) * s
    x = nrm(ks[0], (BATCH, SEQ, D_MODEL), 1.0)
    c = nrm(ks[1], (BATCH, D_MODEL), 1.0)
    positions = (jax.random.randint(ks[2], (BATCH, 1), 0, SEQ, dtype=jnp.int32)
                 + jnp.arange(SEQ, dtype=jnp.int32)[None, :])
    ada_w = nrm(ks[3], (DEPTH, D_MODEL, N_ADA * D_MODEL), 0.5 * D_MODEL ** -0.5)
    ada_b = nrm(ks[4], (DEPTH, N_ADA * D_MODEL), 0.02)
    norm_g = 1.0 + nrm(ks[5], (DEPTH, 3, D_MODEL), 0.02)
    ffn_w_in = nrm(ks[6], (DEPTH, 2, D_MODEL, 2 * D_FF), D_MODEL ** -0.5)
    ffn_w_out = nrm(ks[7], (DEPTH, 2, D_FF, D_MODEL), D_FF ** -0.5)
    mix_w_in = nrm(ks[8], (DEPTH, D_MODEL, IN_WIDTH), D_MODEL ** -0.5)
    gdn_conv_w = nrm(ks[9], (DEPTH, GDN_CONV, 3 * GDN_WIDTH), GDN_CONV ** -0.5)
    gdn_a_log = jnp.log(jax.random.uniform(ks[10], (DEPTH, GDN_HEADS), jnp.float32, 1.0, 16.0))
    dt = jnp.exp(jax.random.uniform(ks[11], (DEPTH, GDN_HEADS), jnp.float32, math.log(1e-3), math.log(1e-1)))
    gdn_dt_bias = dt + jnp.log(-jnp.expm1(-dt))
    gdn_norm_g = 1.0 + nrm(ks[12], (DEPTH, GDN_HEAD_DIM), 0.02)
    cmp_pe_k = nrm(ks[13], (DEPTH, CMP_LEN, NSA_HEAD_DIM), 0.1)
    cmp_w1_k = nrm(ks[14], (DEPTH, CMP_LEN * NSA_HEAD_DIM, CMP_HIDDEN), (CMP_LEN * NSA_HEAD_DIM) ** -0.5)
    cmp_w2_k = nrm(ks[15], (DEPTH, CMP_HIDDEN, NSA_HEAD_DIM), CMP_HIDDEN ** -0.5)
    cmp_pe_v = nrm(ks[16], (DEPTH, CMP_LEN, NSA_HEAD_DIM), 0.1)
    cmp_w1_v = nrm(ks[17], (DEPTH, CMP_LEN * NSA_HEAD_DIM, CMP_HIDDEN), (CMP_LEN * NSA_HEAD_DIM) ** -0.5)
    cmp_w2_v = nrm(ks[18], (DEPTH, CMP_HIDDEN, NSA_HEAD_DIM), CMP_HIDDEN ** -0.5)
    nsa_norm_g = 1.0 + nrm(ks[19], (DEPTH, NSA_HEAD_DIM), 0.02)
    mix_w_out = nrm(ks[20], (DEPTH, MIX_WIDTH, D_MODEL), MIX_WIDTH ** -0.5)
    final_norm_g = 1.0 + nrm(ks[21], (D_MODEL,), 0.02)
    return {"x": x, "c": c, "positions": positions, "ada_w": ada_w, "ada_b": ada_b,
            "norm_g": norm_g, "ffn_w_in": ffn_w_in, "ffn_w_out": ffn_w_out,
            "mix_w_in": mix_w_in, "gdn_conv_w": gdn_conv_w, "gdn_a_log": gdn_a_log,
            "gdn_dt_bias": gdn_dt_bias, "gdn_norm_g": gdn_norm_g,
            "cmp_pe_k": cmp_pe_k, "cmp_w1_k": cmp_w1_k, "cmp_w2_k": cmp_w2_k,
            "cmp_pe_v": cmp_pe_v, "cmp_w1_v": cmp_w1_v, "cmp_w2_v": cmp_w2_v,
            "nsa_norm_g": nsa_norm_g, "mix_w_out": mix_w_out, "final_norm_g": final_norm_g}


def reference(x, c, positions, ada_w, ada_b, norm_g, ffn_w_in, ffn_w_out, mix_w_in,
              gdn_conv_w, gdn_a_log, gdn_dt_bias, gdn_norm_g, cmp_pe_k, cmp_w1_k, cmp_w2_k,
              cmp_pe_v, cmp_w1_v, cmp_w2_v, nsa_norm_g, mix_w_out, final_norm_g):
    B, T, D = x.shape
    half = ROT_DIM // 2
    inv_freq = jnp.power(ROPE_THETA, -jnp.arange(half, dtype=jnp.float32) * (2.0 / ROT_DIM))
    ang = positions.astype(jnp.float32)[..., None] * inv_freq
    cos = jnp.cos(ang)[:, :, None, :]
    sin = jnp.sin(ang)[:, :, None, :]
    cond = jax.nn.silu(c.astype(jnp.float32))
    for l in range(DEPTH):
        mod = (cond @ ada_w[l].astype(jnp.float32) + ada_b[l].astype(jnp.float32)).reshape(B, 3, 3, D).astype(x.dtype)
        h = modulate(rms_norm(x, norm_g[l, 0]), mod[:, 0, 0], mod[:, 0, 1])
        x = x + 0.5 * mod[:, 0, 2][:, None, :] * swiglu(h, ffn_w_in[l, 0], ffn_w_out[l, 0])
        h = modulate(rms_norm(x, norm_g[l, 1]), mod[:, 1, 0], mod[:, 1, 1])
        y = token_mix(h, mix_w_in[l], gdn_conv_w[l], gdn_a_log[l], gdn_dt_bias[l], gdn_norm_g[l],
                      cmp_pe_k[l], cmp_w1_k[l], cmp_w2_k[l], cmp_pe_v[l], cmp_w1_v[l], cmp_w2_v[l],
                      nsa_norm_g[l], mix_w_out[l], cos, sin)
        x = x + mod[:, 1, 2][:, None, :] * y.astype(x.dtype)
        h = modulate(rms_norm(x, norm_g[l, 2]), mod[:, 2, 0], mod[:, 2, 1])
        x = x + 0.5 * mod[:, 2, 2][:, None, :] * swiglu(h, ffn_w_in[l, 1], ffn_w_out[l, 1])
    return rms_norm(x, final_norm_g)
```

```python
import functools
import math

import jax
import jax.numpy as jnp
import numpy as np
from jax import lax
from jax.experimental import pallas as pl
from jax.experimental.pallas import tpu as pltpu

F32 = jnp.float32
BF16 = jnp.bfloat16

NORM_EPS = 1e-6
LANES = 128
GDN_HEADS = 4
GDN_HEAD_DIM = 128
GDN_WIDTH = GDN_HEADS * GDN_HEAD_DIM
GDN_CONV = 4
GDN_CHUNK = 64
GDN_STEP = 2 * GDN_CHUNK
NSA_HEADS = 8
NSA_KV_HEADS = 2
NSA_REP = NSA_HEADS // NSA_KV_HEADS
NSA_HEAD_DIM = 64
NSA_WIDTH = NSA_HEADS * NSA_HEAD_DIM
NSA_KV_WIDTH = NSA_KV_HEADS * NSA_HEAD_DIM
CMP_LEN = 32
CMP_STRIDE = 16
CMP_HIDDEN = 2 * NSA_HEAD_DIM
SEL_BLOCK = 64
N_SELECT = 16
WINDOW = 512
Q_BLOCK = 128
SEL_CHUNK = 512
ROPE_THETA = 500000.0
ROT_DIM = NSA_HEAD_DIM // 4
N_ADA = 9
NEG_BIG = -1e30
VMEM_LIMIT = 56 * 1024 * 1024

HIGHEST = lax.Precision.HIGHEST


def _cparams(sem):
    return pltpu.CompilerParams(dimension_semantics=sem, vmem_limit_bytes=VMEM_LIMIT)


def _dot(a, b):
    return jnp.dot(a, b, preferred_element_type=F32)


def _dot_nt(a, b):
    return lax.dot_general(a, b, (((1,), (1,)), ((), ())), preferred_element_type=F32)


def _dot_hi(a, b):
    return jnp.dot(a, b, preferred_element_type=F32, precision=HIGHEST)


def _split3(x):
    hi = x.astype(BF16)
    r1 = x - hi.astype(F32)
    mid = r1.astype(BF16)
    lo = (r1 - mid.astype(F32)).astype(BF16)
    return hi, mid, lo


def _sigmoid(x):
    return 1.0 / (1.0 + jnp.exp(-x))


def _silu(x):
    return x * _sigmoid(x)


def _rms_rows(x, g):
    ms = jnp.mean(x * x, axis=-1, keepdims=True)
    return x * lax.rsqrt(ms + NORM_EPS) * g


def _ada_body(c_ref, w_ref, b_ref, o_ref):
    cond = _silu(c_ref[...])
    o_ref[0] = _dot_hi(cond, w_ref[0]) + b_ref[0]


def _ada_mod(c_pad, ada_w, ada_b):
    depth, d, n = ada_w.shape
    tn = n // N_ADA
    rows = c_pad.shape[0]
    return pl.pallas_call(
        _ada_body,
        grid=(depth, n // tn),
        in_specs=[
            pl.BlockSpec((rows, d), lambda l, j: (0, 0)),
            pl.BlockSpec((1, d, tn), lambda l, j: (l, 0, j)),
            pl.BlockSpec((1, 1, tn), lambda l, j: (l, 0, j)),
        ],
        out_specs=pl.BlockSpec((1, rows, tn), lambda l, j: (l, 0, j)),
        out_shape=jax.ShapeDtypeStruct((depth, rows, n), F32),
        compiler_params=_cparams(("arbitrary", "arbitrary")),
        name="ada_mod",
    )(c_pad, ada_w, ada_b.reshape(depth, 1, n))


def _rope_body(pos_ref, invf_ref, cos_ref, sin_ref):
    ang = pos_ref[0] * invf_ref[...]
    cos_ref[0] = jnp.cos(ang)
    sin_ref[0] = jnp.sin(ang)


def _rope_tables(pos_f, invf_lane, tm):
    b, t, _ = pos_f.shape
    spec = pl.BlockSpec((1, tm, LANES), lambda i, j: (i, j, 0))
    return pl.pallas_call(
        _rope_body,
        grid=(b, t // tm),
        in_specs=[pl.BlockSpec((1, tm, 1), lambda i, j: (i, j, 0)),
                  pl.BlockSpec((1, LANES), lambda i, j: (0, 0))],
        out_specs=[spec, spec],
        out_shape=[jax.ShapeDtypeStruct((b, t, LANES), F32)] * 2,
        compiler_params=_cparams(("arbitrary", "arbitrary")),
        name="rope_tables",
    )(pos_f, invf_lane)


def _ffn_body(x_ref, mod_ref, g_ref, wg_ref, wu_ref, wo_ref, fg_ref, o_ref,
              h_scr, acc_scr, *, final_norm):
    f = pl.program_id(2)

    @pl.when(f == 0)
    def _():
        y = _rms_rows(x_ref[0], g_ref[...])
        h = y * (1.0 + mod_ref[0, 1:2, :]) + mod_ref[0, 0:1, :]
        h_scr[...] = h.astype(BF16)
        acc_scr[...] = jnp.zeros_like(acc_scr)

    h = h_scr[...]
    a = _dot(h, wg_ref[...])
    u = _dot(h, wu_ref[...])
    act = (_silu(a) * u).astype(BF16)
    acc_scr[...] += _dot(act, wo_ref[...])

    @pl.when(f == pl.num_programs(2) - 1)
    def _():
        out = x_ref[0] + (0.5 * mod_ref[0, 2:3, :]) * acc_scr[...]
        if final_norm:
            out = _rms_rows(out, fg_ref[...])
        o_ref[0] = out


def _ffn(x, mod8, g, wg, wu, wo, fg, *, final_norm, tm=1024, tf=256):
    b, t, d = x.shape
    ff = wg.shape[1]
    return pl.pallas_call(
        functools.partial(_ffn_body, final_norm=final_norm),
        grid=(b, t // tm, ff // tf),
        in_specs=[
            pl.BlockSpec((1, tm, d), lambda i, j, k: (i, j, 0)),
            pl.BlockSpec((1, 8, d), lambda i, j, k: (i, 0, 0)),
            pl.BlockSpec((1, d), lambda i, j, k: (0, 0)),
            pl.BlockSpec((d, tf), lambda i, j, k: (0, k)),
            pl.BlockSpec((d, tf), lambda i, j, k: (0, k)),
            pl.BlockSpec((tf, d), lambda i, j, k: (k, 0)),
            pl.BlockSpec((1, d), lambda i, j, k: (0, 0)),
        ],
        out_specs=pl.BlockSpec((1, tm, d), lambda i, j, k: (i, j, 0)),
        out_shape=jax.ShapeDtypeStruct((b, t, d), F32),
        scratch_shapes=[pltpu.VMEM((tm, d), BF16), pltpu.VMEM((tm, d), F32)],
        compiler_params=_cparams(("arbitrary", "arbitrary", "arbitrary")),
        name="ffn_final" if final_norm else "ffn",
    )(x, mod8, g, wg, wu, wo, fg)


_O_QKV, _O_Z, _O_Q = 0, 1536, 2048
_O_KC, _O_VC, _O_KS, _O_VS, _O_KW, _O_VW, _O_SM = 2560, 2688, 2816, 2944, 3072, 3200, 3328
_IN_COLS = 3456
_SMALL_ROWS = 32


def _rope_tile(x, cosf, sint, first_half):
    fwd = pltpu.roll(x, 8, axis=1)
    bwd = pltpu.roll(x, LANES - 8, axis=1)
    return x * cosf + jnp.where(first_half, -bwd, fwd) * sint


def _inproj_body(x_ref, mod_ref, g_ref, w_ref, wst_ref, cos_ref, sin_ref,
                 qkv_ref, z_ref, ba_ref, bat_ref, qpad_ref,
                 kc_ref, vc_ref, ks_ref, vs_ref, kw_ref, vw_ref):
    y = _rms_rows(x_ref[0], g_ref[...])
    h = (y * (1.0 + mod_ref[0, 1:2, :]) + mod_ref[0, 0:1, :]).astype(BF16)
    cosf = cos_ref[0]
    sint = sin_ref[0]
    lane = lax.broadcasted_iota(jnp.int32, cosf.shape, 1)
    first_half = (lane % NSA_HEAD_DIM) < (ROT_DIM // 2)
    low = lane < NSA_HEAD_DIM

    for j in range(3):
        qkv_ref[0, :, j * 512:(j + 1) * 512] = _dot(h, w_ref[:, _O_QKV + j * 512:_O_QKV + (j + 1) * 512])
    z_ref[0] = _dot(h, w_ref[:, _O_Z:_O_Z + 512])
    ba_ref[0] = _dot(h, w_ref[:, _O_SM:_O_SM + LANES])
    bat_ref[0] = _dot_nt(wst_ref[...], h)

    scale = NSA_HEAD_DIM ** -0.5
    for k in range(NSA_HEADS // 2):
        tile = _dot(h, w_ref[:, _O_Q + k * LANES:_O_Q + (k + 1) * LANES])
        tile = _rope_tile(tile, cosf, sint, first_half) * scale
        swapped = pltpu.roll(tile, NSA_HEAD_DIM, axis=1)
        grp = (2 * k) // NSA_REP
        if grp == 0:
            even = jnp.where(low, tile, 0.0)
            odd = jnp.where(low, swapped, 0.0)
        else:
            even = jnp.where(low, 0.0, swapped)
            odd = jnp.where(low, 0.0, tile)
        qpad_ref[0, 2 * k] = even.astype(BF16)
        qpad_ref[0, 2 * k + 1] = odd.astype(BF16)

    kc = _dot(h, w_ref[:, _O_KC:_O_KC + LANES])
    kc_ref[0] = _rope_tile(kc, cosf, sint, first_half)
    vc_ref[0] = _dot(h, w_ref[:, _O_VC:_O_VC + LANES])
    ks = _dot(h, w_ref[:, _O_KS:_O_KS + LANES])
    ks_ref[0] = _rope_tile(ks, cosf, sint, first_half).astype(BF16)
    vs_ref[0] = _dot(h, w_ref[:, _O_VS:_O_VS + LANES]).astype(BF16)
    kw = _dot(h, w_ref[:, _O_KW:_O_KW + LANES])
    kw_ref[0] = _rope_tile(kw, cosf, sint, first_half).astype(BF16)
    vw_ref[0] = _dot(h, w_ref[:, _O_VW:_O_VW + LANES]).astype(BF16)


def _inproj(x, mod8, g, w, wst, cosf, sint, tm=512):
    b, t, d = x.shape
    row = lambda n: pl.BlockSpec((1, tm, n), lambda i, j: (i, j, 0))
    sds = lambda n, dt: jax.ShapeDtypeStruct((b, t, n), dt)
    return pl.pallas_call(
        _inproj_body,
        grid=(b, t // tm),
        in_specs=[
            row(d),
            pl.BlockSpec((1, 8, d), lambda i, j: (i, 0, 0)),
            pl.BlockSpec((1, d), lambda i, j: (0, 0)),
            pl.BlockSpec((d, _IN_COLS), lambda i, j: (0, 0)),
            pl.BlockSpec((_SMALL_ROWS, d), lambda i, j: (0, 0)),
            row(LANES), row(LANES),
        ],
        out_specs=[
            row(3 * GDN_WIDTH), row(GDN_WIDTH), row(LANES),
            pl.BlockSpec((1, _SMALL_ROWS, tm), lambda i, j: (i, 0, j)),
            pl.BlockSpec((1, NSA_HEADS, tm, LANES), lambda i, j: (i, 0, j, 0)),
            row(LANES), row(LANES), row(LANES), row(LANES), row(LANES), row(LANES),
        ],
        out_shape=[
            sds(3 * GDN_WIDTH, F32), sds(GDN_WIDTH, F32), sds(LANES, F32),
            jax.ShapeDtypeStruct((b, _SMALL_ROWS, t), F32),
            jax.ShapeDtypeStruct((b, NSA_HEADS, t, LANES), BF16),
            sds(LANES, F32), sds(LANES, F32),
            sds(LANES, BF16), sds(LANES, BF16), sds(LANES, BF16), sds(LANES, BF16),
        ],
        compiler_params=_cparams(("arbitrary", "arbitrary")),
        name="mix_inproj",
    )(x, mod8, g, w, wst, cosf, sint)


def _softplus(x):
    return jnp.maximum(x, 0.0) + jnp.log1p(jnp.exp(-jnp.abs(x)))


def _unit_lower_inverse(low):
    n = low.shape[0]
    r = lax.broadcasted_iota(jnp.int32, (n, n), 0)
    c = lax.broadcasted_iota(jnp.int32, (n, n), 1)
    p = -low
    inv = jnp.where(r == c, 1.0, 0.0) + p
    for _ in range(int(math.log2(n)) - 1):
        p = _dot_hi(p, p)
        inv = inv + _dot_hi(inv, p)
    return inv


def _gdn_body(qkv_ref, z_ref, ba_ref, bat_ref, cw_ref, av_ref, avt_ref, ng_ref,
              o_ref, buf, s_scr):
    step = pl.program_id(1)
    c64 = GDN_CHUNK
    hd = GDN_HEAD_DIM

    @pl.when(step == 0)
    def _():
        buf[0:8, :] = jnp.zeros((8, 3 * GDN_WIDTH), F32)
        s_scr[...] = jnp.zeros_like(s_scr)

    x = qkv_ref[0]
    buf[8:8 + GDN_STEP, :] = x
    y = x * cw_ref[3:4, :]
    for j in range(GDN_CONV - 1):
        sh = GDN_CONV - 1 - j
        y = y + buf[8 - sh:8 - sh + GDN_STEP, :] * cw_ref[j:j + 1, :]
    buf[0:8, :] = x[GDN_STEP - 8:GDN_STEP, :]
    y = _silu(y)

    ba = ba_ref[0]
    beta_all = _sigmoid(ba)
    g_col_all = av_ref[0:1, :] * _softplus(ba + av_ref[1:2, :])
    bat = bat_ref[0]
    g_row_all = avt_ref[:, 0:1] * _softplus(bat + avt_ref[:, 1:2])

    r64 = lax.broadcasted_iota(jnp.int32, (c64, c64), 0)
    k64 = lax.broadcasted_iota(jnp.int32, (c64, c64), 1)
    causal = r64 >= k64
    strict = r64 > k64
    tri = jnp.where(causal, 1.0, 0.0)
    tri_t = jnp.where(r64 <= k64, 1.0, 0.0)

    for ch in range(GDN_STEP // c64):
        rows = slice(ch * c64, (ch + 1) * c64)
        gc_col_all = _dot_hi(tri, g_col_all[rows, :])
        gc_row_all = _dot_hi(g_row_all[:, rows], tri_t)
        for h in range(GDN_HEADS):
            qh = y[rows, h * hd:(h + 1) * hd]
            kh = y[rows, GDN_WIDTH + h * hd:GDN_WIDTH + (h + 1) * hd]
            vh = y[rows, 2 * GDN_WIDTH + h * hd:2 * GDN_WIDTH + (h + 1) * hd]
            qh = qh * lax.rsqrt(jnp.sum(qh * qh, axis=-1, keepdims=True) + NORM_EPS) * (hd ** -0.5)
            kh = kh * lax.rsqrt(jnp.sum(kh * kh, axis=-1, keepdims=True) + NORM_EPS)
            bcol = beta_all[rows, h:h + 1]
            gcol = gc_col_all[:, GDN_HEADS + h:GDN_HEADS + h + 1]
            grow = gc_row_all[GDN_HEADS + h:GDN_HEADS + h + 1, :]
            decay = jnp.exp(jnp.where(causal, gcol - grow, -jnp.inf))
            kb = kh * bcol
            khb = kh.astype(BF16)
            low = jnp.where(strict, _dot_nt(kb.astype(BF16), khb) * decay, 0.0)
            inv = _unit_lower_inverse(low)
            eg = jnp.exp(gcol)
            rhs = jnp.concatenate([vh * bcol, kb * eg], axis=1)
            sol = _dot_hi(inv, rhs)
            u = sol[:, :hd]
            w = sol[:, hd:]
            s_old = s_scr[h]
            s_b = s_old.astype(BF16)
            v_new = u - _dot(w.astype(BF16), s_b)
            attn = _dot_nt(qh.astype(BF16), khb) * decay
            vnb = v_new.astype(BF16)
            o = _dot((qh * eg).astype(BF16), s_b) + _dot(attn.astype(BF16), vnb)
            glast = grow[:, c64 - 1:c64]
            kd_t = (kh * jnp.exp(glast - gcol)).T.astype(BF16)
            s_scr[h] = s_old * jnp.exp(glast) + _dot(kd_t, vnb)
            on = _rms_rows(o, ng_ref[...])
            zh = z_ref[0, rows, h * hd:(h + 1) * hd]
            o_ref[0, rows, h * hd:(h + 1) * hd] = (on * _silu(zh)).astype(o_ref.dtype)


def _gdn(qkv, z, ba, bat, conv_w, avec, avect, norm_g):
    b, t, _ = qkv.shape
    row = lambda n: pl.BlockSpec((1, GDN_STEP, n), lambda i, j: (i, j, 0))
    return pl.pallas_call(
        _gdn_body,
        grid=(b, t // GDN_STEP),
        in_specs=[
            row(3 * GDN_WIDTH), row(GDN_WIDTH), row(LANES),
            pl.BlockSpec((1, _SMALL_ROWS, GDN_STEP), lambda i, j: (i, 0, j)),
            pl.BlockSpec((GDN_CONV, 3 * GDN_WIDTH), lambda i, j: (0, 0)),
            pl.BlockSpec((8, LANES), lambda i, j: (0, 0)),
            pl.BlockSpec((_SMALL_ROWS, LANES), lambda i, j: (0, 0)),
            pl.BlockSpec((1, GDN_HEAD_DIM), lambda i, j: (0, 0)),
        ],
        out_specs=row(GDN_WIDTH),
        out_shape=jax.ShapeDtypeStruct((b, t, GDN_WIDTH), BF16),
        scratch_shapes=[pltpu.VMEM((8 + GDN_STEP, 3 * GDN_WIDTH), F32),
                        pltpu.VMEM((GDN_HEADS, GDN_HEAD_DIM, GDN_HEAD_DIM), F32)],
        compiler_params=_cparams(("arbitrary", "arbitrary")),
        name="gdn",
    )(qkv, z, ba, bat, conv_w, avec, avect, norm_g)


def _cmp_body(x_ref, pea_ref, peb_ref, wa_ref, wb_ref, w2_ref, o_ref):
    x = x_ref[0]
    a = _dot((x + pea_ref[...]).astype(BF16), wa_ref[...])
    bm = _dot((x + peb_ref[...]).astype(BF16), wb_ref[...])
    n = a.shape[0]
    h1 = a + pltpu.roll(bm, n - 1, axis=0)
    o_ref[0] = _dot(_silu(h1).astype(BF16), w2_ref[...]).astype(o_ref.dtype)


def _compress(x2, pea, peb, wa, wb, w2bd):
    b, n, wdt = x2.shape
    full = lambda s: pl.BlockSpec(s, lambda i: (0,) * len(s))
    return pl.pallas_call(
        _cmp_body,
        grid=(b,),
        in_specs=[pl.BlockSpec((1, n, wdt), lambda i: (i, 0, 0)),
                  full(pea.shape), full(peb.shape), full(wa.shape), full(wb.shape), full(w2bd.shape)],
        out_specs=pl.BlockSpec((1, n, LANES), lambda i: (i, 0, 0)),
        out_shape=jax.ShapeDtypeStruct((b, n, LANES), BF16),
        compiler_params=_cparams(("arbitrary",)),
        name="nsa_compress",
    )(x2, pea, peb, wa, wb, w2bd)


def _nsa_body(q_ref, kcmp_ref, vcmp_ref, ks_ref, vs_ref, kw_ref, vw_ref, gate_ref,
              ovl_ref, ng_ref, o_ref, *, seq):
    qb = pl.program_id(1)
    s0 = qb * Q_BLOCK
    n_cmp_rows = kcmp_ref.shape[1]
    n_blk = seq // SEL_BLOCK
    rq = NSA_REP * Q_BLOCK

    t_q = s0 + lax.broadcasted_iota(jnp.int32, (Q_BLOCK, 1), 0)
    lane = lax.broadcasted_iota(jnp.int32, (Q_BLOCK, LANES), 1)
    gates = _sigmoid(gate_ref[0])

    n_iota = lax.broadcasted_iota(jnp.int32, (Q_BLOCK, n_cmp_rows), 1)
    cmask = (n_iota * CMP_STRIDE + (CMP_LEN - 1)) <= t_q

    wstart = pl.multiple_of(jnp.maximum(s0 - WINDOW, 0), Q_BLOCK)
    kp = wstart + lax.broadcasted_iota(jnp.int32, (Q_BLOCK, WINDOW + Q_BLOCK), 1)
    wmask = (kp <= t_q) & (kp > t_q - WINDOW)

    cur = lax.shift_right_logical(t_q, 6)
    forced = (lane == 0) | (lane == cur) | (lane == cur - 1)
    valid = (lane * SEL_BLOCK <= t_q) & (lane < n_blk)
    jrow = lax.broadcasted_iota(jnp.int32, (n_blk, Q_BLOCK), 0)

    n_chunks = qb // (SEL_CHUNK // Q_BLOCK) + 1

    for g in range(NSA_KV_HEADS):
        q = q_ref[0, g * NSA_REP:(g + 1) * NSA_REP].reshape(rq, LANES)
        lmask = (lane >= g * NSA_HEAD_DIM) & (lane < (g + 1) * NSA_HEAD_DIM)

        sc = _dot_nt(q, kcmp_ref[0]).reshape(NSA_REP, Q_BLOCK, n_cmp_rows)
        sc = jnp.where(cmask[None], sc, -jnp.inf)
        m = jnp.max(sc, axis=-1, keepdims=True)
        m = jnp.where(m == -jnp.inf, 0.0, m)
        p = jnp.exp(sc - m)
        p = p / jnp.maximum(jnp.sum(p, axis=-1, keepdims=True), 1e-30)
        o_cmp = _dot(p.reshape(rq, n_cmp_rows).astype(BF16), vcmp_ref[0])
        psum = p[0] + p[1] + p[2] + p[3]
        hi, mid, lo = _split3(psum)
        ovl = ovl_ref[...]
        imp = _dot(hi, ovl) + _dot(mid, ovl) + _dot(lo, ovl)

        key = jnp.where(forced, jnp.inf, jnp.where(valid, imp, -jnp.inf))
        key_t = key.T[:n_blk, :]
        cnt = jnp.zeros((n_blk, Q_BLOCK), F32)
        for i in range(n_blk):
            row = key_t[i:i + 1, :]
            ge = jnp.where(row >= key_t, 1.0, 0.0)
            gt = jnp.where(row > key_t, 1.0, 0.0)
            cnt = cnt + jnp.where(jrow > i, ge, gt)
        sel_t = jnp.where(cnt < float(N_SELECT), 1.0, 0.0)
        if n_blk < LANES:
            sel_t = jnp.concatenate([sel_t, jnp.zeros((LANES - n_blk, Q_BLOCK), F32)], axis=0)
        sel = sel_t.T.astype(BF16)

        def sel_step(c, carry):
            m_i, l_i, acc = carry
            k0 = pl.multiple_of(c * SEL_CHUNK, SEL_CHUNK)
            kch = ks_ref[0, pl.ds(k0, SEL_CHUNK), :]
            vch = vs_ref[0, pl.ds(k0, SEL_CHUNK), :]
            s = _dot_nt(q, kch).reshape(NSA_REP, Q_BLOCK, SEL_CHUNK)
            eb = lax.broadcasted_iota(jnp.int32, (LANES, SEL_CHUNK), 0)
            ek = lax.broadcasted_iota(jnp.int32, (LANES, SEL_CHUNK), 1)
            expand = jnp.where(eb == c * (SEL_CHUNK // SEL_BLOCK) + ek // SEL_BLOCK, 1.0, 0.0).astype(BF16)
            chosen = _dot(sel, expand) > 0.5
            kidx = k0 + lax.broadcasted_iota(jnp.int32, (Q_BLOCK, SEL_CHUNK), 1)
            mask = chosen & (kidx <= t_q)
            s = jnp.where(mask[None], s, NEG_BIG)
            m_new = jnp.maximum(m_i, jnp.max(s, axis=-1, keepdims=True))
            alpha = jnp.exp(m_i - m_new)
            pch = jnp.exp(s - m_new)
            l_new = alpha * l_i + jnp.sum(pch, axis=-1, keepdims=True)
            pv = _dot(pch.reshape(rq, SEL_CHUNK).astype(BF16), vch).reshape(NSA_REP, Q_BLOCK, LANES)
            return m_new, l_new, alpha * acc + pv

        init = (jnp.full((NSA_REP, Q_BLOCK, 1), NEG_BIG, F32),
                jnp.zeros((NSA_REP, Q_BLOCK, 1), F32),
                jnp.zeros((NSA_REP, Q_BLOCK, LANES), F32))
        _, l_s, acc_s = lax.fori_loop(0, n_chunks, sel_step, init)
        o_sel = acc_s / l_s

        kwin = kw_ref[0, pl.ds(wstart, WINDOW + Q_BLOCK), :]
        vwin = vw_ref[0, pl.ds(wstart, WINDOW + Q_BLOCK), :]
        sw = _dot_nt(q, kwin).reshape(NSA_REP, Q_BLOCK, WINDOW + Q_BLOCK)
        sw = jnp.where(wmask[None], sw, NEG_BIG)
        pw = jnp.exp(sw - jnp.max(sw, axis=-1, keepdims=True))
        pw = pw / jnp.sum(pw, axis=-1, keepdims=True)
        o_win = _dot(pw.reshape(rq, WINDOW + Q_BLOCK).astype(BF16), vwin)

        o_cmp = o_cmp.reshape(NSA_REP, Q_BLOCK, LANES)
        o_win = o_win.reshape(NSA_REP, Q_BLOCK, LANES)

        ys = []
        for r in range(NSA_REP):
            col = 2 * GDN_HEADS + (g * NSA_REP + r) * 3
            o = (gates[:, col:col + 1] * o_cmp[r] + gates[:, col + 1:col + 2] * o_sel[r]
                 + gates[:, col + 2:col + 3] * o_win[r])
            o = jnp.where(lmask, o, 0.0)
            ms = jnp.sum(o * o, axis=-1, keepdims=True) * (1.0 / NSA_HEAD_DIM)
            ys.append(o * lax.rsqrt(ms + NORM_EPS) * ng_ref[...])
        for pair in range(NSA_REP // 2):
            a, bb = ys[2 * pair], ys[2 * pair + 1]
            if g == 0:
                tile = a + pltpu.roll(bb, NSA_HEAD_DIM, axis=1)
            else:
                tile = pltpu.roll(a, NSA_HEAD_DIM, axis=1) + bb
            c0 = g * NSA_REP * NSA_HEAD_DIM + pair * LANES
            o_ref[0, :, c0:c0 + LANES] = tile.astype(o_ref.dtype)


def _nsa(qpad, kcmp, vcmp, ks, vs, kw, vw, ba, ovl, ng_lane):
    b, _, t, _ = qpad.shape
    n = kcmp.shape[1]
    whole = lambda rows: pl.BlockSpec((1, rows, LANES), lambda i, j: (i, 0, 0))
    return pl.pallas_call(
        functools.partial(_nsa_body, seq=t),
        grid=(b, t // Q_BLOCK),
        in_specs=[
            pl.BlockSpec((1, NSA_HEADS, Q_BLOCK, LANES), lambda i, j: (i, 0, j, 0)),
            whole(n), whole(n), whole(t), whole(t), whole(t), whole(t),
            pl.BlockSpec((1, Q_BLOCK, LANES), lambda i, j: (i, j, 0)),
            pl.BlockSpec((n, LANES), lambda i, j: (0, 0)),
            pl.BlockSpec((1, LANES), lambda i, j: (0, 0)),
        ],
        out_specs=pl.BlockSpec((1, Q_BLOCK, NSA_WIDTH), lambda i, j: (i, j, 0)),
        out_shape=jax.ShapeDtypeStruct((b, t, NSA_WIDTH), BF16),
        compiler_params=_cparams(("arbitrary", "arbitrary")),
        name="nsa_attn",
    )(qpad, kcmp, vcmp, ks, vs, kw, vw, ba, ovl, ng_lane)


def _outproj_body(x_ref, mod_ref, yg_ref, yn_ref, w_ref, o_ref):
    y = _dot(yg_ref[0], w_ref[0:GDN_WIDTH, :]) + _dot(yn_ref[0], w_ref[GDN_WIDTH:, :])
    o_ref[0] = x_ref[0] + mod_ref[0, 2:3, :] * y


def _outproj(x, mod8, yg, yn, w, tm=1024):
    b, t, d = x.shape
    row = lambda n: pl.BlockSpec((1, tm, n), lambda i, j: (i, j, 0))
    return pl.pallas_call(
        _outproj_body,
        grid=(b, t // tm),
        in_specs=[row(d), pl.BlockSpec((1, 8, d), lambda i, j: (i, 0, 0)),
                  row(GDN_WIDTH), row(NSA_WIDTH),
                  pl.BlockSpec(w.shape, lambda i, j: (0, 0))],
        out_specs=row(d),
        out_shape=jax.ShapeDtypeStruct((b, t, d), F32),
        compiler_params=_cparams(("arbitrary", "arbitrary")),
        name="mix_outproj",
    )(x, mod8, yg, yn, w)


_IN_SIZES = (3 * GDN_WIDTH, GDN_WIDTH, GDN_HEADS, GDN_HEADS, NSA_WIDTH) + (NSA_KV_WIDTH,) * 6 + (3 * NSA_HEADS,)


def _prep_inproj_weight(w):
    offs = np.concatenate([[0], np.cumsum(_IN_SIZES)])
    seg = lambda i: w[:, int(offs[i]):int(offs[i + 1])]
    small = jnp.concatenate([seg(2), seg(3), seg(11)], axis=1)
    big = jnp.concatenate([seg(0), seg(1)] + [seg(i) for i in range(4, 11)]
                          + [small, jnp.zeros((w.shape[0], LANES - _SMALL_ROWS), w.dtype)], axis=1)
    return big.astype(BF16), small.T.astype(BF16)


def _prep_compress(pe, w1, w2):
    hid = CMP_HIDDEN
    g, dh = NSA_KV_HEADS, NSA_HEAD_DIM
    half = CMP_LEN // 2
    w1r = w1.reshape(CMP_LEN, dh, hid)
    eye = jnp.eye(g, dtype=w1.dtype)
    expand = lambda part: jnp.einsum("ldh,gk->lgdkh", part, eye).reshape(half * g * dh, g * hid)
    wa, wb = expand(w1r[:half]), expand(w1r[half:])
    pe_row = lambda part: jnp.broadcast_to(part[:, None, :], (half, g, dh)).reshape(1, half * g * dh)
    pea, peb = pe_row(pe[:half]), pe_row(pe[half:])
    w2bd = jnp.einsum("hd,gk->ghkd", w2, eye).reshape(g * hid, g * dh)
    return pea, peb, wa.astype(BF16), wb.astype(BF16), w2bd.astype(BF16)


def _overlap_matrix(n_rows, seq):
    n_cmp = (seq - CMP_LEN) // CMP_STRIDE + 1
    n_blk = seq // SEL_BLOCK
    n = np.arange(n_rows)[:, None]
    j = np.arange(LANES)[None, :]
    start, end = n * CMP_STRIDE, n * CMP_STRIDE + CMP_LEN - 1
    ovl = (start < j * SEL_BLOCK + SEL_BLOCK) & (end >= j * SEL_BLOCK) & (n < n_cmp) & (j < n_blk)
    return jnp.asarray(ovl.astype(np.float32), dtype=BF16)


def kernel(x, c, positions, ada_w, ada_b, norm_g, ffn_w_in, ffn_w_out, mix_w_in, gdn_conv_w, gdn_a_log, gdn_dt_bias, gdn_norm_g, cmp_pe_k, cmp_w1_k, cmp_w2_k, cmp_pe_v, cmp_w1_v, cmp_w2_v, nsa_norm_g, mix_w_out, final_norm_g):
    b, t, d = x.shape
    depth = ada_w.shape[0]
    d_ff = ffn_w_out.shape[2]
    assert t % (SEL_CHUNK * 4) == 0 and b <= 8

    c_pad = jnp.zeros((8, d), F32).at[:b].set(c.astype(F32))
    mod = _ada_mod(c_pad, ada_w.astype(F32), ada_b.astype(F32))
    mod = mod[:, :b].reshape(depth, b, 3, 3, d)
    mod8 = jnp.concatenate([mod, jnp.zeros((depth, b, 3, 5, d), F32)], axis=3)

    half = ROT_DIM // 2
    inv_freq = jnp.power(ROPE_THETA, -jnp.arange(half, dtype=F32) * (2.0 / ROT_DIM))
    dim = np.arange(LANES) % NSA_HEAD_DIM
    invf_lane = jnp.where(jnp.asarray(dim < ROT_DIM), inv_freq[dim % half], 0.0).reshape(1, LANES)
    cosf, sint = _rope_tables(positions.astype(F32).reshape(b, t, 1), invf_lane, tm=1024)

    n_rows = t // CMP_STRIDE
    ovl = _overlap_matrix(n_rows, t)
    fg = final_norm_g.reshape(1, d).astype(F32)

    for l in range(depth):
        wg = [ffn_w_in[l, i, :, :d_ff].astype(BF16) for i in range(2)]
        wu = [ffn_w_in[l, i, :, d_ff:].astype(BF16) for i in range(2)]
        wo = [ffn_w_out[l, i].astype(BF16) for i in range(2)]
        ng = lambda i: norm_g[l, i].reshape(1, d).astype(F32)

        x = _ffn(x, mod8[l, :, 0], ng(0), wg[0], wu[0], wo[0], fg, final_norm=False)

        w_big, w_small_t = _prep_inproj_weight(mix_w_in[l])
        (qkv, z, ba, bat, qpad, kc, vc, ks, vs, kw, vw) = _inproj(
            x, mod8[l, :, 1], ng(1), w_big, w_small_t, cosf, sint)

        neg_a = -jnp.exp(gdn_a_log[l].astype(F32))
        dtb = gdn_dt_bias[l].astype(F32)
        avec = jnp.zeros((8, LANES), F32).at[0, GDN_HEADS:2 * GDN_HEADS].set(neg_a)
        avec = avec.at[1, GDN_HEADS:2 * GDN_HEADS].set(dtb)
        avect = jnp.zeros((_SMALL_ROWS, LANES), F32).at[GDN_HEADS:2 * GDN_HEADS, 0].set(neg_a)
        avect = avect.at[GDN_HEADS:2 * GDN_HEADS, 1].set(dtb)
        y_gdn = _gdn(qkv, z, ba, bat, gdn_conv_w[l].astype(F32), avec, avect,
                     gdn_norm_g[l].reshape(1, GDN_HEAD_DIM).astype(F32))

        kcmp = _compress(kc.reshape(b, n_rows, CMP_STRIDE * LANES),
                         *_prep_compress(cmp_pe_k[l], cmp_w1_k[l], cmp_w2_k[l]))
        vcmp = _compress(vc.reshape(b, n_rows, CMP_STRIDE * LANES),
                         *_prep_compress(cmp_pe_v[l], cmp_w1_v[l], cmp_w2_v[l]))
        ng_lane = jnp.tile(nsa_norm_g[l].astype(F32), NSA_KV_HEADS).reshape(1, LANES)
        y_nsa = _nsa(qpad, kcmp, vcmp, ks, vs, kw, vw, ba, ovl, ng_lane)

        x = _outproj(x, mod8[l, :, 1], y_gdn, y_nsa, mix_w_out[l].astype(BF16))

        x = _ffn(x, mod8[l, :, 2], ng(2), wg[1], wu[1], wo[1], fg, final_norm=(l == depth - 1))
    return x
```

```python
import functools
import math

import jax
import jax.numpy as jnp
import numpy as np
from jax import lax
from jax.experimental import pallas as pl
from jax.experimental.pallas import tpu as pltpu

F32 = jnp.float32
BF16 = jnp.bfloat16

NORM_EPS = 1e-6
LANES = 128
GDN_HEADS = 4
GDN_HEAD_DIM = 128
GDN_WIDTH = GDN_HEADS * GDN_HEAD_DIM
GDN_CONV = 4
GDN_CHUNK = 64
GDN_PREP_STEP = 4 * GDN_CHUNK
GDN_SCAN_STEP = 2 * GDN_CHUNK
NSA_HEADS = 8
NSA_KV_HEADS = 2
NSA_REP = NSA_HEADS // NSA_KV_HEADS
NSA_HEAD_DIM = 64
NSA_WIDTH = NSA_HEADS * NSA_HEAD_DIM
NSA_KV_WIDTH = NSA_KV_HEADS * NSA_HEAD_DIM
CMP_LEN = 32
CMP_STRIDE = 16
CMP_HIDDEN = 2 * NSA_HEAD_DIM
SEL_BLOCK = 64
N_SELECT = 16
WINDOW = 512
Q_BLOCK = 128
SEL_CHUNK = 512
ROPE_THETA = 500000.0
ROT_DIM = NSA_HEAD_DIM // 4
N_ADA = 9
NEG_BIG = -1e30
VMEM_LIMIT = 56 * 1024 * 1024

HIGHEST = lax.Precision.HIGHEST


def _cparams(sem):
    return pltpu.CompilerParams(dimension_semantics=sem, vmem_limit_bytes=VMEM_LIMIT)


def _dot(a, b):
    return jnp.dot(a, b, preferred_element_type=F32)


def _dot_nt(a, b):
    return lax.dot_general(a, b, (((1,), (1,)), ((), ())), preferred_element_type=F32)


def _dot_hi(a, b):
    return jnp.dot(a, b, preferred_element_type=F32, precision=HIGHEST)


def _split3(x):
    hi = x.astype(BF16)
    r1 = x - hi.astype(F32)
    mid = r1.astype(BF16)
    lo = (r1 - mid.astype(F32)).astype(BF16)
    return hi, mid, lo


def _sigmoid(x):
    return 1.0 / (1.0 + jnp.exp(-x))


def _silu(x):
    return x * _sigmoid(x)


def _rms_rows(x, g):
    ms = jnp.mean(x * x, axis=-1, keepdims=True)
    return x * lax.rsqrt(ms + NORM_EPS) * g


def _ada_body(c_ref, w_ref, b_ref, o_ref):
    cond = _silu(c_ref[...])
    o_ref[0] = _dot_hi(cond, w_ref[0]) + b_ref[0]


def _ada_mod(c_pad, ada_w, ada_b):
    depth, d, n = ada_w.shape
    tn = n // N_ADA
    rows = c_pad.shape[0]
    return pl.pallas_call(
        _ada_body,
        grid=(depth, n // tn),
        in_specs=[
            pl.BlockSpec((rows, d), lambda l, j: (0, 0)),
            pl.BlockSpec((1, d, tn), lambda l, j: (l, 0, j)),
            pl.BlockSpec((1, 1, tn), lambda l, j: (l, 0, j)),
        ],
        out_specs=pl.BlockSpec((1, rows, tn), lambda l, j: (l, 0, j)),
        out_shape=jax.ShapeDtypeStruct((depth, rows, n), F32),
        compiler_params=_cparams(("arbitrary", "arbitrary")),
        name="ada_mod",
    )(c_pad, ada_w, ada_b.reshape(depth, 1, n))


def _rope_body(pos_ref, invf_ref, cos_ref, sin_ref):
    ang = pos_ref[0] * invf_ref[...]
    cos_ref[0] = jnp.cos(ang)
    sin_ref[0] = jnp.sin(ang)


def _rope_tables(pos_f, invf_lane, tm):
    b, t, _ = pos_f.shape
    spec = pl.BlockSpec((1, tm, LANES), lambda i, j: (i, j, 0))
    return pl.pallas_call(
        _rope_body,
        grid=(b, t // tm),
        in_specs=[pl.BlockSpec((1, tm, 1), lambda i, j: (i, j, 0)),
                  pl.BlockSpec((1, LANES), lambda i, j: (0, 0))],
        out_specs=[spec, spec],
        out_shape=[jax.ShapeDtypeStruct((b, t, LANES), F32)] * 2,
        compiler_params=_cparams(("arbitrary", "arbitrary")),
        name="rope_tables",
    )(pos_f, invf_lane)


def _ffn_body(x_ref, mod_ref, g_ref, wg_ref, wu_ref, wo_ref, fg_ref, o_ref,
              h_scr, acc_scr, *, final_norm):
    f = pl.program_id(2)

    @pl.when(f == 0)
    def _():
        y = _rms_rows(x_ref[0], g_ref[...])
        h = y * (1.0 + mod_ref[0, 1:2, :]) + mod_ref[0, 0:1, :]
        h_scr[...] = h.astype(BF16)
        acc_scr[...] = jnp.zeros_like(acc_scr)

    h = h_scr[...]
    a = _dot(h, wg_ref[...])
    u = _dot(h, wu_ref[...])
    act = (_silu(a) * u).astype(BF16)
    acc_scr[...] += _dot(act, wo_ref[...])

    @pl.when(f == pl.num_programs(2) - 1)
    def _():
        out = x_ref[0] + (0.5 * mod_ref[0, 2:3, :]) * acc_scr[...]
        if final_norm:
            out = _rms_rows(out, fg_ref[...])
        o_ref[0] = out


def _ffn(x, mod8, g, wg, wu, wo, fg, *, final_norm, tm=1024, tf=256):
    b, t, d = x.shape
    ff = wg.shape[1]
    return pl.pallas_call(
        functools.partial(_ffn_body, final_norm=final_norm),
        grid=(b, t // tm, ff // tf),
        in_specs=[
            pl.BlockSpec((1, tm, d), lambda i, j, k: (i, j, 0)),
            pl.BlockSpec((1, 8, d), lambda i, j, k: (i, 0, 0)),
            pl.BlockSpec((1, d), lambda i, j, k: (0, 0)),
            pl.BlockSpec((d, tf), lambda i, j, k: (0, k)),
            pl.BlockSpec((d, tf), lambda i, j, k: (0, k)),
            pl.BlockSpec((tf, d), lambda i, j, k: (k, 0)),
            pl.BlockSpec((1, d), lambda i, j, k: (0, 0)),
        ],
        out_specs=pl.BlockSpec((1, tm, d), lambda i, j, k: (i, j, 0)),
        out_shape=jax.ShapeDtypeStruct((b, t, d), F32),
        scratch_shapes=[pltpu.VMEM((tm, d), BF16), pltpu.VMEM((tm, d), F32)],
        compiler_params=_cparams(("arbitrary", "arbitrary", "arbitrary")),
        name="ffn_final" if final_norm else "ffn",
    )(x, mod8, g, wg, wu, wo, fg)


_O_QKV, _O_Z, _O_Q = 0, 1536, 2048
_O_KC, _O_VC, _O_KS, _O_VS, _O_KW, _O_VW, _O_SM = 2560, 2688, 2816, 2944, 3072, 3200, 3328
_IN_COLS = 3456
_SMALL_ROWS = 32


def _rope_tile(x, cosf, sint, first_half):
    fwd = pltpu.roll(x, 8, axis=1)
    bwd = pltpu.roll(x, LANES - 8, axis=1)
    return x * cosf + jnp.where(first_half, -bwd, fwd) * sint


def _inproj_body(x_ref, mod_ref, g_ref, w_ref, wst_ref, cos_ref, sin_ref,
                 qkv_ref, z_ref, ba_ref, bat_ref, qpad_ref,
                 kc_ref, vc_ref, ks_ref, vs_ref, kw_ref, vw_ref):
    y = _rms_rows(x_ref[0], g_ref[...])
    h = (y * (1.0 + mod_ref[0, 1:2, :]) + mod_ref[0, 0:1, :]).astype(BF16)
    cosf = cos_ref[0]
    sint = sin_ref[0]
    lane = lax.broadcasted_iota(jnp.int32, cosf.shape, 1)
    first_half = (lane % NSA_HEAD_DIM) < (ROT_DIM // 2)
    low = lane < NSA_HEAD_DIM

    for j in range(3):
        qkv_ref[0, :, j * 512:(j + 1) * 512] = _dot(h, w_ref[:, _O_QKV + j * 512:_O_QKV + (j + 1) * 512])
    z_ref[0] = _dot(h, w_ref[:, _O_Z:_O_Z + 512])
    ba_ref[0] = _dot(h, w_ref[:, _O_SM:_O_SM + LANES])
    bat_ref[0] = _dot_nt(wst_ref[...], h)

    scale = NSA_HEAD_DIM ** -0.5
    for k in range(NSA_HEADS // 2):
        tile = _dot(h, w_ref[:, _O_Q + k * LANES:_O_Q + (k + 1) * LANES])
        tile = _rope_tile(tile, cosf, sint, first_half) * scale
        swapped = pltpu.roll(tile, NSA_HEAD_DIM, axis=1)
        grp = (2 * k) // NSA_REP
        if grp == 0:
            even = jnp.where(low, tile, 0.0)
            odd = jnp.where(low, swapped, 0.0)
        else:
            even = jnp.where(low, 0.0, swapped)
            odd = jnp.where(low, 0.0, tile)
        qpad_ref[0, 2 * k] = even.astype(BF16)
        qpad_ref[0, 2 * k + 1] = odd.astype(BF16)

    kc = _dot(h, w_ref[:, _O_KC:_O_KC + LANES])
    kc_ref[0] = _rope_tile(kc, cosf, sint, first_half)
    vc_ref[0] = _dot(h, w_ref[:, _O_VC:_O_VC + LANES])
    ks = _dot(h, w_ref[:, _O_KS:_O_KS + LANES])
    ks_ref[0] = _rope_tile(ks, cosf, sint, first_half).astype(BF16)
    vs_ref[0] = _dot(h, w_ref[:, _O_VS:_O_VS + LANES]).astype(BF16)
    kw = _dot(h, w_ref[:, _O_KW:_O_KW + LANES])
    kw_ref[0] = _rope_tile(kw, cosf, sint, first_half).astype(BF16)
    vw_ref[0] = _dot(h, w_ref[:, _O_VW:_O_VW + LANES]).astype(BF16)


def _inproj(x, mod8, g, w, wst, cosf, sint, tm=512):
    b, t, d = x.shape
    row = lambda n: pl.BlockSpec((1, tm, n), lambda i, j: (i, j, 0))
    sds = lambda n, dt: jax.ShapeDtypeStruct((b, t, n), dt)
    return pl.pallas_call(
        _inproj_body,
        grid=(b, t // tm),
        in_specs=[
            row(d),
            pl.BlockSpec((1, 8, d), lambda i, j: (i, 0, 0)),
            pl.BlockSpec((1, d), lambda i, j: (0, 0)),
            pl.BlockSpec((d, _IN_COLS), lambda i, j: (0, 0)),
            pl.BlockSpec((_SMALL_ROWS, d), lambda i, j: (0, 0)),
            row(LANES), row(LANES),
        ],
        out_specs=[
            row(3 * GDN_WIDTH), row(GDN_WIDTH), row(LANES),
            pl.BlockSpec((1, _SMALL_ROWS, tm), lambda i, j: (i, 0, j)),
            pl.BlockSpec((1, NSA_HEADS, tm, LANES), lambda i, j: (i, 0, j, 0)),
            row(LANES), row(LANES), row(LANES), row(LANES), row(LANES), row(LANES),
        ],
        out_shape=[
            sds(3 * GDN_WIDTH, F32), sds(GDN_WIDTH, F32), sds(LANES, F32),
            jax.ShapeDtypeStruct((b, _SMALL_ROWS, t), F32),
            jax.ShapeDtypeStruct((b, NSA_HEADS, t, LANES), BF16),
            sds(LANES, F32), sds(LANES, F32),
            sds(LANES, BF16), sds(LANES, BF16), sds(LANES, BF16), sds(LANES, BF16),
        ],
        compiler_params=_cparams(("arbitrary", "arbitrary")),
        name="mix_inproj",
    )(x, mod8, g, w, wst, cosf, sint)


def _softplus(x):
    return jnp.maximum(x, 0.0) + jnp.log1p(jnp.exp(-jnp.abs(x)))


def _dot_inv(a, b):
    ah = a.astype(BF16)
    al = (a - ah.astype(F32)).astype(BF16)
    bh = b.astype(BF16)
    bl = (b - bh.astype(F32)).astype(BF16)
    m = a.shape[0]
    top = _dot(jnp.concatenate([ah, al], axis=0), bh)
    return top[:m] + top[m:] + _dot(ah, bl)


def _unit_lower_inverse(low):
    n = low.shape[0]
    r = lax.broadcasted_iota(jnp.int32, (n, n), 0)
    c = lax.broadcasted_iota(jnp.int32, (n, n), 1)
    p = -low
    inv = jnp.where(r == c, 1.0, 0.0) + p
    for _ in range(int(math.log2(n)) - 1):
        p = _dot_inv(p, p)
        inv = inv + _dot_inv(inv, p)
    return inv


def _gdn_prep_body(qkv_ref, ba_ref, bat_ref, cw_ref, av_ref, avt_ref,
                   u_ref, wq_ref, ak_ref, egl_ref, buf):
    step = pl.program_id(1)
    c64 = GDN_CHUNK
    hd = GDN_HEAD_DIM
    nt = GDN_PREP_STEP

    @pl.when(step == 0)
    def _():
        buf[0:8, :] = jnp.zeros((8, 3 * GDN_WIDTH), F32)

    x = qkv_ref[0]
    buf[8:8 + nt, :] = x
    y = x * cw_ref[3:4, :]
    for j in range(GDN_CONV - 1):
        sh = GDN_CONV - 1 - j
        y = y + buf[8 - sh:8 - sh + nt, :] * cw_ref[j:j + 1, :]
    buf[0:8, :] = x[nt - 8:nt, :]
    y = _silu(y)

    ba = ba_ref[0]
    beta_all = _sigmoid(ba)
    g_col_all = av_ref[0:1, :] * _softplus(ba + av_ref[1:2, :])
    bat = bat_ref[0]
    g_row_all = avt_ref[:, 0:1] * _softplus(bat + avt_ref[:, 1:2])

    r64 = lax.broadcasted_iota(jnp.int32, (c64, c64), 0)
    k64 = lax.broadcasted_iota(jnp.int32, (c64, c64), 1)
    causal = r64 >= k64
    strict = r64 > k64
    tri = jnp.where(causal, 1.0, 0.0)
    tri_t = jnp.where(r64 <= k64, 1.0, 0.0)

    for ch in range(nt // c64):
        rows = slice(ch * c64, (ch + 1) * c64)
        gc_col_all = _dot_hi(tri, g_col_all[rows, :])
        gc_row_all = _dot_hi(g_row_all[:, rows], tri_t)
        egl_rows = []
        for h in range(GDN_HEADS):
            qh = y[rows, h * hd:(h + 1) * hd]
            kh = y[rows, GDN_WIDTH + h * hd:GDN_WIDTH + (h + 1) * hd]
            vh = y[rows, 2 * GDN_WIDTH + h * hd:2 * GDN_WIDTH + (h + 1) * hd]
            qh = qh * lax.rsqrt(jnp.sum(qh * qh, axis=-1, keepdims=True) + NORM_EPS) * (hd ** -0.5)
            kh = kh * lax.rsqrt(jnp.sum(kh * kh, axis=-1, keepdims=True) + NORM_EPS)
            bcol = beta_all[rows, h:h + 1]
            gcol = gc_col_all[:, GDN_HEADS + h:GDN_HEADS + h + 1]
            grow = gc_row_all[GDN_HEADS + h:GDN_HEADS + h + 1, :]
            decay = jnp.exp(jnp.where(causal, gcol - grow, -jnp.inf))
            kb = kh * bcol
            khb = kh.astype(BF16)
            low = jnp.where(strict, _dot_nt(kb.astype(BF16), khb) * decay, 0.0)
            inv = _unit_lower_inverse(low)
            eg = jnp.exp(gcol)
            rhs = jnp.concatenate([vh * bcol, kb * eg], axis=1)
            sol = _dot_inv(inv, rhs)
            attn = _dot_nt(qh.astype(BF16), khb) * decay
            glast = grow[:, c64 - 1:c64]
            kd_t = (kh * jnp.exp(glast - gcol)).T
            u_ref[0, h, rows, :] = sol[:, :hd]
            wq_ref[0, h, ch, 0:c64, :] = sol[:, hd:].astype(BF16)
            wq_ref[0, h, ch, c64:2 * c64, :] = (qh * eg).astype(BF16)
            ak_ref[0, h, ch, 0:c64, :] = attn.astype(BF16)
            ak_ref[0, h, ch, c64:c64 + hd, :] = kd_t.astype(BF16)
            egl_rows.append(jnp.broadcast_to(jnp.exp(glast), (1, LANES)))
        egl_rows.append(jnp.zeros((8 - GDN_HEADS, LANES), F32))
        egl_ref[0, ch] = jnp.concatenate(egl_rows, axis=0)


def _gdn_prep(qkv, ba, bat, conv_w, avec, avect):
    b, t, _ = qkv.shape
    nt = GDN_PREP_STEP
    nch = nt // GDN_CHUNK
    hd = GDN_HEAD_DIM
    row = lambda n: pl.BlockSpec((1, nt, n), lambda i, j: (i, j, 0))
    return pl.pallas_call(
        _gdn_prep_body,
        grid=(b, t // nt),
        in_specs=[
            row(3 * GDN_WIDTH), row(LANES),
            pl.BlockSpec((1, _SMALL_ROWS, nt), lambda i, j: (i, 0, j)),
            pl.BlockSpec((GDN_CONV, 3 * GDN_WIDTH), lambda i, j: (0, 0)),
            pl.BlockSpec((8, LANES), lambda i, j: (0, 0)),
            pl.BlockSpec((_SMALL_ROWS, LANES), lambda i, j: (0, 0)),
        ],
        out_specs=[
            pl.BlockSpec((1, GDN_HEADS, nt, hd), lambda i, j: (i, 0, j, 0)),
            pl.BlockSpec((1, GDN_HEADS, nch, 2 * GDN_CHUNK, hd), lambda i, j: (i, 0, j, 0, 0)),
            pl.BlockSpec((1, GDN_HEADS, nch, GDN_CHUNK + hd, GDN_CHUNK), lambda i, j: (i, 0, j, 0, 0)),
            pl.BlockSpec((1, nch, 8, LANES), lambda i, j: (i, j, 0, 0)),
        ],
        out_shape=[
            jax.ShapeDtypeStruct((b, GDN_HEADS, t, hd), F32),
            jax.ShapeDtypeStruct((b, GDN_HEADS, t // GDN_CHUNK, 2 * GDN_CHUNK, hd), BF16),
            jax.ShapeDtypeStruct((b, GDN_HEADS, t // GDN_CHUNK, GDN_CHUNK + hd, GDN_CHUNK), BF16),
            jax.ShapeDtypeStruct((b, t // GDN_CHUNK, 8, LANES), F32),
        ],
        scratch_shapes=[pltpu.VMEM((8 + nt, 3 * GDN_WIDTH), F32)],
        compiler_params=_cparams(("arbitrary", "arbitrary")),
        name="gdn_prep",
    )(qkv, ba, bat, conv_w, avec, avect)


def _gdn_scan_body(u_ref, wq_ref, ak_ref, egl_ref, z_ref, ng_ref, o_ref, s_scr):
    c64 = GDN_CHUNK
    hd = GDN_HEAD_DIM
    nb = u_ref.shape[0]

    @pl.when(pl.program_id(0) == 0)
    def _():
        s_scr[...] = jnp.zeros_like(s_scr)

    for ch in range(GDN_SCAN_STEP // c64):
        rows = slice(ch * c64, (ch + 1) * c64)
        for b in range(nb):
            for h in range(GDN_HEADS):
                s_old = s_scr[b, h]
                r1 = _dot(wq_ref[b, h, ch], s_old.astype(BF16))
                v_new = u_ref[b, h, rows, :] - r1[0:c64]
                r2 = _dot(ak_ref[b, h, ch], v_new.astype(BF16))
                o = r1[c64:2 * c64] + r2[0:c64]
                s_scr[b, h] = s_old * egl_ref[b, ch, h:h + 1, :] + r2[c64:c64 + hd]
                on = _rms_rows(o, ng_ref[...])
                zh = z_ref[b, rows, h * hd:(h + 1) * hd]
                o_ref[b, rows, h * hd:(h + 1) * hd] = (on * _silu(zh)).astype(o_ref.dtype)


def _gdn_scan(u, wq, ak, egl, z, norm_g):
    b, _, t, hd = u.shape
    nt = GDN_SCAN_STEP
    nch = nt // GDN_CHUNK
    return pl.pallas_call(
        _gdn_scan_body,
        grid=(t // nt,),
        in_specs=[
            pl.BlockSpec((b, GDN_HEADS, nt, hd), lambda j: (0, 0, j, 0)),
            pl.BlockSpec((b, GDN_HEADS, nch, 2 * GDN_CHUNK, hd), lambda j: (0, 0, j, 0, 0)),
            pl.BlockSpec((b, GDN_HEADS, nch, GDN_CHUNK + hd, GDN_CHUNK), lambda j: (0, 0, j, 0, 0)),
            pl.BlockSpec((b, nch, 8, LANES), lambda j: (0, j, 0, 0)),
            pl.BlockSpec((b, nt, GDN_WIDTH), lambda j: (0, j, 0)),
            pl.BlockSpec((1, hd), lambda j: (0, 0)),
        ],
        out_specs=pl.BlockSpec((b, nt, GDN_WIDTH), lambda j: (0, j, 0)),
        out_shape=jax.ShapeDtypeStruct((b, t, GDN_WIDTH), BF16),
        scratch_shapes=[pltpu.VMEM((b, GDN_HEADS, hd, hd), F32)],
        compiler_params=_cparams(("arbitrary",)),
        name="gdn_scan",
    )(u, wq, ak, egl, z, norm_g)


def _gdn(qkv, z, ba, bat, conv_w, avec, avect, norm_g):
    u, wq, ak, egl = _gdn_prep(qkv, ba, bat, conv_w, avec, avect)
    return _gdn_scan(u, wq, ak, egl, z, norm_g)


def _cmp_body(x_ref, pea_ref, peb_ref, wa_ref, wb_ref, w2_ref, o_ref):
    x = x_ref[0]
    a = _dot((x + pea_ref[...]).astype(BF16), wa_ref[...])
    bm = _dot((x + peb_ref[...]).astype(BF16), wb_ref[...])
    n = a.shape[0]
    h1 = a + pltpu.roll(bm, n - 1, axis=0)
    o_ref[0] = _dot(_silu(h1).astype(BF16), w2_ref[...]).astype(o_ref.dtype)


def _compress(x2, pea, peb, wa, wb, w2bd):
    b, n, wdt = x2.shape
    full = lambda s: pl.BlockSpec(s, lambda i: (0,) * len(s))
    return pl.pallas_call(
        _cmp_body,
        grid=(b,),
        in_specs=[pl.BlockSpec((1, n, wdt), lambda i: (i, 0, 0)),
                  full(pea.shape), full(peb.shape), full(wa.shape), full(wb.shape), full(w2bd.shape)],
        out_specs=pl.BlockSpec((1, n, LANES), lambda i: (i, 0, 0)),
        out_shape=jax.ShapeDtypeStruct((b, n, LANES), BF16),
        compiler_params=_cparams(("arbitrary",)),
        name="nsa_compress",
    )(x2, pea, peb, wa, wb, w2bd)


def _nsa_body(q_ref, kcmp_ref, vcmp_ref, ks_ref, vs_ref, kw_ref, vw_ref, gate_ref,
              ovl_ref, ng_ref, o_ref, *, seq):
    qb = pl.program_id(1)
    s0 = qb * Q_BLOCK
    n_cmp_rows = kcmp_ref.shape[1]
    n_blk = seq // SEL_BLOCK
    rq = NSA_REP * Q_BLOCK

    t_q = s0 + lax.broadcasted_iota(jnp.int32, (Q_BLOCK, 1), 0)
    lane = lax.broadcasted_iota(jnp.int32, (Q_BLOCK, LANES), 1)
    gates = _sigmoid(gate_ref[0])

    n_iota = lax.broadcasted_iota(jnp.int32, (Q_BLOCK, n_cmp_rows), 1)
    cmask = (n_iota * CMP_STRIDE + (CMP_LEN - 1)) <= t_q

    wstart = pl.multiple_of(jnp.maximum(s0 - WINDOW, 0), Q_BLOCK)
    kp = wstart + lax.broadcasted_iota(jnp.int32, (Q_BLOCK, WINDOW + Q_BLOCK), 1)
    wmask = (kp <= t_q) & (kp > t_q - WINDOW)

    cur = lax.shift_right_logical(t_q, 6)
    forced = (lane == 0) | (lane == cur) | (lane == cur - 1)
    valid = (lane * SEL_BLOCK <= t_q) & (lane < n_blk)
    jrow8 = lax.broadcasted_iota(jnp.int32, (8, Q_BLOCK), 0)

    n_chunks = qb // (SEL_CHUNK // Q_BLOCK) + 1

    for g in range(NSA_KV_HEADS):
        q = q_ref[0, g * NSA_REP:(g + 1) * NSA_REP].reshape(rq, LANES)
        lmask = (lane >= g * NSA_HEAD_DIM) & (lane < (g + 1) * NSA_HEAD_DIM)

        sc = _dot_nt(q, kcmp_ref[0]).reshape(NSA_REP, Q_BLOCK, n_cmp_rows)
        sc = jnp.where(cmask[None], sc, -jnp.inf)
        m = jnp.max(sc, axis=-1, keepdims=True)
        m = jnp.where(m == -jnp.inf, 0.0, m)
        p = jnp.exp(sc - m)
        p = p / jnp.maximum(jnp.sum(p, axis=-1, keepdims=True), 1e-30)
        o_cmp = _dot(p.reshape(rq, n_cmp_rows).astype(BF16), vcmp_ref[0])
        psum = p[0] + p[1] + p[2] + p[3]
        hi, mid, lo = _split3(psum)
        ovl = ovl_ref[...]
        imp = _dot(hi, ovl) + _dot(mid, ovl) + _dot(lo, ovl)

        key = jnp.where(forced, jnp.inf, jnp.where(valid, imp, -jnp.inf))
        key_t = key.T[:n_blk, :]
        groups = [key_t[8 * v:8 * v + 8, :] for v in range(n_blk // 8)]
        cnts = [jnp.zeros((8, Q_BLOCK), F32) for _ in groups]
        for i in range(n_blk):
            row = key_t[i:i + 1, :]
            for v, grp in enumerate(groups):
                if v < i // 8:
                    beats = jnp.where(row > grp, 1.0, 0.0)
                elif v > i // 8:
                    beats = jnp.where(row >= grp, 1.0, 0.0)
                else:
                    beats = jnp.where(jrow8 > i % 8, jnp.where(row >= grp, 1.0, 0.0),
                                      jnp.where(row > grp, 1.0, 0.0))
                cnts[v] = cnts[v] + beats
        sel_t = jnp.where(jnp.concatenate(cnts, axis=0) < float(N_SELECT), 1.0, 0.0)
        if n_blk < LANES:
            sel_t = jnp.concatenate([sel_t, jnp.zeros((LANES - n_blk, Q_BLOCK), F32)], axis=0)
        sel = sel_t.T.astype(BF16)

        def sel_step(c, carry):
            m_i, l_i, acc = carry
            k0 = pl.multiple_of(c * SEL_CHUNK, SEL_CHUNK)
            kch = ks_ref[0, pl.ds(k0, SEL_CHUNK), :]
            vch = vs_ref[0, pl.ds(k0, SEL_CHUNK), :]
            s = _dot_nt(q, kch).reshape(NSA_REP, Q_BLOCK, SEL_CHUNK)
            eb = lax.broadcasted_iota(jnp.int32, (LANES, SEL_CHUNK), 0)
            ek = lax.broadcasted_iota(jnp.int32, (LANES, SEL_CHUNK), 1)
            expand = jnp.where(eb == c * (SEL_CHUNK // SEL_BLOCK) + ek // SEL_BLOCK, 1.0, 0.0).astype(BF16)
            chosen = _dot(sel, expand) > 0.5
            kidx = k0 + lax.broadcasted_iota(jnp.int32, (Q_BLOCK, SEL_CHUNK), 1)
            mask = chosen & (kidx <= t_q)
            s = jnp.where(mask[None], s, NEG_BIG)
            m_new = jnp.maximum(m_i, jnp.max(s, axis=-1, keepdims=True))
            alpha = jnp.exp(m_i - m_new)
            pch = jnp.exp(s - m_new)
            l_new = alpha * l_i + jnp.sum(pch, axis=-1, keepdims=True)
            pv = _dot(pch.reshape(rq, SEL_CHUNK).astype(BF16), vch).reshape(NSA_REP, Q_BLOCK, LANES)
            return m_new, l_new, alpha * acc + pv

        init = (jnp.full((NSA_REP, Q_BLOCK, 1), NEG_BIG, F32),
                jnp.zeros((NSA_REP, Q_BLOCK, 1), F32),
                jnp.zeros((NSA_REP, Q_BLOCK, LANES), F32))
        _, l_s, acc_s = lax.fori_loop(0, n_chunks, sel_step, init)
        o_sel = acc_s / l_s

        kwin = kw_ref[0, pl.ds(wstart, WINDOW + Q_BLOCK), :]
        vwin = vw_ref[0, pl.ds(wstart, WINDOW + Q_BLOCK), :]
        sw = _dot_nt(q, kwin).reshape(NSA_REP, Q_BLOCK, WINDOW + Q_BLOCK)
        sw = jnp.where(wmask[None], sw, NEG_BIG)
        pw = jnp.exp(sw - jnp.max(sw, axis=-1, keepdims=True))
        pw = pw / jnp.sum(pw, axis=-1, keepdims=True)
        o_win = _dot(pw.reshape(rq, WINDOW + Q_BLOCK).astype(BF16), vwin)

        o_cmp = o_cmp.reshape(NSA_REP, Q_BLOCK, LANES)
        o_win = o_win.reshape(NSA_REP, Q_BLOCK, LANES)

        ys = []
        for r in range(NSA_REP):
            col = 2 * GDN_HEADS + (g * NSA_REP + r) * 3
            o = (gates[:, col:col + 1] * o_cmp[r] + gates[:, col + 1:col + 2] * o_sel[r]
                 + gates[:, col + 2:col + 3] * o_win[r])
            o = jnp.where(lmask, o, 0.0)
            ms = jnp.sum(o * o, axis=-1, keepdims=True) * (1.0 / NSA_HEAD_DIM)
            ys.append(o * lax.rsqrt(ms + NORM_EPS) * ng_ref[...])
        for pair in range(NSA_REP // 2):
            a, bb = ys[2 * pair], ys[2 * pair + 1]
            if g == 0:
                tile = a + pltpu.roll(bb, NSA_HEAD_DIM, axis=1)
            else:
                tile = pltpu.roll(a, NSA_HEAD_DIM, axis=1) + bb
            c0 = g * NSA_REP * NSA_HEAD_DIM + pair * LANES
            o_ref[0, :, c0:c0 + LANES] = tile.astype(o_ref.dtype)


def _nsa(qpad, kcmp, vcmp, ks, vs, kw, vw, ba, ovl, ng_lane):
    b, _, t, _ = qpad.shape
    n = kcmp.shape[1]
    whole = lambda rows: pl.BlockSpec((1, rows, LANES), lambda i, j: (i, 0, 0))
    return pl.pallas_call(
        functools.partial(_nsa_body, seq=t),
        grid=(b, t // Q_BLOCK),
        in_specs=[
            pl.BlockSpec((1, NSA_HEADS, Q_BLOCK, LANES), lambda i, j: (i, 0, j, 0)),
            whole(n), whole(n), whole(t), whole(t), whole(t), whole(t),
            pl.BlockSpec((1, Q_BLOCK, LANES), lambda i, j: (i, j, 0)),
            pl.BlockSpec((n, LANES), lambda i, j: (0, 0)),
            pl.BlockSpec((1, LANES), lambda i, j: (0, 0)),
        ],
        out_specs=pl.BlockSpec((1, Q_BLOCK, NSA_WIDTH), lambda i, j: (i, j, 0)),
        out_shape=jax.ShapeDtypeStruct((b, t, NSA_WIDTH), BF16),
        compiler_params=_cparams(("arbitrary", "arbitrary")),
        name="nsa_attn",
    )(qpad, kcmp, vcmp, ks, vs, kw, vw, ba, ovl, ng_lane)


def _outproj_body(x_ref, mod_ref, yg_ref, yn_ref, w_ref, o_ref):
    y = _dot(yg_ref[0], w_ref[0:GDN_WIDTH, :]) + _dot(yn_ref[0], w_ref[GDN_WIDTH:, :])
    o_ref[0] = x_ref[0] + mod_ref[0, 2:3, :] * y


def _outproj(x, mod8, yg, yn, w, tm=1024):
    b, t, d = x.shape
    row = lambda n: pl.BlockSpec((1, tm, n), lambda i, j: (i, j, 0))
    return pl.pallas_call(
        _outproj_body,
        grid=(b, t // tm),
        in_specs=[row(d), pl.BlockSpec((1, 8, d), lambda i, j: (i, 0, 0)),
                  row(GDN_WIDTH), row(NSA_WIDTH),
                  pl.BlockSpec(w.shape, lambda i, j: (0, 0))],
        out_specs=row(d),
        out_shape=jax.ShapeDtypeStruct((b, t, d), F32),
        compiler_params=_cparams(("arbitrary", "arbitrary")),
        name="mix_outproj",
    )(x, mod8, yg, yn, w)


_IN_SIZES = (3 * GDN_WIDTH, GDN_WIDTH, GDN_HEADS, GDN_HEADS, NSA_WIDTH) + (NSA_KV_WIDTH,) * 6 + (3 * NSA_HEADS,)


def _prep_inproj_weight(w):
    offs = np.concatenate([[0], np.cumsum(_IN_SIZES)])
    seg = lambda i: w[:, int(offs[i]):int(offs[i + 1])]
    small = jnp.concatenate([seg(2), seg(3), seg(11)], axis=1)
    big = jnp.concatenate([seg(0), seg(1)] + [seg(i) for i in range(4, 11)]
                          + [small, jnp.zeros((w.shape[0], LANES - _SMALL_ROWS), w.dtype)], axis=1)
    return big.astype(BF16), small.T.astype(BF16)


def _prep_compress(pe, w1, w2):
    hid = CMP_HIDDEN
    g, dh = NSA_KV_HEADS, NSA_HEAD_DIM
    half = CMP_LEN // 2
    w1r = w1.reshape(CMP_LEN, dh, hid)
    eye = jnp.eye(g, dtype=w1.dtype)
    expand = lambda part: jnp.einsum("ldh,gk->lgdkh", part, eye).reshape(half * g * dh, g * hid)
    wa, wb = expand(w1r[:half]), expand(w1r[half:])
    pe_row = lambda part: jnp.broadcast_to(part[:, None, :], (half, g, dh)).reshape(1, half * g * dh)
    pea, peb = pe_row(pe[:half]), pe_row(pe[half:])
    w2bd = jnp.einsum("hd,gk->ghkd", w2, eye).reshape(g * hid, g * dh)
    return pea, peb, wa.astype(BF16), wb.astype(BF16), w2bd.astype(BF16)


def _overlap_matrix(n_rows, seq):
    n_cmp = (seq - CMP_LEN) // CMP_STRIDE + 1
    n_blk = seq // SEL_BLOCK
    n = np.arange(n_rows)[:, None]
    j = np.arange(LANES)[None, :]
    start, end = n * CMP_STRIDE, n * CMP_STRIDE + CMP_LEN - 1
    ovl = (start < j * SEL_BLOCK + SEL_BLOCK) & (end >= j * SEL_BLOCK) & (n < n_cmp) & (j < n_blk)
    return jnp.asarray(ovl.astype(np.float32), dtype=BF16)


def kernel(x, c, positions, ada_w, ada_b, norm_g, ffn_w_in, ffn_w_out, mix_w_in, gdn_conv_w, gdn_a_log, gdn_dt_bias, gdn_norm_g, cmp_pe_k, cmp_w1_k, cmp_w2_k, cmp_pe_v, cmp_w1_v, cmp_w2_v, nsa_norm_g, mix_w_out, final_norm_g):
    b, t, d = x.shape
    depth = ada_w.shape[0]
    d_ff = ffn_w_out.shape[2]
    assert t % (SEL_CHUNK * 4) == 0 and b <= 8

    c_pad = jnp.zeros((8, d), F32).at[:b].set(c.astype(F32))
    mod = _ada_mod(c_pad, ada_w.astype(F32), ada_b.astype(F32))
    mod = mod[:, :b].reshape(depth, b, 3, 3, d)
    mod8 = jnp.concatenate([mod, jnp.zeros((depth, b, 3, 5, d), F32)], axis=3)

    half = ROT_DIM // 2
    inv_freq = jnp.power(ROPE_THETA, -jnp.arange(half, dtype=F32) * (2.0 / ROT_DIM))
    dim = np.arange(LANES) % NSA_HEAD_DIM
    invf_lane = jnp.where(jnp.asarray(dim < ROT_DIM), inv_freq[dim % half], 0.0).reshape(1, LANES)
    cosf, sint = _rope_tables(positions.astype(F32).reshape(b, t, 1), invf_lane, tm=1024)

    n_rows = t // CMP_STRIDE
    ovl = _overlap_matrix(n_rows, t)
    fg = final_norm_g.reshape(1, d).astype(F32)

    for l in range(depth):
        wg = [ffn_w_in[l, i, :, :d_ff].astype(BF16) for i in range(2)]
        wu = [ffn_w_in[l, i, :, d_ff:].astype(BF16) for i in range(2)]
        wo = [ffn_w_out[l, i].astype(BF16) for i in range(2)]
        ng = lambda i: norm_g[l, i].reshape(1, d).astype(F32)

        x = _ffn(x, mod8[l, :, 0], ng(0), wg[0], wu[0], wo[0], fg, final_norm=False)

        w_big, w_small_t = _prep_inproj_weight(mix_w_in[l])
        (qkv, z, ba, bat, qpad, kc, vc, ks, vs, kw, vw) = _inproj(
            x, mod8[l, :, 1], ng(1), w_big, w_small_t, cosf, sint)

        neg_a = -jnp.exp(gdn_a_log[l].astype(F32))
        dtb = gdn_dt_bias[l].astype(F32)
        avec = jnp.zeros((8, LANES), F32).at[0, GDN_HEADS:2 * GDN_HEADS].set(neg_a)
        avec = avec.at[1, GDN_HEADS:2 * GDN_HEADS].set(dtb)
        avect = jnp.zeros((_SMALL_ROWS, LANES), F32).at[GDN_HEADS:2 * GDN_HEADS, 0].set(neg_a)
        avect = avect.at[GDN_HEADS:2 * GDN_HEADS, 1].set(dtb)
        y_gdn = _gdn(qkv, z, ba, bat, gdn_conv_w[l].astype(F32), avec, avect,
                     gdn_norm_g[l].reshape(1, GDN_HEAD_DIM).astype(F32))

        kcmp = _compress(kc.reshape(b, n_rows, CMP_STRIDE * LANES),
                         *_prep_compress(cmp_pe_k[l], cmp_w1_k[l], cmp_w2_k[l]))
        vcmp = _compress(vc.reshape(b, n_rows, CMP_STRIDE * LANES),
                         *_prep_compress(cmp_pe_v[l], cmp_w1_v[l], cmp_w2_v[l]))
        ng_lane = jnp.tile(nsa_norm_g[l].astype(F32), NSA_KV_HEADS).reshape(1, LANES)
        y_nsa = _nsa(qpad, kcmp, vcmp, ks, vs, kw, vw, ba, ovl, ng_lane)

        x = _outproj(x, mod8[l, :, 1], y_gdn, y_nsa, mix_w_out[l].astype(BF16))

        x = _ffn(x, mod8[l, :, 2], ng(2), wg[1], wu[1], wo[1], fg, final_norm=(l == depth - 1))
    return x
```

```python
import functools
import math

import jax
import jax.numpy as jnp
import numpy as np
from jax import lax
from jax.experimental import pallas as pl
from jax.experimental.pallas import tpu as pltpu

F32 = jnp.float32
BF16 = jnp.bfloat16

NORM_EPS = 1e-6
LANES = 128
GDN_HEADS = 4
GDN_HEAD_DIM = 128
GDN_WIDTH = GDN_HEADS * GDN_HEAD_DIM
GDN_CONV = 4
GDN_CHUNK = 64
GDN_PREP_STEP = 4 * GDN_CHUNK
GDN_SCAN_STEP = 2 * GDN_CHUNK
NSA_HEADS = 8
NSA_KV_HEADS = 2
NSA_REP = NSA_HEADS // NSA_KV_HEADS
NSA_HEAD_DIM = 64
NSA_WIDTH = NSA_HEADS * NSA_HEAD_DIM
NSA_KV_WIDTH = NSA_KV_HEADS * NSA_HEAD_DIM
CMP_LEN = 32
CMP_STRIDE = 16
CMP_HIDDEN = 2 * NSA_HEAD_DIM
SEL_BLOCK = 64
N_SELECT = 16
WINDOW = 512
Q_BLOCK = 128
SEL_CHUNK = 512
ROPE_THETA = 500000.0
ROT_DIM = NSA_HEAD_DIM // 4
N_ADA = 9
NEG_BIG = -1e30
VMEM_LIMIT = 56 * 1024 * 1024

HIGHEST = lax.Precision.HIGHEST


def _cparams(sem):
    return pltpu.CompilerParams(dimension_semantics=sem, vmem_limit_bytes=VMEM_LIMIT)


def _dot(a, b):
    return jnp.dot(a, b, preferred_element_type=F32)


def _dot_nt(a, b):
    return lax.dot_general(a, b, (((1,), (1,)), ((), ())), preferred_element_type=F32)


def _dot_hi(a, b):
    return jnp.dot(a, b, preferred_element_type=F32, precision=HIGHEST)


def _split3(x):
    hi = x.astype(BF16)
    r1 = x - hi.astype(F32)
    mid = r1.astype(BF16)
    lo = (r1 - mid.astype(F32)).astype(BF16)
    return hi, mid, lo


def _sigmoid(x):
    return 1.0 / (1.0 + jnp.exp(-x))


def _silu(x):
    return x * _sigmoid(x)


def _rms_rows(x, g):
    ms = jnp.mean(x * x, axis=-1, keepdims=True)
    return x * lax.rsqrt(ms + NORM_EPS) * g


def _ada_body(c_ref, w_ref, b_ref, o_ref):
    cond = _silu(c_ref[...])
    o_ref[0] = _dot_hi(cond, w_ref[0]) + b_ref[0]


def _ada_mod(c_pad, ada_w, ada_b):
    depth, d, n = ada_w.shape
    tn = n // N_ADA
    rows = c_pad.shape[0]
    return pl.pallas_call(
        _ada_body,
        grid=(depth, n // tn),
        in_specs=[
            pl.BlockSpec((rows, d), lambda l, j: (0, 0)),
            pl.BlockSpec((1, d, tn), lambda l, j: (l, 0, j)),
            pl.BlockSpec((1, 1, tn), lambda l, j: (l, 0, j)),
        ],
        out_specs=pl.BlockSpec((1, rows, tn), lambda l, j: (l, 0, j)),
        out_shape=jax.ShapeDtypeStruct((depth, rows, n), F32),
        compiler_params=_cparams(("arbitrary", "arbitrary")),
        name="ada_mod",
    )(c_pad, ada_w, ada_b.reshape(depth, 1, n))


def _rope_body(pos_ref, invf_ref, cos_ref, sin_ref):
    ang = pos_ref[0] * invf_ref[...]
    cos_ref[0] = jnp.cos(ang)
    sin_ref[0] = jnp.sin(ang)


def _rope_tables(pos_f, invf_lane, tm):
    b, t, _ = pos_f.shape
    spec = pl.BlockSpec((1, tm, LANES), lambda i, j: (i, j, 0))
    return pl.pallas_call(
        _rope_body,
        grid=(b, t // tm),
        in_specs=[pl.BlockSpec((1, tm, 1), lambda i, j: (i, j, 0)),
                  pl.BlockSpec((1, LANES), lambda i, j: (0, 0))],
        out_specs=[spec, spec],
        out_shape=[jax.ShapeDtypeStruct((b, t, LANES), F32)] * 2,
        compiler_params=_cparams(("arbitrary", "arbitrary")),
        name="rope_tables",
    )(pos_f, invf_lane)


def _ffn_body(x_ref, mod_ref, g_ref, wg_ref, wu_ref, wo_ref, fg_ref, o_ref,
              act_scr, *, final_norm, tf):
    x = x_ref[0]
    y = _rms_rows(x, g_ref[...])
    h = (y * (1.0 + mod_ref[0, 1:2, :]) + mod_ref[0, 0:1, :]).astype(BF16)
    ff = wg_ref.shape[1]
    for j in range(ff // tf):
        cols = slice(j * tf, (j + 1) * tf)
        a = _dot(h, wg_ref[:, cols])
        u = _dot(h, wu_ref[:, cols])
        act_scr[:, cols] = (_silu(a) * u).astype(BF16)
    out = x + (0.5 * mod_ref[0, 2:3, :]) * _dot(act_scr[...], wo_ref[...])
    if final_norm:
        out = _rms_rows(out, fg_ref[...])
    o_ref[0] = out


def _ffn(x, mod8, g, wg, wu, wo, fg, *, final_norm, tm=1024, tf=256):
    b, t, d = x.shape
    ff = wg.shape[1]
    resident = lambda shape: pl.BlockSpec(shape, lambda i, j: (0, 0), pipeline_mode=pl.Buffered(1))
    return pl.pallas_call(
        functools.partial(_ffn_body, final_norm=final_norm, tf=tf),
        grid=(b, t // tm),
        in_specs=[
            pl.BlockSpec((1, tm, d), lambda i, j: (i, j, 0)),
            pl.BlockSpec((1, 8, d), lambda i, j: (i, 0, 0)),
            pl.BlockSpec((1, d), lambda i, j: (0, 0)),
            resident((d, ff)), resident((d, ff)), resident((ff, d)),
            pl.BlockSpec((1, d), lambda i, j: (0, 0)),
        ],
        out_specs=pl.BlockSpec((1, tm, d), lambda i, j: (i, j, 0)),
        out_shape=jax.ShapeDtypeStruct((b, t, d), F32),
        scratch_shapes=[pltpu.VMEM((tm, ff), BF16)],
        compiler_params=_cparams(("arbitrary", "arbitrary")),
        name="ffn_final" if final_norm else "ffn",
    )(x, mod8, g, wg, wu, wo, fg)


_O_QKV, _O_Z, _O_Q = 0, 1536, 2048
_O_KC, _O_VC, _O_KS, _O_VS, _O_KW, _O_VW, _O_SM = 2560, 2688, 2816, 2944, 3072, 3200, 3328
_IN_COLS = 3456
_SMALL_ROWS = 32


def _rope_tile(x, cosf, sint, first_half):
    fwd = pltpu.roll(x, 8, axis=1)
    bwd = pltpu.roll(x, LANES - 8, axis=1)
    return x * cosf + jnp.where(first_half, -bwd, fwd) * sint


def _inproj_body(x_ref, mod_ref, g_ref, w_ref, wst_ref, cos_ref, sin_ref,
                 qkv_ref, z_ref, ba_ref, bat_ref, qpad_ref,
                 kc_ref, vc_ref, ks_ref, vs_ref, kw_ref, vw_ref):
    y = _rms_rows(x_ref[0], g_ref[...])
    h = (y * (1.0 + mod_ref[0, 1:2, :]) + mod_ref[0, 0:1, :]).astype(BF16)
    cosf = cos_ref[0]
    sint = sin_ref[0]
    lane = lax.broadcasted_iota(jnp.int32, cosf.shape, 1)
    first_half = (lane % NSA_HEAD_DIM) < (ROT_DIM // 2)
    low = lane < NSA_HEAD_DIM

    for j in range(3):
        qkv_ref[0, :, j * 512:(j + 1) * 512] = _dot(h, w_ref[:, _O_QKV + j * 512:_O_QKV + (j + 1) * 512])
    z_ref[0] = _dot(h, w_ref[:, _O_Z:_O_Z + 512])
    ba_ref[0] = _dot(h, w_ref[:, _O_SM:_O_SM + LANES])
    bat_ref[0] = _dot_nt(wst_ref[...], h)

    scale = NSA_HEAD_DIM ** -0.5
    for k in range(NSA_HEADS // 2):
        tile = _dot(h, w_ref[:, _O_Q + k * LANES:_O_Q + (k + 1) * LANES])
        tile = _rope_tile(tile, cosf, sint, first_half) * scale
        swapped = pltpu.roll(tile, NSA_HEAD_DIM, axis=1)
        grp = (2 * k) // NSA_REP
        if grp == 0:
            even = jnp.where(low, tile, 0.0)
            odd = jnp.where(low, swapped, 0.0)
        else:
            even = jnp.where(low, 0.0, swapped)
            odd = jnp.where(low, 0.0, tile)
        qpad_ref[0, 2 * k] = even.astype(BF16)
        qpad_ref[0, 2 * k + 1] = odd.astype(BF16)

    kc = _dot(h, w_ref[:, _O_KC:_O_KC + LANES])
    kc_ref[0] = _rope_tile(kc, cosf, sint, first_half)
    vc_ref[0] = _dot(h, w_ref[:, _O_VC:_O_VC + LANES])
    ks = _dot(h, w_ref[:, _O_KS:_O_KS + LANES])
    ks_ref[0] = _rope_tile(ks, cosf, sint, first_half).astype(BF16)
    vs_ref[0] = _dot(h, w_ref[:, _O_VS:_O_VS + LANES]).astype(BF16)
    kw = _dot(h, w_ref[:, _O_KW:_O_KW + LANES])
    kw_ref[0] = _rope_tile(kw, cosf, sint, first_half).astype(BF16)
    vw_ref[0] = _dot(h, w_ref[:, _O_VW:_O_VW + LANES]).astype(BF16)


def _inproj(x, mod8, g, w, wst, cosf, sint, tm=512):
    b, t, d = x.shape
    row = lambda n: pl.BlockSpec((1, tm, n), lambda i, j: (i, j, 0))
    sds = lambda n, dt: jax.ShapeDtypeStruct((b, t, n), dt)
    return pl.pallas_call(
        _inproj_body,
        grid=(b, t // tm),
        in_specs=[
            row(d),
            pl.BlockSpec((1, 8, d), lambda i, j: (i, 0, 0)),
            pl.BlockSpec((1, d), lambda i, j: (0, 0)),
            pl.BlockSpec((d, _IN_COLS), lambda i, j: (0, 0)),
            pl.BlockSpec((_SMALL_ROWS, d), lambda i, j: (0, 0)),
            row(LANES), row(LANES),
        ],
        out_specs=[
            row(3 * GDN_WIDTH), row(GDN_WIDTH), row(LANES),
            pl.BlockSpec((1, _SMALL_ROWS, tm), lambda i, j: (i, 0, j)),
            pl.BlockSpec((1, NSA_HEADS, tm, LANES), lambda i, j: (i, 0, j, 0)),
            row(LANES), row(LANES), row(LANES), row(LANES), row(LANES), row(LANES),
        ],
        out_shape=[
            sds(3 * GDN_WIDTH, F32), sds(GDN_WIDTH, F32), sds(LANES, F32),
            jax.ShapeDtypeStruct((b, _SMALL_ROWS, t), F32),
            jax.ShapeDtypeStruct((b, NSA_HEADS, t, LANES), BF16),
            sds(LANES, F32), sds(LANES, F32),
            sds(LANES, BF16), sds(LANES, BF16), sds(LANES, BF16), sds(LANES, BF16),
        ],
        compiler_params=_cparams(("arbitrary", "arbitrary")),
        name="mix_inproj",
    )(x, mod8, g, w, wst, cosf, sint)


def _softplus(x):
    return jnp.maximum(x, 0.0) + jnp.log1p(jnp.exp(-jnp.abs(x)))


def _dot_inv(a, b):
    ah = a.astype(BF16)
    al = (a - ah.astype(F32)).astype(BF16)
    bh = b.astype(BF16)
    bl = (b - bh.astype(F32)).astype(BF16)
    m = a.shape[0]
    top = _dot(jnp.concatenate([ah, al], axis=0), bh)
    return top[:m] + top[m:] + _dot(ah, bl)


def _unit_lower_inverses(lows):
    n = lows[0].shape[0]
    r = lax.broadcasted_iota(jnp.int32, (n, n), 0)
    c = lax.broadcasted_iota(jnp.int32, (n, n), 1)
    eye = jnp.where(r == c, 1.0, 0.0)
    ps = [-low for low in lows]
    invs = [eye + p for p in ps]
    for _ in range(int(math.log2(n)) - 1):
        ps = [_dot_inv(p, p) for p in ps]
        invs = [inv + _dot_inv(inv, p) for inv, p in zip(invs, ps)]
    return invs


def _gdn_prep_body(qkv_ref, ba_ref, bat_ref, cw_ref, av_ref, avt_ref,
                   u_ref, wq_ref, ak_ref, egl_ref, buf):
    step = pl.program_id(1)
    c64 = GDN_CHUNK
    hd = GDN_HEAD_DIM
    nt = GDN_PREP_STEP

    @pl.when(step == 0)
    def _():
        buf[0:8, :] = jnp.zeros((8, 3 * GDN_WIDTH), F32)

    x = qkv_ref[0]
    buf[8:8 + nt, :] = x
    y = x * cw_ref[3:4, :]
    for j in range(GDN_CONV - 1):
        sh = GDN_CONV - 1 - j
        y = y + buf[8 - sh:8 - sh + nt, :] * cw_ref[j:j + 1, :]
    buf[0:8, :] = x[nt - 8:nt, :]
    y = _silu(y)

    ba = ba_ref[0]
    beta_all = _sigmoid(ba)
    g_col_all = av_ref[0:1, :] * _softplus(ba + av_ref[1:2, :])
    bat = bat_ref[0]
    g_row_all = avt_ref[:, 0:1] * _softplus(bat + avt_ref[:, 1:2])

    r64 = lax.broadcasted_iota(jnp.int32, (c64, c64), 0)
    k64 = lax.broadcasted_iota(jnp.int32, (c64, c64), 1)
    causal = r64 >= k64
    strict = r64 > k64
    tri = jnp.where(causal, 1.0, 0.0)
    tri_t = jnp.where(r64 <= k64, 1.0, 0.0)

    nch = nt // c64
    units = [(ch, h) for ch in range(nch) for h in range(GDN_HEADS)]
    rows_of = lambda ch: slice(ch * c64, (ch + 1) * c64)
    gc_cols = [_dot_hi(tri, g_col_all[rows_of(ch), :]) for ch in range(nch)]
    gc_rows = [_dot_hi(g_row_all[:, rows_of(ch)], tri_t) for ch in range(nch)]

    def l2n(v):
        return v * lax.rsqrt(jnp.sum(v * v, axis=-1, keepdims=True) + NORM_EPS)

    khs, kbs, decays, lows = [], [], [], []
    for ch, h in units:
        kh = l2n(y[rows_of(ch), GDN_WIDTH + h * hd:GDN_WIDTH + (h + 1) * hd])
        gcol = gc_cols[ch][:, GDN_HEADS + h:GDN_HEADS + h + 1]
        grow = gc_rows[ch][GDN_HEADS + h:GDN_HEADS + h + 1, :]
        decay = jnp.exp(jnp.where(causal, gcol - grow, -jnp.inf))
        kb = kh * beta_all[rows_of(ch), h:h + 1]
        low = jnp.where(strict, _dot_nt(kb.astype(BF16), kh.astype(BF16)) * decay, 0.0)
        khs.append(kh), kbs.append(kb), decays.append(decay), lows.append(low)

    invs = _unit_lower_inverses(lows)

    egl_rows = [[] for _ in range(nch)]
    for i, (ch, h) in enumerate(units):
        rows = rows_of(ch)
        kh, kb, decay = khs[i], kbs[i], decays[i]
        qh = l2n(y[rows, h * hd:(h + 1) * hd]) * (hd ** -0.5)
        vh = y[rows, 2 * GDN_WIDTH + h * hd:2 * GDN_WIDTH + (h + 1) * hd]
        gcol = gc_cols[ch][:, GDN_HEADS + h:GDN_HEADS + h + 1]
        glast = gc_rows[ch][GDN_HEADS + h:GDN_HEADS + h + 1, c64 - 1:c64]
        eg = jnp.exp(gcol)
        rhs = jnp.concatenate([vh * beta_all[rows, h:h + 1], kb * eg], axis=1)
        sol = _dot_inv(invs[i], rhs)
        attn = _dot_nt(qh.astype(BF16), kh.astype(BF16)) * decay
        kd_t = (kh * jnp.exp(glast - gcol)).T
        u_ref[0, h, rows, :] = sol[:, :hd]
        wq_ref[0, h, ch, 0:c64, :] = sol[:, hd:].astype(BF16)
        wq_ref[0, h, ch, c64:2 * c64, :] = (qh * eg).astype(BF16)
        ak_ref[0, h, ch, 0:c64, :] = attn.astype(BF16)
        ak_ref[0, h, ch, c64:c64 + hd, :] = kd_t.astype(BF16)
        egl_rows[ch].append(jnp.broadcast_to(jnp.exp(glast), (1, LANES)))
    for ch in range(nch):
        egl_ref[0, ch] = jnp.concatenate(egl_rows[ch] + [jnp.zeros((8 - GDN_HEADS, LANES), F32)], axis=0)


def _gdn_prep(qkv, ba, bat, conv_w, avec, avect):
    b, t, _ = qkv.shape
    nt = GDN_PREP_STEP
    nch = nt // GDN_CHUNK
    hd = GDN_HEAD_DIM
    row = lambda n: pl.BlockSpec((1, nt, n), lambda i, j: (i, j, 0))
    return pl.pallas_call(
        _gdn_prep_body,
        grid=(b, t // nt),
        in_specs=[
            row(3 * GDN_WIDTH), row(LANES),
            pl.BlockSpec((1, _SMALL_ROWS, nt), lambda i, j: (i, 0, j)),
            pl.BlockSpec((GDN_CONV, 3 * GDN_WIDTH), lambda i, j: (0, 0)),
            pl.BlockSpec((8, LANES), lambda i, j: (0, 0)),
            pl.BlockSpec((_SMALL_ROWS, LANES), lambda i, j: (0, 0)),
        ],
        out_specs=[
            pl.BlockSpec((1, GDN_HEADS, nt, hd), lambda i, j: (i, 0, j, 0)),
            pl.BlockSpec((1, GDN_HEADS, nch, 2 * GDN_CHUNK, hd), lambda i, j: (i, 0, j, 0, 0)),
            pl.BlockSpec((1, GDN_HEADS, nch, GDN_CHUNK + hd, GDN_CHUNK), lambda i, j: (i, 0, j, 0, 0)),
            pl.BlockSpec((1, nch, 8, LANES), lambda i, j: (i, j, 0, 0)),
        ],
        out_shape=[
            jax.ShapeDtypeStruct((b, GDN_HEADS, t, hd), F32),
            jax.ShapeDtypeStruct((b, GDN_HEADS, t // GDN_CHUNK, 2 * GDN_CHUNK, hd), BF16),
            jax.ShapeDtypeStruct((b, GDN_HEADS, t // GDN_CHUNK, GDN_CHUNK + hd, GDN_CHUNK), BF16),
            jax.ShapeDtypeStruct((b, t // GDN_CHUNK, 8, LANES), F32),
        ],
        scratch_shapes=[pltpu.VMEM((8 + nt, 3 * GDN_WIDTH), F32)],
        compiler_params=_cparams(("arbitrary", "arbitrary")),
        name="gdn_prep",
    )(qkv, ba, bat, conv_w, avec, avect)


def _gdn_scan_body(u_ref, wq_ref, ak_ref, egl_ref, z_ref, ng_ref, o_ref, s_scr):
    c64 = GDN_CHUNK
    hd = GDN_HEAD_DIM
    nb = u_ref.shape[0]

    @pl.when(pl.program_id(0) == 0)
    def _():
        s_scr[...] = jnp.zeros_like(s_scr)

    chains = [(b, h) for b in range(nb) for h in range(GDN_HEADS)]
    states = [s_scr[b, h] for b, h in chains]
    for ch in range(GDN_SCAN_STEP // c64):
        rows = slice(ch * c64, (ch + 1) * c64)
        r1s = [_dot(wq_ref[b, h, ch], s.astype(BF16)) for (b, h), s in zip(chains, states)]
        vns = [(u_ref[b, h, rows, :] - r1[0:c64]).astype(BF16) for (b, h), r1 in zip(chains, r1s)]
        r2s = [_dot(ak_ref[b, h, ch], vn) for (b, h), vn in zip(chains, vns)]
        states = [s * egl_ref[b, ch, h:h + 1, :] + r2[c64:c64 + hd]
                  for (b, h), s, r2 in zip(chains, states, r2s)]
        for (b, h), r1, r2 in zip(chains, r1s, r2s):
            on = _rms_rows(r1[c64:2 * c64] + r2[0:c64], ng_ref[...])
            zh = z_ref[b, rows, h * hd:(h + 1) * hd]
            o_ref[b, rows, h * hd:(h + 1) * hd] = (on * _silu(zh)).astype(o_ref.dtype)
    for (b, h), s in zip(chains, states):
        s_scr[b, h] = s


def _gdn_scan(u, wq, ak, egl, z, norm_g):
    b, _, t, hd = u.shape
    nt = GDN_SCAN_STEP
    nch = nt // GDN_CHUNK
    return pl.pallas_call(
        _gdn_scan_body,
        grid=(t // nt,),
        in_specs=[
            pl.BlockSpec((b, GDN_HEADS, nt, hd), lambda j: (0, 0, j, 0)),
            pl.BlockSpec((b, GDN_HEADS, nch, 2 * GDN_CHUNK, hd), lambda j: (0, 0, j, 0, 0)),
            pl.BlockSpec((b, GDN_HEADS, nch, GDN_CHUNK + hd, GDN_CHUNK), lambda j: (0, 0, j, 0, 0)),
            pl.BlockSpec((b, nch, 8, LANES), lambda j: (0, j, 0, 0)),
            pl.BlockSpec((b, nt, GDN_WIDTH), lambda j: (0, j, 0)),
            pl.BlockSpec((1, hd), lambda j: (0, 0)),
        ],
        out_specs=pl.BlockSpec((b, nt, GDN_WIDTH), lambda j: (0, j, 0)),
        out_shape=jax.ShapeDtypeStruct((b, t, GDN_WIDTH), BF16),
        scratch_shapes=[pltpu.VMEM((b, GDN_HEADS, hd, hd), F32)],
        compiler_params=_cparams(("arbitrary",)),
        name="gdn_scan",
    )(u, wq, ak, egl, z, norm_g)


def _gdn(qkv, z, ba, bat, conv_w, avec, avect, norm_g):
    u, wq, ak, egl = _gdn_prep(qkv, ba, bat, conv_w, avec, avect)
    return _gdn_scan(u, wq, ak, egl, z, norm_g)


def _cmp_body(x_ref, pea_ref, peb_ref, wa_ref, wb_ref, w2_ref, o_ref):
    x = x_ref[0]
    a = _dot((x + pea_ref[...]).astype(BF16), wa_ref[...])
    bm = _dot((x + peb_ref[...]).astype(BF16), wb_ref[...])
    n = a.shape[0]
    h1 = a + pltpu.roll(bm, n - 1, axis=0)
    o_ref[0] = _dot(_silu(h1).astype(BF16), w2_ref[...]).astype(o_ref.dtype)


def _compress(x2, pea, peb, wa, wb, w2bd):
    b, n, wdt = x2.shape
    full = lambda s: pl.BlockSpec(s, lambda i: (0,) * len(s))
    return pl.pallas_call(
        _cmp_body,
        grid=(b,),
        in_specs=[pl.BlockSpec((1, n, wdt), lambda i: (i, 0, 0)),
                  full(pea.shape), full(peb.shape), full(wa.shape), full(wb.shape), full(w2bd.shape)],
        out_specs=pl.BlockSpec((1, n, LANES), lambda i: (i, 0, 0)),
        out_shape=jax.ShapeDtypeStruct((b, n, LANES), BF16),
        compiler_params=_cparams(("arbitrary",)),
        name="nsa_compress",
    )(x2, pea, peb, wa, wb, w2bd)


def _nsa_body(q_ref, kcmp_ref, vcmp_ref, ks_ref, vs_ref, kw_ref, vw_ref, gate_ref,
              ovl_ref, ng_ref, o_ref, *, seq):
    qb = pl.program_id(1)
    s0 = qb * Q_BLOCK
    n_cmp_rows = kcmp_ref.shape[1]
    n_blk = seq // SEL_BLOCK
    rq = NSA_REP * Q_BLOCK

    t_q = s0 + lax.broadcasted_iota(jnp.int32, (Q_BLOCK, 1), 0)
    lane = lax.broadcasted_iota(jnp.int32, (Q_BLOCK, LANES), 1)
    gates = _sigmoid(gate_ref[0])

    n_iota = lax.broadcasted_iota(jnp.int32, (Q_BLOCK, n_cmp_rows), 1)
    cmask = (n_iota * CMP_STRIDE + (CMP_LEN - 1)) <= t_q

    wstart = pl.multiple_of(jnp.maximum(s0 - WINDOW, 0), Q_BLOCK)
    kp = wstart + lax.broadcasted_iota(jnp.int32, (Q_BLOCK, WINDOW + Q_BLOCK), 1)
    wmask = (kp <= t_q) & (kp > t_q - WINDOW)

    cur = lax.shift_right_logical(t_q, 6)
    forced = (lane == 0) | (lane == cur) | (lane == cur - 1)
    valid = (lane * SEL_BLOCK <= t_q) & (lane < n_blk)
    jrow8 = lax.broadcasted_iota(jnp.int32, (8, Q_BLOCK), 0)

    n_chunks = qb // (SEL_CHUNK // Q_BLOCK) + 1

    groups_g = range(NSA_KV_HEADS)
    qs = [q_ref[0, g * NSA_REP:(g + 1) * NSA_REP].reshape(rq, LANES) for g in groups_g]
    lmasks = [(lane >= g * NSA_HEAD_DIM) & (lane < (g + 1) * NSA_HEAD_DIM) for g in groups_g]
    kv_lane = lax.broadcasted_iota(jnp.int32, (1, LANES), 1)
    own = [(kv_lane >= g * NSA_HEAD_DIM) & (kv_lane < (g + 1) * NSA_HEAD_DIM) for g in groups_g]
    den_lane = [(1 - g) * NSA_HEAD_DIM for g in groups_g]

    def with_ones(v, g):
        return jnp.where(own[g], v, jnp.ones_like(v))

    def compressed(g):
        q = qs[g]
        sc = _dot_nt(q, kcmp_ref[0]).reshape(NSA_REP, Q_BLOCK, n_cmp_rows)
        sc = jnp.where(cmask[None], sc, -jnp.inf)
        m = jnp.max(sc, axis=-1, keepdims=True)
        m = jnp.where(m == -jnp.inf, 0.0, m)
        p = jnp.exp(sc - m)
        p = p * (1.0 / jnp.maximum(jnp.sum(p, axis=-1, keepdims=True), 1e-30))
        o_cmp = _dot(p.reshape(rq, n_cmp_rows).astype(BF16), vcmp_ref[0])
        psum = p[0] + p[1] + p[2] + p[3]
        hi, mid, lo = _split3(psum)
        ovl = ovl_ref[...]
        imp = _dot(hi, ovl) + _dot(mid, ovl) + _dot(lo, ovl)
        return o_cmp.reshape(NSA_REP, Q_BLOCK, LANES), imp

    def select_blocks(imp):
        key = jnp.where(forced, jnp.inf, jnp.where(valid, imp, -jnp.inf))
        key_t = key.T[:n_blk, :]
        groups = [key_t[8 * v:8 * v + 8, :] for v in range(n_blk // 8)]
        cnts = [jnp.zeros((8, Q_BLOCK), F32) for _ in groups]
        for i in range(n_blk):
            row = key_t[i:i + 1, :]
            for v, grp in enumerate(groups):
                if v < i // 8:
                    beats = jnp.where(row > grp, 1.0, 0.0)
                elif v > i // 8:
                    beats = jnp.where(row >= grp, 1.0, 0.0)
                else:
                    beats = jnp.where(jrow8 > i % 8, jnp.where(row >= grp, 1.0, 0.0),
                                      jnp.where(row > grp, 1.0, 0.0))
                cnts[v] = cnts[v] + beats
        sel_t = jnp.where(jnp.concatenate(cnts, axis=0) < float(N_SELECT), 1.0, 0.0)
        if n_blk < LANES:
            sel_t = jnp.concatenate([sel_t, jnp.zeros((LANES - n_blk, Q_BLOCK), F32)], axis=0)
        return sel_t.T.astype(BF16)

    cmp_out = [compressed(g) for g in groups_g]
    sels = [select_blocks(imp) for _, imp in cmp_out]

    def sel_step(c, carry):
        k0 = pl.multiple_of(c * SEL_CHUNK, SEL_CHUNK)
        kch = ks_ref[0, pl.ds(k0, SEL_CHUNK), :]
        vch = vs_ref[0, pl.ds(k0, SEL_CHUNK), :]
        eb = lax.broadcasted_iota(jnp.int32, (LANES, SEL_CHUNK), 0)
        ek = lax.broadcasted_iota(jnp.int32, (LANES, SEL_CHUNK), 1)
        blk = c * (SEL_CHUNK // SEL_BLOCK) + lax.shift_right_logical(ek, 6)
        expand = jnp.where(eb == blk, 1.0, 0.0).astype(BF16)
        kidx = k0 + lax.broadcasted_iota(jnp.int32, (Q_BLOCK, SEL_CHUNK), 1)
        visible = kidx <= t_q
        ss = [_dot_nt(qs[g], kch).reshape(NSA_REP, Q_BLOCK, SEL_CHUNK) for g in groups_g]
        chosen = [_dot(sels[g], expand) > 0.5 for g in groups_g]
        ss = [jnp.where((chosen[g] & visible)[None], ss[g], NEG_BIG) for g in groups_g]
        m_new = [jnp.maximum(carry[g][0], jnp.max(ss[g], axis=-1, keepdims=True)) for g in groups_g]
        ps = [jnp.exp(ss[g] - m_new[g]) for g in groups_g]
        pvs = [_dot(ps[g].reshape(rq, SEL_CHUNK).astype(BF16), with_ones(vch, g)).reshape(NSA_REP, Q_BLOCK, LANES)
               for g in groups_g]
        return tuple((m_new[g], jnp.exp(carry[g][0] - m_new[g]) * carry[g][1] + pvs[g]) for g in groups_g)

    init = tuple((jnp.full((NSA_REP, Q_BLOCK, 1), NEG_BIG, F32),
                  jnp.zeros((NSA_REP, Q_BLOCK, LANES), F32)) for _ in groups_g)
    sel_out = lax.fori_loop(0, n_chunks, sel_step, init)

    kwin = kw_ref[0, pl.ds(wstart, WINDOW + Q_BLOCK), :]
    vwin = vw_ref[0, pl.ds(wstart, WINDOW + Q_BLOCK), :]
    sws = [_dot_nt(qs[g], kwin).reshape(NSA_REP, Q_BLOCK, WINDOW + Q_BLOCK) for g in groups_g]
    sws = [jnp.where(wmask[None], sw, NEG_BIG) for sw in sws]
    pws = [jnp.exp(sw - jnp.max(sw, axis=-1, keepdims=True)) for sw in sws]
    wins = [_dot(pws[g].reshape(rq, WINDOW + Q_BLOCK).astype(BF16), with_ones(vwin, g)).reshape(NSA_REP, Q_BLOCK, LANES)
            for g in groups_g]

    for g in groups_g:
        lmask = lmasks[g]
        o_cmp = cmp_out[g][0]
        acc_s = sel_out[g][1]
        o_sel = acc_s * (1.0 / acc_s[:, :, den_lane[g]:den_lane[g] + 1])
        o_win = wins[g] * (1.0 / wins[g][:, :, den_lane[g]:den_lane[g] + 1])

        ys = []
        for r in range(NSA_REP):
            col = 2 * GDN_HEADS + (g * NSA_REP + r) * 3
            o = (gates[:, col:col + 1] * o_cmp[r] + gates[:, col + 1:col + 2] * o_sel[r]
                 + gates[:, col + 2:col + 3] * o_win[r])
            o = jnp.where(lmask, o, 0.0)
            ms = jnp.sum(o * o, axis=-1, keepdims=True) * (1.0 / NSA_HEAD_DIM)
            ys.append(o * lax.rsqrt(ms + NORM_EPS) * ng_ref[...])
        for pair in range(NSA_REP // 2):
            a, bb = ys[2 * pair], ys[2 * pair + 1]
            if g == 0:
                tile = a + pltpu.roll(bb, NSA_HEAD_DIM, axis=1)
            else:
                tile = pltpu.roll(a, NSA_HEAD_DIM, axis=1) + bb
            c0 = g * NSA_REP * NSA_HEAD_DIM + pair * LANES
            o_ref[0, :, c0:c0 + LANES] = tile.astype(o_ref.dtype)


def _nsa(qpad, kcmp, vcmp, ks, vs, kw, vw, ba, ovl, ng_lane):
    b, _, t, _ = qpad.shape
    n = kcmp.shape[1]
    whole = lambda rows: pl.BlockSpec((1, rows, LANES), lambda i, j: (i, 0, 0))
    return pl.pallas_call(
        functools.partial(_nsa_body, seq=t),
        grid=(b, t // Q_BLOCK),
        in_specs=[
            pl.BlockSpec((1, NSA_HEADS, Q_BLOCK, LANES), lambda i, j: (i, 0, j, 0)),
            whole(n), whole(n), whole(t), whole(t), whole(t), whole(t),
            pl.BlockSpec((1, Q_BLOCK, LANES), lambda i, j: (i, j, 0)),
            pl.BlockSpec((n, LANES), lambda i, j: (0, 0)),
            pl.BlockSpec((1, LANES), lambda i, j: (0, 0)),
        ],
        out_specs=pl.BlockSpec((1, Q_BLOCK, NSA_WIDTH), lambda i, j: (i, j, 0)),
        out_shape=jax.ShapeDtypeStruct((b, t, NSA_WIDTH), BF16),
        compiler_params=_cparams(("arbitrary", "arbitrary")),
        name="nsa_attn",
    )(qpad, kcmp, vcmp, ks, vs, kw, vw, ba, ovl, ng_lane)


def _outproj_body(x_ref, mod_ref, yg_ref, yn_ref, w_ref, o_ref):
    y = _dot(yg_ref[0], w_ref[0:GDN_WIDTH, :]) + _dot(yn_ref[0], w_ref[GDN_WIDTH:, :])
    o_ref[0] = x_ref[0] + mod_ref[0, 2:3, :] * y


def _outproj(x, mod8, yg, yn, w, tm=1024):
    b, t, d = x.shape
    row = lambda n: pl.BlockSpec((1, tm, n), lambda i, j: (i, j, 0))
    return pl.pallas_call(
        _outproj_body,
        grid=(b, t // tm),
        in_specs=[row(d), pl.BlockSpec((1, 8, d), lambda i, j: (i, 0, 0)),
                  row(GDN_WIDTH), row(NSA_WIDTH),
                  pl.BlockSpec(w.shape, lambda i, j: (0, 0))],
        out_specs=row(d),
        out_shape=jax.ShapeDtypeStruct((b, t, d), F32),
        compiler_params=_cparams(("arbitrary", "arbitrary")),
        name="mix_outproj",
    )(x, mod8, yg, yn, w)


_IN_SIZES = (3 * GDN_WIDTH, GDN_WIDTH, GDN_HEADS, GDN_HEADS, NSA_WIDTH) + (NSA_KV_WIDTH,) * 6 + (3 * NSA_HEADS,)


def _prep_inproj_weight(w):
    offs = np.concatenate([[0], np.cumsum(_IN_SIZES)])
    seg = lambda i: w[:, int(offs[i]):int(offs[i + 1])]
    small = jnp.concatenate([seg(2), seg(3), seg(11)], axis=1)
    big = jnp.concatenate([seg(0), seg(1)] + [seg(i) for i in range(4, 11)]
                          + [small, jnp.zeros((w.shape[0], LANES - _SMALL_ROWS), w.dtype)], axis=1)
    return big.astype(BF16), small.T.astype(BF16)


def _prep_compress(pe, w1, w2):
    hid = CMP_HIDDEN
    g, dh = NSA_KV_HEADS, NSA_HEAD_DIM
    half = CMP_LEN // 2
    w1r = w1.reshape(CMP_LEN, dh, hid)
    eye = jnp.eye(g, dtype=w1.dtype)
    expand = lambda part: jnp.einsum("ldh,gk->lgdkh", part, eye).reshape(half * g * dh, g * hid)
    wa, wb = expand(w1r[:half]), expand(w1r[half:])
    pe_row = lambda part: jnp.broadcast_to(part[:, None, :], (half, g, dh)).reshape(1, half * g * dh)
    pea, peb = pe_row(pe[:half]), pe_row(pe[half:])
    w2bd = jnp.einsum("hd,gk->ghkd", w2, eye).reshape(g * hid, g * dh)
    return pea, peb, wa.astype(BF16), wb.astype(BF16), w2bd.astype(BF16)


def _overlap_matrix(n_rows, seq):
    n_cmp = (seq - CMP_LEN) // CMP_STRIDE + 1
    n_blk = seq // SEL_BLOCK
    n = np.arange(n_rows)[:, None]
    j = np.arange(LANES)[None, :]
    start, end = n * CMP_STRIDE, n * CMP_STRIDE + CMP_LEN - 1
    ovl = (start < j * SEL_BLOCK + SEL_BLOCK) & (end >= j * SEL_BLOCK) & (n < n_cmp) & (j < n_blk)
    return jnp.asarray(ovl.astype(np.float32), dtype=BF16)


def kernel(x, c, positions, ada_w, ada_b, norm_g, ffn_w_in, ffn_w_out, mix_w_in, gdn_conv_w, gdn_a_log, gdn_dt_bias, gdn_norm_g, cmp_pe_k, cmp_w1_k, cmp_w2_k, cmp_pe_v, cmp_w1_v, cmp_w2_v, nsa_norm_g, mix_w_out, final_norm_g):
    b, t, d = x.shape
    depth = ada_w.shape[0]
    d_ff = ffn_w_out.shape[2]
    assert t % (SEL_CHUNK * 4) == 0 and b <= 8

    c_pad = jnp.zeros((8, d), F32).at[:b].set(c.astype(F32))
    mod = _ada_mod(c_pad, ada_w.astype(F32), ada_b.astype(F32))
    mod = mod[:, :b].reshape(depth, b, 3, 3, d)
    mod8 = jnp.concatenate([mod, jnp.zeros((depth, b, 3, 5, d), F32)], axis=3)

    half = ROT_DIM // 2
    inv_freq = jnp.power(ROPE_THETA, -jnp.arange(half, dtype=F32) * (2.0 / ROT_DIM))
    dim = np.arange(LANES) % NSA_HEAD_DIM
    invf_lane = jnp.where(jnp.asarray(dim < ROT_DIM), inv_freq[dim % half], 0.0).reshape(1, LANES)
    cosf, sint = _rope_tables(positions.astype(F32).reshape(b, t, 1), invf_lane, tm=1024)

    n_rows = t // CMP_STRIDE
    ovl = _overlap_matrix(n_rows, t)
    fg = final_norm_g.reshape(1, d).astype(F32)

    for l in range(depth):
        wg = [ffn_w_in[l, i, :, :d_ff].astype(BF16) for i in range(2)]
        wu = [ffn_w_in[l, i, :, d_ff:].astype(BF16) for i in range(2)]
        wo = [ffn_w_out[l, i].astype(BF16) for i in range(2)]
        ng = lambda i: norm_g[l, i].reshape(1, d).astype(F32)

        x = _ffn(x, mod8[l, :, 0], ng(0), wg[0], wu[0], wo[0], fg, final_norm=False)

        w_big, w_small_t = _prep_inproj_weight(mix_w_in[l])
        (qkv, z, ba, bat, qpad, kc, vc, ks, vs, kw, vw) = _inproj(
            x, mod8[l, :, 1], ng(1), w_big, w_small_t, cosf, sint)

        neg_a = -jnp.exp(gdn_a_log[l].astype(F32))
        dtb = gdn_dt_bias[l].astype(F32)
        avec = jnp.zeros((8, LANES), F32).at[0, GDN_HEADS:2 * GDN_HEADS].set(neg_a)
        avec = avec.at[1, GDN_HEADS:2 * GDN_HEADS].set(dtb)
        avect = jnp.zeros((_SMALL_ROWS, LANES), F32).at[GDN_HEADS:2 * GDN_HEADS, 0].set(neg_a)
        avect = avect.at[GDN_HEADS:2 * GDN_HEADS, 1].set(dtb)
        y_gdn = _gdn(qkv, z, ba, bat, gdn_conv_w[l].astype(F32), avec, avect,
                     gdn_norm_g[l].reshape(1, GDN_HEAD_DIM).astype(F32))

        kcmp = _compress(kc.reshape(b, n_rows, CMP_STRIDE * LANES),
                         *_prep_compress(cmp_pe_k[l], cmp_w1_k[l], cmp_w2_k[l]))
        vcmp = _compress(vc.reshape(b, n_rows, CMP_STRIDE * LANES),
                         *_prep_compress(cmp_pe_v[l], cmp_w1_v[l], cmp_w2_v[l]))
        ng_lane = jnp.tile(nsa_norm_g[l].astype(F32), NSA_KV_HEADS).reshape(1, LANES)
        y_nsa = _nsa(qpad, kcmp, vcmp, ks, vs, kw, vw, ba, ovl, ng_lane)

        x = _outproj(x, mod8[l, :, 1], y_gdn, y_nsa, mix_w_out[l].astype(BF16))

        x = _ffn(x, mod8[l, :, 2], ng(2), wg[1], wu[1], wo[1], fg, final_norm=(l == depth - 1))
    return x
```

```python
import functools
import math

import jax
import jax.numpy as jnp
import numpy as np
from jax import lax
from jax.experimental import pallas as pl
from jax.experimental.pallas import tpu as pltpu

F32 = jnp.float32
BF16 = jnp.bfloat16

NORM_EPS = 1e-6
LANES = 128
GDN_HEADS = 4
GDN_HEAD_DIM = 128
GDN_WIDTH = GDN_HEADS * GDN_HEAD_DIM
GDN_CONV = 4
GDN_CHUNK = 64
GDN_PREP_STEP = 4 * GDN_CHUNK
GDN_SCAN_STEP = 2 * GDN_CHUNK
NSA_HEADS = 8
NSA_KV_HEADS = 2
NSA_REP = NSA_HEADS // NSA_KV_HEADS
NSA_HEAD_DIM = 64
NSA_WIDTH = NSA_HEADS * NSA_HEAD_DIM
NSA_KV_WIDTH = NSA_KV_HEADS * NSA_HEAD_DIM
CMP_LEN = 32
CMP_STRIDE = 16
CMP_HIDDEN = 2 * NSA_HEAD_DIM
SEL_BLOCK = 64
N_SELECT = 16
WINDOW = 512
Q_BLOCK = 128
SEL_CHUNK = 1024
ROPE_THETA = 500000.0
ROT_DIM = NSA_HEAD_DIM // 4
N_ADA = 9
NEG_BIG = -1e30
VMEM_LIMIT = 56 * 1024 * 1024

HIGHEST = lax.Precision.HIGHEST


def _cparams(sem):
    return pltpu.CompilerParams(dimension_semantics=sem, vmem_limit_bytes=VMEM_LIMIT)


def _dot(a, b):
    return jnp.dot(a, b, preferred_element_type=F32)


def _dot_nt(a, b):
    return lax.dot_general(a, b, (((1,), (1,)), ((), ())), preferred_element_type=F32)


def _dot_hi(a, b):
    return jnp.dot(a, b, preferred_element_type=F32, precision=HIGHEST)


def _split3(x):
    hi = x.astype(BF16)
    r1 = x - hi.astype(F32)
    mid = r1.astype(BF16)
    lo = (r1 - mid.astype(F32)).astype(BF16)
    return hi, mid, lo


def _sigmoid(x):
    return 1.0 / (1.0 + jnp.exp(-x))


def _silu(x):
    return x * _sigmoid(x)


def _rms_rows(x, g):
    ms = jnp.mean(x * x, axis=-1, keepdims=True)
    return x * lax.rsqrt(ms + NORM_EPS) * g


def _ada_body(c_ref, w_ref, b_ref, o_ref):
    cond = _silu(c_ref[...])
    o_ref[0] = _dot_hi(cond, w_ref[0]) + b_ref[0]


def _ada_mod(c_pad, ada_w, ada_b):
    depth, d, n = ada_w.shape
    tn = n // N_ADA
    rows = c_pad.shape[0]
    return pl.pallas_call(
        _ada_body,
        grid=(depth, n // tn),
        in_specs=[
            pl.BlockSpec((rows, d), lambda l, j: (0, 0)),
            pl.BlockSpec((1, d, tn), lambda l, j: (l, 0, j)),
            pl.BlockSpec((1, 1, tn), lambda l, j: (l, 0, j)),
        ],
        out_specs=pl.BlockSpec((1, rows, tn), lambda l, j: (l, 0, j)),
        out_shape=jax.ShapeDtypeStruct((depth, rows, n), F32),
        compiler_params=_cparams(("arbitrary", "arbitrary")),
        name="ada_mod",
    )(c_pad, ada_w, ada_b.reshape(depth, 1, n))


def _rope_body(pos_ref, invf_ref, cos_ref, sin_ref):
    ang = pos_ref[0] * invf_ref[...]
    cos_ref[0] = jnp.cos(ang)
    sin_ref[0] = jnp.sin(ang)


def _rope_tables(pos_f, invf_lane, tm):
    b, t, _ = pos_f.shape
    spec = pl.BlockSpec((1, tm, LANES), lambda i, j: (i, j, 0))
    return pl.pallas_call(
        _rope_body,
        grid=(b, t // tm),
        in_specs=[pl.BlockSpec((1, tm, 1), lambda i, j: (i, j, 0)),
                  pl.BlockSpec((1, LANES), lambda i, j: (0, 0))],
        out_specs=[spec, spec],
        out_shape=[jax.ShapeDtypeStruct((b, t, LANES), F32)] * 2,
        compiler_params=_cparams(("arbitrary", "arbitrary")),
        name="rope_tables",
    )(pos_f, invf_lane)


def _ffn_body(*refs, final_norm, with_mix, tf):
    if with_mix:
        (x_ref, mod_ref, g_ref, wi_ref, wo_ref, fg_ref,
         mmod_ref, yg_ref, yn_ref, wm_ref, o_ref, act_scr) = refs
        mix = _dot(yg_ref[0], wm_ref[0:GDN_WIDTH, :]) + _dot(yn_ref[0], wm_ref[GDN_WIDTH:, :])
        x = x_ref[0] + mmod_ref[0, 2:3, :] * mix
    else:
        x_ref, mod_ref, g_ref, wi_ref, wo_ref, fg_ref, o_ref, act_scr = refs
        x = x_ref[0]
    y = _rms_rows(x, g_ref[...])
    h = (y * (1.0 + mod_ref[0, 1:2, :]) + mod_ref[0, 0:1, :]).astype(BF16)
    ff = wo_ref.shape[0]
    for j in range(ff // tf):
        a = _dot(h, wi_ref[:, j * tf:(j + 1) * tf])
        u = _dot(h, wi_ref[:, ff + j * tf:ff + (j + 1) * tf])
        act_scr[:, j * tf:(j + 1) * tf] = (_silu(a) * u).astype(BF16)
    out = x + (0.5 * mod_ref[0, 2:3, :]) * _dot(act_scr[...], wo_ref[...])
    if final_norm:
        out = _rms_rows(out, fg_ref[...])
    o_ref[0] = out


def _ffn(x, mod8, g, w_in_all, w_out_all, layer, idx, fg, *, final_norm, mix=None, tm=1024, tf=256):
    b, t, d = x.shape
    ff = w_out_all.shape[2]
    row = lambda n: pl.BlockSpec((1, tm, n), lambda i, j: (i, j, 0))
    mod_spec = pl.BlockSpec((1, 8, d), lambda i, j: (i, 0, 0))
    vec_spec = pl.BlockSpec((1, d), lambda i, j: (0, 0))
    stacked = lambda r, c, *lead: pl.BlockSpec((None,) * len(lead) + (r, c), lambda i, j: lead + (0, 0),
                                               pipeline_mode=pl.Buffered(1))
    in_specs = [row(d), mod_spec, vec_spec, stacked(d, 2 * ff, layer, idx), stacked(ff, d, layer, idx), vec_spec]
    args = [x, mod8, g, w_in_all, w_out_all, fg]
    if mix is not None:
        mmod8, yg, yn, w_mix_all = mix
        in_specs += [mod_spec, row(GDN_WIDTH), row(NSA_WIDTH), stacked(d, d, layer)]
        args += [mmod8, yg, yn, w_mix_all]
    return pl.pallas_call(
        functools.partial(_ffn_body, final_norm=final_norm, with_mix=mix is not None, tf=tf),
        grid=(b, t // tm),
        in_specs=in_specs,
        out_specs=row(d),
        out_shape=jax.ShapeDtypeStruct((b, t, d), F32),
        scratch_shapes=[pltpu.VMEM((tm, ff), BF16)],
        compiler_params=_cparams(("arbitrary", "arbitrary")),
        name=("ffn_mix" if mix is not None else "ffn") + ("_final" if final_norm else ""),
    )(*args)


_O_QKV, _O_Z, _O_Q = 0, 1536, 2048
_O_KC, _O_VC, _O_KS, _O_VS, _O_KW, _O_VW, _O_SM = 2560, 2688, 2816, 2944, 3072, 3200, 3328
_IN_COLS = 3456
_SMALL_ROWS = 32


def _rope_tile(x, cosf, sint, first_half):
    fwd = pltpu.roll(x, 8, axis=1)
    bwd = pltpu.roll(x, LANES - 8, axis=1)
    return x * cosf + jnp.where(first_half, -bwd, fwd) * sint


def _inproj_body(x_ref, mod_ref, g_ref, w_ref, wst_ref, cos_ref, sin_ref,
                 qkv_ref, z_ref, ba_ref, bat_ref, qpad_ref,
                 kc_ref, vc_ref, ks_ref, vs_ref, kw_ref, vw_ref):
    y = _rms_rows(x_ref[0], g_ref[...])
    h = (y * (1.0 + mod_ref[0, 1:2, :]) + mod_ref[0, 0:1, :]).astype(BF16)
    cosf = cos_ref[0]
    sint = sin_ref[0]
    lane = lax.broadcasted_iota(jnp.int32, cosf.shape, 1)
    first_half = (lane % NSA_HEAD_DIM) < (ROT_DIM // 2)
    low = lane < NSA_HEAD_DIM

    for j in range(3):
        qkv_ref[0, :, j * 512:(j + 1) * 512] = _dot(h, w_ref[:, _O_QKV + j * 512:_O_QKV + (j + 1) * 512])
    z_ref[0] = _dot(h, w_ref[:, _O_Z:_O_Z + 512])
    ba_ref[0] = _dot(h, w_ref[:, _O_SM:_O_SM + LANES])
    bat_ref[0] = _dot_nt(wst_ref[...], h)

    scale = NSA_HEAD_DIM ** -0.5 * math.log2(math.e)
    def pair(off):
        res = _dot(h, w_ref[:, off:off + 2 * LANES])
        return res[:, :LANES], res[:, LANES:]

    q_tiles = pair(_O_Q) + pair(_O_Q + 2 * LANES)
    for k in range(NSA_HEADS // 2):
        tile = _rope_tile(q_tiles[k], cosf, sint, first_half) * scale
        swapped = pltpu.roll(tile, NSA_HEAD_DIM, axis=1)
        grp = (2 * k) // NSA_REP
        if grp == 0:
            even = jnp.where(low, tile, 0.0)
            odd = jnp.where(low, swapped, 0.0)
        else:
            even = jnp.where(low, 0.0, swapped)
            odd = jnp.where(low, 0.0, tile)
        qpad_ref[0, 2 * k] = even.astype(BF16)
        qpad_ref[0, 2 * k + 1] = odd.astype(BF16)

    kc, vc = pair(_O_KC)
    kc_ref[0] = _rope_tile(kc, cosf, sint, first_half)
    vc_ref[0] = vc
    ks, vs = pair(_O_KS)
    ks_ref[0] = _rope_tile(ks, cosf, sint, first_half).astype(BF16)
    vs_ref[0] = vs.astype(BF16)
    kw, vw = pair(_O_KW)
    kw_ref[0] = _rope_tile(kw, cosf, sint, first_half).astype(BF16)
    vw_ref[0] = vw.astype(BF16)


def _inproj(x, mod8, g, w, wst, cosf, sint, tm=512):
    b, t, d = x.shape
    row = lambda n: pl.BlockSpec((1, tm, n), lambda i, j: (i, j, 0))
    sds = lambda n, dt: jax.ShapeDtypeStruct((b, t, n), dt)
    return pl.pallas_call(
        _inproj_body,
        grid=(b, t // tm),
        in_specs=[
            row(d),
            pl.BlockSpec((1, 8, d), lambda i, j: (i, 0, 0)),
            pl.BlockSpec((1, d), lambda i, j: (0, 0)),
            pl.BlockSpec((d, _IN_COLS), lambda i, j: (0, 0)),
            pl.BlockSpec((_SMALL_ROWS, d), lambda i, j: (0, 0)),
            row(LANES), row(LANES),
        ],
        out_specs=[
            row(3 * GDN_WIDTH), row(GDN_WIDTH), row(LANES),
            pl.BlockSpec((1, _SMALL_ROWS, tm), lambda i, j: (i, 0, j)),
            pl.BlockSpec((1, NSA_HEADS, tm, LANES), lambda i, j: (i, 0, j, 0)),
            row(LANES), row(LANES), row(LANES), row(LANES), row(LANES), row(LANES),
        ],
        out_shape=[
            sds(3 * GDN_WIDTH, F32), sds(GDN_WIDTH, F32), sds(LANES, F32),
            jax.ShapeDtypeStruct((b, _SMALL_ROWS, t), F32),
            jax.ShapeDtypeStruct((b, NSA_HEADS, t, LANES), BF16),
            sds(LANES, F32), sds(LANES, F32),
            sds(LANES, BF16), sds(LANES, BF16), sds(LANES, BF16), sds(LANES, BF16),
        ],
        compiler_params=_cparams(("arbitrary", "arbitrary")),
        name="mix_inproj",
    )(x, mod8, g, w, wst, cosf, sint)


def _softplus(x):
    return jnp.maximum(x, 0.0) + jnp.log1p(jnp.exp(-jnp.abs(x)))


def _dot_inv(a, b):
    ah = a.astype(BF16)
    al = (a - ah.astype(F32)).astype(BF16)
    bh = b.astype(BF16)
    bl = (b - bh.astype(F32)).astype(BF16)
    m = a.shape[0]
    top = _dot(jnp.concatenate([ah, al], axis=0), bh)
    return top[:m] + top[m:] + _dot(ah, bl)


def _unit_lower_inverses(lows):
    n = lows[0].shape[0]
    r = lax.broadcasted_iota(jnp.int32, (n, n), 0)
    c = lax.broadcasted_iota(jnp.int32, (n, n), 1)
    eye = jnp.where(r == c, 1.0, 0.0)
    ps = [-low for low in lows]
    invs = [eye + p for p in ps]
    for _ in range(int(math.log2(n)) - 1):
        ps = [_dot_inv(p, p) for p in ps]
        invs = [inv + _dot_inv(inv, p) for inv, p in zip(invs, ps)]
    return invs


def _gdn_prep_body(qkv_ref, ba_ref, bat_ref, cw_ref, av_ref, avt_ref,
                   u_ref, wq_ref, ak_ref, egl_ref, buf):
    step = pl.program_id(1)
    c64 = GDN_CHUNK
    hd = GDN_HEAD_DIM
    nt = GDN_PREP_STEP

    @pl.when(step == 0)
    def _():
        buf[0:8, :] = jnp.zeros((8, 3 * GDN_WIDTH), F32)

    x = qkv_ref[0]
    buf[8:8 + nt, :] = x
    y = x * cw_ref[3:4, :]
    for j in range(GDN_CONV - 1):
        sh = GDN_CONV - 1 - j
        y = y + buf[8 - sh:8 - sh + nt, :] * cw_ref[j:j + 1, :]
    buf[0:8, :] = x[nt - 8:nt, :]
    y = _silu(y)

    ba = ba_ref[0]
    beta_all = _sigmoid(ba)
    g_col_all = av_ref[0:1, :] * _softplus(ba + av_ref[1:2, :])
    bat = bat_ref[0]
    g_row_all = avt_ref[:, 0:1] * _softplus(bat + avt_ref[:, 1:2])

    r64 = lax.broadcasted_iota(jnp.int32, (c64, c64), 0)
    k64 = lax.broadcasted_iota(jnp.int32, (c64, c64), 1)
    causal = r64 >= k64
    strict = r64 > k64
    tri = jnp.where(causal, 1.0, 0.0)
    tri_t = jnp.where(r64 <= k64, 1.0, 0.0)

    nch = nt // c64
    units = [(ch, h) for ch in range(nch) for h in range(GDN_HEADS)]
    rows_of = lambda ch: slice(ch * c64, (ch + 1) * c64)
    gc_cols = [_dot_hi(tri, g_col_all[rows_of(ch), :]) for ch in range(nch)]
    gc_rows = [_dot_hi(g_row_all[:, rows_of(ch)], tri_t) for ch in range(nch)]

    def l2n(v):
        return v * lax.rsqrt(jnp.sum(v * v, axis=-1, keepdims=True) + NORM_EPS)

    khs, kbs, decays, lows = [], [], [], []
    for ch, h in units:
        kh = l2n(y[rows_of(ch), GDN_WIDTH + h * hd:GDN_WIDTH + (h + 1) * hd])
        gcol = gc_cols[ch][:, GDN_HEADS + h:GDN_HEADS + h + 1]
        grow = gc_rows[ch][GDN_HEADS + h:GDN_HEADS + h + 1, :]
        decay = jnp.exp(jnp.where(causal, gcol - grow, -jnp.inf))
        kb = kh * beta_all[rows_of(ch), h:h + 1]
        low = jnp.where(strict, _dot_nt(kb.astype(BF16), kh.astype(BF16)) * decay, 0.0)
        khs.append(kh), kbs.append(kb), decays.append(decay), lows.append(low)

    invs = _unit_lower_inverses(lows)

    egl_rows = [[] for _ in range(nch)]
    for i, (ch, h) in enumerate(units):
        rows = rows_of(ch)
        kh, kb, decay = khs[i], kbs[i], decays[i]
        qh = l2n(y[rows, h * hd:(h + 1) * hd]) * (hd ** -0.5)
        vh = y[rows, 2 * GDN_WIDTH + h * hd:2 * GDN_WIDTH + (h + 1) * hd]
        gcol = gc_cols[ch][:, GDN_HEADS + h:GDN_HEADS + h + 1]
        glast = gc_rows[ch][GDN_HEADS + h:GDN_HEADS + h + 1, c64 - 1:c64]
        eg = jnp.exp(gcol)
        rhs = jnp.concatenate([vh * beta_all[rows, h:h + 1], kb * eg], axis=1)
        sol = _dot_inv(invs[i], rhs)
        attn = _dot_nt(qh.astype(BF16), kh.astype(BF16)) * decay
        kd_t = (kh * jnp.exp(glast - gcol)).T
        u_ref[0, h, rows, :] = sol[:, :hd]
        wq_ref[0, h, ch, 0:c64, :] = sol[:, hd:].astype(BF16)
        wq_ref[0, h, ch, c64:2 * c64, :] = (qh * eg).astype(BF16)
        ak_ref[0, h, ch, 0:c64, :] = attn.astype(BF16)
        ak_ref[0, h, ch, c64:c64 + hd, :] = kd_t.astype(BF16)
        egl_rows[ch].append(jnp.broadcast_to(jnp.exp(glast), (1, LANES)))
    for ch in range(nch):
        egl_ref[0, ch] = jnp.concatenate(egl_rows[ch] + [jnp.zeros((8 - GDN_HEADS, LANES), F32)], axis=0)


def _gdn_prep(qkv, ba, bat, conv_w, avec, avect):
    b, t, _ = qkv.shape
    nt = GDN_PREP_STEP
    nch = nt // GDN_CHUNK
    hd = GDN_HEAD_DIM
    row = lambda n: pl.BlockSpec((1, nt, n), lambda i, j: (i, j, 0))
    return pl.pallas_call(
        _gdn_prep_body,
        grid=(b, t // nt),
        in_specs=[
            row(3 * GDN_WIDTH), row(LANES),
            pl.BlockSpec((1, _SMALL_ROWS, nt), lambda i, j: (i, 0, j)),
            pl.BlockSpec((GDN_CONV, 3 * GDN_WIDTH), lambda i, j: (0, 0)),
            pl.BlockSpec((8, LANES), lambda i, j: (0, 0)),
            pl.BlockSpec((_SMALL_ROWS, LANES), lambda i, j: (0, 0)),
        ],
        out_specs=[
            pl.BlockSpec((1, GDN_HEADS, nt, hd), lambda i, j: (i, 0, j, 0)),
            pl.BlockSpec((1, GDN_HEADS, nch, 2 * GDN_CHUNK, hd), lambda i, j: (i, 0, j, 0, 0)),
            pl.BlockSpec((1, GDN_HEADS, nch, GDN_CHUNK + hd, GDN_CHUNK), lambda i, j: (i, 0, j, 0, 0)),
            pl.BlockSpec((1, nch, 8, LANES), lambda i, j: (i, j, 0, 0)),
        ],
        out_shape=[
            jax.ShapeDtypeStruct((b, GDN_HEADS, t, hd), F32),
            jax.ShapeDtypeStruct((b, GDN_HEADS, t // GDN_CHUNK, 2 * GDN_CHUNK, hd), BF16),
            jax.ShapeDtypeStruct((b, GDN_HEADS, t // GDN_CHUNK, GDN_CHUNK + hd, GDN_CHUNK), BF16),
            jax.ShapeDtypeStruct((b, t // GDN_CHUNK, 8, LANES), F32),
        ],
        scratch_shapes=[pltpu.VMEM((8 + nt, 3 * GDN_WIDTH), F32)],
        compiler_params=_cparams(("arbitrary", "arbitrary")),
        name="gdn_prep",
    )(qkv, ba, bat, conv_w, avec, avect)


def _gdn_scan_body(u_ref, wq_ref, ak_ref, egl_ref, z_ref, ng_ref, o_ref, s_scr):
    c64 = GDN_CHUNK
    hd = GDN_HEAD_DIM
    nb = u_ref.shape[0]

    @pl.when(pl.program_id(0) == 0)
    def _():
        s_scr[...] = jnp.zeros_like(s_scr)

    chains = [(b, h) for b in range(nb) for h in range(GDN_HEADS)]
    states = [s_scr[b, h] for b, h in chains]
    for ch in range(GDN_SCAN_STEP // c64):
        rows = slice(ch * c64, (ch + 1) * c64)
        r1s = [_dot(wq_ref[b, h, ch], s.astype(BF16)) for (b, h), s in zip(chains, states)]
        vns = [(u_ref[b, h, rows, :] - r1[0:c64]).astype(BF16) for (b, h), r1 in zip(chains, r1s)]
        r2s = [_dot(ak_ref[b, h, ch], vn) for (b, h), vn in zip(chains, vns)]
        states = [s * egl_ref[b, ch, h:h + 1, :] + r2[c64:c64 + hd]
                  for (b, h), s, r2 in zip(chains, states, r2s)]
        for (b, h), r1, r2 in zip(chains, r1s, r2s):
            on = _rms_rows(r1[c64:2 * c64] + r2[0:c64], ng_ref[...])
            zh = z_ref[b, rows, h * hd:(h + 1) * hd]
            o_ref[b, rows, h * hd:(h + 1) * hd] = (on * _silu(zh)).astype(o_ref.dtype)
    for (b, h), s in zip(chains, states):
        s_scr[b, h] = s


def _gdn_scan(u, wq, ak, egl, z, norm_g):
    b, _, t, hd = u.shape
    nt = GDN_SCAN_STEP
    nch = nt // GDN_CHUNK
    return pl.pallas_call(
        _gdn_scan_body,
        grid=(t // nt,),
        in_specs=[
            pl.BlockSpec((b, GDN_HEADS, nt, hd), lambda j: (0, 0, j, 0)),
            pl.BlockSpec((b, GDN_HEADS, nch, 2 * GDN_CHUNK, hd), lambda j: (0, 0, j, 0, 0)),
            pl.BlockSpec((b, GDN_HEADS, nch, GDN_CHUNK + hd, GDN_CHUNK), lambda j: (0, 0, j, 0, 0)),
            pl.BlockSpec((b, nch, 8, LANES), lambda j: (0, j, 0, 0)),
            pl.BlockSpec((b, nt, GDN_WIDTH), lambda j: (0, j, 0)),
            pl.BlockSpec((1, hd), lambda j: (0, 0)),
        ],
        out_specs=pl.BlockSpec((b, nt, GDN_WIDTH), lambda j: (0, j, 0)),
        out_shape=jax.ShapeDtypeStruct((b, t, GDN_WIDTH), BF16),
        scratch_shapes=[pltpu.VMEM((b, GDN_HEADS, hd, hd), F32)],
        compiler_params=_cparams(("arbitrary",)),
        name="gdn_scan",
    )(u, wq, ak, egl, z, norm_g)


def _gdn(qkv, z, ba, bat, conv_w, avec, avect, norm_g):
    u, wq, ak, egl = _gdn_prep(qkv, ba, bat, conv_w, avec, avect)
    return _gdn_scan(u, wq, ak, egl, z, norm_g)


def _cmp_body(x_ref, pea_ref, peb_ref, wa_ref, wb_ref, w2_ref, o_ref):
    x = x_ref[0]
    a = _dot((x + pea_ref[...]).astype(BF16), wa_ref[...])
    bm = _dot((x + peb_ref[...]).astype(BF16), wb_ref[...])
    n = a.shape[0]
    h1 = a + pltpu.roll(bm, n - 1, axis=0)
    o_ref[0] = _dot(_silu(h1).astype(BF16), w2_ref[...]).astype(o_ref.dtype)


def _compress(x2, pea, peb, wa, wb, w2bd):
    b, n, wdt = x2.shape
    full = lambda s: pl.BlockSpec(s, lambda i: (0,) * len(s))
    return pl.pallas_call(
        _cmp_body,
        grid=(b,),
        in_specs=[pl.BlockSpec((1, n, wdt), lambda i: (i, 0, 0)),
                  full(pea.shape), full(peb.shape), full(wa.shape), full(wb.shape), full(w2bd.shape)],
        out_specs=pl.BlockSpec((1, n, LANES), lambda i: (i, 0, 0)),
        out_shape=jax.ShapeDtypeStruct((b, n, LANES), BF16),
        compiler_params=_cparams(("arbitrary",)),
        name="nsa_compress",
    )(x2, pea, peb, wa, wb, w2bd)


def _nsa_body(q_ref, kcmp_ref, vcmp_ref, ks_ref, vs_ref, kw_ref, vw_ref, gate_ref,
              ovl_ref, ng_ref, o_ref, *, seq):
    qb = pl.program_id(1)
    s0 = qb * Q_BLOCK
    n_cmp_rows = kcmp_ref.shape[1]
    n_blk = seq // SEL_BLOCK
    rq = NSA_REP * Q_BLOCK

    t_q = s0 + lax.broadcasted_iota(jnp.int32, (Q_BLOCK, 1), 0)
    lane = lax.broadcasted_iota(jnp.int32, (Q_BLOCK, LANES), 1)
    gates = _sigmoid(gate_ref[0])

    n_iota = lax.broadcasted_iota(jnp.int32, (Q_BLOCK, n_cmp_rows), 1)
    cmask = (n_iota * CMP_STRIDE + (CMP_LEN - 1)) <= t_q

    wstart = pl.multiple_of(jnp.maximum(s0 - WINDOW, 0), Q_BLOCK)
    kp = wstart + lax.broadcasted_iota(jnp.int32, (Q_BLOCK, WINDOW + Q_BLOCK), 1)
    wmask = (kp <= t_q) & (kp > t_q - WINDOW)

    cur = lax.shift_right_logical(t_q, 6)
    forced = (lane == 0) | (lane == cur) | (lane == cur - 1)
    valid = (lane * SEL_BLOCK <= t_q) & (lane < n_blk)
    jrow8 = lax.broadcasted_iota(jnp.int32, (8, Q_BLOCK), 0)

    n_chunks = qb // (SEL_CHUNK // Q_BLOCK) + 1

    groups_g = range(NSA_KV_HEADS)
    qs = [q_ref[0, g * NSA_REP:(g + 1) * NSA_REP].reshape(rq, LANES) for g in groups_g]
    lmasks = [(lane >= g * NSA_HEAD_DIM) & (lane < (g + 1) * NSA_HEAD_DIM) for g in groups_g]
    kv_lane = lax.broadcasted_iota(jnp.int32, (1, LANES), 1)
    own = [(kv_lane >= g * NSA_HEAD_DIM) & (kv_lane < (g + 1) * NSA_HEAD_DIM) for g in groups_g]
    den_lane = [(1 - g) * NSA_HEAD_DIM for g in groups_g]

    def with_ones(v, g):
        return jnp.where(own[g], v, jnp.ones_like(v))

    def compressed(g, sc):
        sc = jnp.where(cmask[None], sc.reshape(NSA_REP, Q_BLOCK, n_cmp_rows), -jnp.inf)
        m = jnp.max(sc, axis=-1, keepdims=True)
        m = jnp.where(m == -jnp.inf, 0.0, m)
        p = jnp.exp2(sc - m)
        p = p * (1.0 / jnp.maximum(jnp.sum(p, axis=-1, keepdims=True), 1e-30))
        o_cmp = _dot(p.reshape(rq, n_cmp_rows).astype(BF16), vcmp_ref[0])
        psum = p[0] + p[1] + p[2] + p[3]
        hi, mid, lo = _split3(psum)
        ovl = ovl_ref[...]
        imp = _dot(hi, ovl) + _dot(mid, ovl) + _dot(lo, ovl)
        return o_cmp.reshape(NSA_REP, Q_BLOCK, LANES), imp

    def select_blocks(imp):
        key = jnp.where(forced, jnp.inf, jnp.where(valid, imp, -jnp.inf))
        key_t = key.T[:n_blk, :]
        groups = [key_t[8 * v:8 * v + 8, :] for v in range(n_blk // 8)]
        cnts = [jnp.zeros((8, Q_BLOCK), F32) for _ in groups]
        for i in range(n_blk):
            row = key_t[i:i + 1, :]
            for v, grp in enumerate(groups):
                if v < i // 8:
                    beats = jnp.where(row > grp, 1.0, 0.0)
                elif v > i // 8:
                    beats = jnp.where(row >= grp, 1.0, 0.0)
                else:
                    beats = jnp.where(jrow8 > i % 8, jnp.where(row >= grp, 1.0, 0.0),
                                      jnp.where(row > grp, 1.0, 0.0))
                cnts[v] = cnts[v] + beats
        sel_t = jnp.where(jnp.concatenate(cnts, axis=0) < float(N_SELECT), 1.0, 0.0)
        if n_blk < LANES:
            sel_t = jnp.concatenate([sel_t, jnp.zeros((LANES - n_blk, Q_BLOCK), F32)], axis=0)
        return sel_t.T.astype(BF16)

    def sel_scores(c, g):
        k0 = pl.multiple_of(c * SEL_CHUNK, SEL_CHUNK)
        return _dot_nt(qs[g], ks_ref[0, pl.ds(k0, SEL_CHUNK), :])

    def sel_shared(c):
        k0 = pl.multiple_of(c * SEL_CHUNK, SEL_CHUNK)
        vch = vs_ref[0, pl.ds(k0, SEL_CHUNK), :]
        eb = lax.broadcasted_iota(jnp.int32, (LANES, SEL_CHUNK), 0)
        ek = lax.broadcasted_iota(jnp.int32, (LANES, SEL_CHUNK), 1)
        blk = c * (SEL_CHUNK // SEL_BLOCK) + lax.shift_right_logical(ek, 6)
        expand = jnp.where(eb == blk, 1.0, 0.0).astype(BF16)
        kidx = k0 + lax.broadcasted_iota(jnp.int32, (Q_BLOCK, SEL_CHUNK), 1)
        hidden = jnp.where(kidx <= t_q, 0.0, NEG_BIG)
        return vch, expand, hidden

    def sel_update(g, shared, scores, state):
        vch, expand, hidden = shared
        m_old, acc = state
        bias = (_dot(sels[g], expand) - 1.0) * (-NEG_BIG) + hidden
        s = scores.reshape(NSA_REP, Q_BLOCK, SEL_CHUNK) + bias[None]
        m_new = jnp.maximum(m_old, jnp.max(s, axis=-1, keepdims=True))
        p = jnp.exp2(s - m_new)
        pv = _dot(p.reshape(rq, SEL_CHUNK).astype(BF16), with_ones(vch, g)).reshape(NSA_REP, Q_BLOCK, LANES)
        return m_new, jnp.exp2(m_old - m_new) * acc + pv

    kwin = kw_ref[0, pl.ds(wstart, WINDOW + Q_BLOCK), :]
    vwin = vw_ref[0, pl.ds(wstart, WINDOW + Q_BLOCK), :]
    cmp_scores = [_dot_nt(qs[g], kcmp_ref[0]) for g in groups_g]
    win_scores = [_dot_nt(qs[g], kwin) for g in groups_g]
    first_scores = tuple(sel_scores(0, g) for g in groups_g)

    cmp_out = [compressed(g, cmp_scores[g]) for g in groups_g]

    wbias = jnp.where(wmask, 0.0, NEG_BIG)
    sws = [win_scores[g].reshape(NSA_REP, Q_BLOCK, WINDOW + Q_BLOCK) + wbias[None] for g in groups_g]
    pws = [jnp.exp2(sw - jnp.max(sw, axis=-1, keepdims=True)) for sw in sws]
    wins = [_dot(pws[g].reshape(rq, WINDOW + Q_BLOCK).astype(BF16), with_ones(vwin, g)).reshape(NSA_REP, Q_BLOCK, LANES)
            for g in groups_g]

    sels = [select_blocks(imp) for _, imp in cmp_out]

    def sel_step(c, states):
        shared = sel_shared(c)
        scores = [sel_scores(c, g) for g in groups_g]
        return tuple(sel_update(g, shared, scores[g], states[g]) for g in groups_g)

    init = tuple((jnp.full((NSA_REP, Q_BLOCK, 1), NEG_BIG, F32),
                  jnp.zeros((NSA_REP, Q_BLOCK, LANES), F32)) for _ in groups_g)
    shared0 = sel_shared(0)
    states = tuple(sel_update(g, shared0, first_scores[g], init[g]) for g in groups_g)
    sel_out = lax.fori_loop(1, n_chunks, sel_step, states)

    for g in groups_g:
        lmask = lmasks[g]
        o_cmp = cmp_out[g][0]
        acc_s = sel_out[g][1]
        o_sel = acc_s * (1.0 / acc_s[:, :, den_lane[g]:den_lane[g] + 1])
        o_win = wins[g] * (1.0 / wins[g][:, :, den_lane[g]:den_lane[g] + 1])

        ys = []
        for r in range(NSA_REP):
            col = 2 * GDN_HEADS + (g * NSA_REP + r) * 3
            o = (gates[:, col:col + 1] * o_cmp[r] + gates[:, col + 1:col + 2] * o_sel[r]
                 + gates[:, col + 2:col + 3] * o_win[r])
            o = jnp.where(lmask, o, 0.0)
            ms = jnp.sum(o * o, axis=-1, keepdims=True) * (1.0 / NSA_HEAD_DIM)
            ys.append(o * lax.rsqrt(ms + NORM_EPS) * ng_ref[...])
        for pair in range(NSA_REP // 2):
            a, bb = ys[2 * pair], ys[2 * pair + 1]
            if g == 0:
                tile = a + pltpu.roll(bb, NSA_HEAD_DIM, axis=1)
            else:
                tile = pltpu.roll(a, NSA_HEAD_DIM, axis=1) + bb
            c0 = g * NSA_REP * NSA_HEAD_DIM + pair * LANES
            o_ref[0, :, c0:c0 + LANES] = tile.astype(o_ref.dtype)


def _nsa(qpad, kcmp, vcmp, ks, vs, kw, vw, ba, ovl, ng_lane):
    b, _, t, _ = qpad.shape
    n = kcmp.shape[1]
    whole = lambda rows: pl.BlockSpec((1, rows, LANES), lambda i, j: (i, 0, 0))
    return pl.pallas_call(
        functools.partial(_nsa_body, seq=t),
        grid=(b, t // Q_BLOCK),
        in_specs=[
            pl.BlockSpec((1, NSA_HEADS, Q_BLOCK, LANES), lambda i, j: (i, 0, j, 0)),
            whole(n), whole(n), whole(t), whole(t), whole(t), whole(t),
            pl.BlockSpec((1, Q_BLOCK, LANES), lambda i, j: (i, j, 0)),
            pl.BlockSpec((n, LANES), lambda i, j: (0, 0)),
            pl.BlockSpec((1, LANES), lambda i, j: (0, 0)),
        ],
        out_specs=pl.BlockSpec((1, Q_BLOCK, NSA_WIDTH), lambda i, j: (i, j, 0)),
        out_shape=jax.ShapeDtypeStruct((b, t, NSA_WIDTH), BF16),
        compiler_params=_cparams(("arbitrary", "arbitrary")),
        name="nsa_attn",
    )(qpad, kcmp, vcmp, ks, vs, kw, vw, ba, ovl, ng_lane)


_IN_SIZES = (3 * GDN_WIDTH, GDN_WIDTH, GDN_HEADS, GDN_HEADS, NSA_WIDTH) + (NSA_KV_WIDTH,) * 6 + (3 * NSA_HEADS,)


def _prep_inproj_weight(w):
    offs = np.concatenate([[0], np.cumsum(_IN_SIZES)])
    seg = lambda i: w[:, int(offs[i]):int(offs[i + 1])]
    small = jnp.concatenate([seg(2), seg(3), seg(11)], axis=1)
    big = jnp.concatenate([seg(0), seg(1)] + [seg(i) for i in range(4, 11)]
                          + [small, jnp.zeros((w.shape[0], LANES - _SMALL_ROWS), w.dtype)], axis=1)
    return big.astype(BF16), small.T.astype(BF16)


def _prep_compress(pe, w1, w2):
    hid = CMP_HIDDEN
    g, dh = NSA_KV_HEADS, NSA_HEAD_DIM
    half = CMP_LEN // 2
    w1r = w1.reshape(CMP_LEN, dh, hid)
    eye = jnp.eye(g, dtype=w1.dtype)
    expand = lambda part: jnp.einsum("ldh,gk->lgdkh", part, eye).reshape(half * g * dh, g * hid)
    wa, wb = expand(w1r[:half]), expand(w1r[half:])
    pe_row = lambda part: jnp.broadcast_to(part[:, None, :], (half, g, dh)).reshape(1, half * g * dh)
    pea, peb = pe_row(pe[:half]), pe_row(pe[half:])
    w2bd = jnp.einsum("hd,gk->ghkd", w2, eye).reshape(g * hid, g * dh)
    return pea, peb, wa.astype(BF16), wb.astype(BF16), w2bd.astype(BF16)


def _overlap_matrix(n_rows, seq):
    n_cmp = (seq - CMP_LEN) // CMP_STRIDE + 1
    n_blk = seq // SEL_BLOCK
    n = np.arange(n_rows)[:, None]
    j = np.arange(LANES)[None, :]
    start, end = n * CMP_STRIDE, n * CMP_STRIDE + CMP_LEN - 1
    ovl = (start < j * SEL_BLOCK + SEL_BLOCK) & (end >= j * SEL_BLOCK) & (n < n_cmp) & (j < n_blk)
    return jnp.asarray(ovl.astype(np.float32), dtype=BF16)


def kernel(x, c, positions, ada_w, ada_b, norm_g, ffn_w_in, ffn_w_out, mix_w_in, gdn_conv_w, gdn_a_log, gdn_dt_bias, gdn_norm_g, cmp_pe_k, cmp_w1_k, cmp_w2_k, cmp_pe_v, cmp_w1_v, cmp_w2_v, nsa_norm_g, mix_w_out, final_norm_g):
    b, t, d = x.shape
    depth = ada_w.shape[0]
    assert t % SEL_CHUNK == 0 and t >= WINDOW + Q_BLOCK and b <= 8

    c_pad = jnp.zeros((8, d), F32).at[:b].set(c.astype(F32))
    mod = _ada_mod(c_pad, ada_w.astype(F32), ada_b.astype(F32))
    mod = mod[:, :b].reshape(depth, b, 3, 3, d)
    mod8 = jnp.concatenate([mod, jnp.zeros((depth, b, 3, 5, d), F32)], axis=3)

    half = ROT_DIM // 2
    inv_freq = jnp.power(ROPE_THETA, -jnp.arange(half, dtype=F32) * (2.0 / ROT_DIM))
    dim = np.arange(LANES) % NSA_HEAD_DIM
    invf_lane = jnp.where(jnp.asarray(dim < ROT_DIM), inv_freq[dim % half], 0.0).reshape(1, LANES)
    cosf, sint = _rope_tables(positions.astype(F32).reshape(b, t, 1), invf_lane, tm=1024)

    n_rows = t // CMP_STRIDE
    ovl = _overlap_matrix(n_rows, t)
    fg = final_norm_g.reshape(1, d).astype(F32)

    w_in_all = ffn_w_in.astype(BF16)
    w_out_all = ffn_w_out.astype(BF16)
    w_mix_all = mix_w_out.astype(BF16)

    for l in range(depth):
        ng = lambda i: norm_g[l, i].reshape(1, d).astype(F32)

        x = _ffn(x, mod8[l, :, 0], ng(0), w_in_all, w_out_all, l, 0, fg, final_norm=False)

        w_big, w_small_t = _prep_inproj_weight(mix_w_in[l])
        (qkv, z, ba, bat, qpad, kc, vc, ks, vs, kw, vw) = _inproj(
            x, mod8[l, :, 1], ng(1), w_big, w_small_t, cosf, sint)

        neg_a = -jnp.exp(gdn_a_log[l].astype(F32))
        dtb = gdn_dt_bias[l].astype(F32)
        avec = jnp.zeros((8, LANES), F32).at[0, GDN_HEADS:2 * GDN_HEADS].set(neg_a)
        avec = avec.at[1, GDN_HEADS:2 * GDN_HEADS].set(dtb)
        avect = jnp.zeros((_SMALL_ROWS, LANES), F32).at[GDN_HEADS:2 * GDN_HEADS, 0].set(neg_a)
        avect = avect.at[GDN_HEADS:2 * GDN_HEADS, 1].set(dtb)
        y_gdn = _gdn(qkv, z, ba, bat, gdn_conv_w[l].astype(F32), avec, avect,
                     gdn_norm_g[l].reshape(1, GDN_HEAD_DIM).astype(F32))

        kcmp = _compress(kc.reshape(b, n_rows, CMP_STRIDE * LANES),
                         *_prep_compress(cmp_pe_k[l], cmp_w1_k[l], cmp_w2_k[l]))
        vcmp = _compress(vc.reshape(b, n_rows, CMP_STRIDE * LANES),
                         *_prep_compress(cmp_pe_v[l], cmp_w1_v[l], cmp_w2_v[l]))
        ng_lane = jnp.tile(nsa_norm_g[l].astype(F32), NSA_KV_HEADS).reshape(1, LANES)
        y_nsa = _nsa(qpad, kcmp, vcmp, ks, vs, kw, vw, ba, ovl, ng_lane)

        x = _ffn(x, mod8[l, :, 2], ng(2), w_in_all, w_out_all, l, 1, fg, final_norm=(l == depth - 1),
                 mix=(mod8[l, :, 1], y_gdn, y_nsa, w_mix_all))
    return x
```

```python
import functools
import math

import jax
import jax.numpy as jnp
import numpy as np
from jax import lax
from jax.experimental import pallas as pl
from jax.experimental.pallas import tpu as pltpu

F32 = jnp.float32
BF16 = jnp.bfloat16

NORM_EPS = 1e-6
LANES = 128
GDN_HEADS = 4
GDN_HEAD_DIM = 128
GDN_WIDTH = GDN_HEADS * GDN_HEAD_DIM
GDN_CONV = 4
GDN_CHUNK = 64
GDN_PREP_STEP = 4 * GDN_CHUNK
GDN_SCAN_STEP = 2 * GDN_CHUNK
NSA_HEADS = 8
NSA_KV_HEADS = 2
NSA_REP = NSA_HEADS // NSA_KV_HEADS
NSA_HEAD_DIM = 64
NSA_WIDTH = NSA_HEADS * NSA_HEAD_DIM
NSA_KV_WIDTH = NSA_KV_HEADS * NSA_HEAD_DIM
CMP_LEN = 32
CMP_STRIDE = 16
CMP_HIDDEN = 2 * NSA_HEAD_DIM
SEL_BLOCK = 64
N_SELECT = 16
WINDOW = 512
Q_BLOCK = 128
SEL_CHUNK = 1024
SEL_LANES = SEL_CHUNK // SEL_BLOCK


def sel_lane_base(g):
    return NSA_HEAD_DIM if g == 0 else 0
ROPE_THETA = 500000.0
ROT_DIM = NSA_HEAD_DIM // 4
N_ADA = 9
NEG_BIG = -1e30
VMEM_LIMIT = 56 * 1024 * 1024

HIGHEST = lax.Precision.HIGHEST


def _cparams(sem):
    return pltpu.CompilerParams(dimension_semantics=sem, vmem_limit_bytes=VMEM_LIMIT)


def _dot(a, b):
    return jnp.dot(a, b, preferred_element_type=F32)


def _dot_nt(a, b):
    return lax.dot_general(a, b, (((1,), (1,)), ((), ())), preferred_element_type=F32)


def _dot_hi(a, b):
    return jnp.dot(a, b, preferred_element_type=F32, precision=HIGHEST)


def _split3(x):
    hi = x.astype(BF16)
    r1 = x - hi.astype(F32)
    mid = r1.astype(BF16)
    lo = (r1 - mid.astype(F32)).astype(BF16)
    return hi, mid, lo


def _sigmoid(x):
    return 1.0 / (1.0 + jnp.exp(-x))


def _silu(x):
    return x * _sigmoid(x)


def _rms_rows(x, g):
    ms = jnp.mean(x * x, axis=-1, keepdims=True)
    return x * lax.rsqrt(ms + NORM_EPS) * g


def _ada_body(c_ref, w_ref, b_ref, o_ref):
    cond = _silu(c_ref[...])
    o_ref[0] = _dot_hi(cond, w_ref[0]) + b_ref[0]


def _ada_mod(c_pad, ada_w, ada_b):
    depth, d, n = ada_w.shape
    tn = n // N_ADA
    rows = c_pad.shape[0]
    return pl.pallas_call(
        _ada_body,
        grid=(depth, n // tn),
        in_specs=[
            pl.BlockSpec((rows, d), lambda l, j: (0, 0)),
            pl.BlockSpec((1, d, tn), lambda l, j: (l, 0, j)),
            pl.BlockSpec((1, 1, tn), lambda l, j: (l, 0, j)),
        ],
        out_specs=pl.BlockSpec((1, rows, tn), lambda l, j: (l, 0, j)),
        out_shape=jax.ShapeDtypeStruct((depth, rows, n), F32),
        compiler_params=_cparams(("arbitrary", "arbitrary")),
        name="ada_mod",
    )(c_pad, ada_w, ada_b.reshape(depth, 1, n))


def _rope_body(pos_ref, invf_ref, cos_ref, sin_ref):
    ang = pos_ref[0] * invf_ref[...]
    cos_ref[0] = jnp.cos(ang)
    sin_ref[0] = jnp.sin(ang)


def _rope_tables(pos_f, invf_lane, tm):
    b, t, _ = pos_f.shape
    spec = pl.BlockSpec((1, tm, LANES), lambda i, j: (i, j, 0))
    return pl.pallas_call(
        _rope_body,
        grid=(b, t // tm),
        in_specs=[pl.BlockSpec((1, tm, 1), lambda i, j: (i, j, 0)),
                  pl.BlockSpec((1, LANES), lambda i, j: (0, 0))],
        out_specs=[spec, spec],
        out_shape=[jax.ShapeDtypeStruct((b, t, LANES), F32)] * 2,
        compiler_params=_cparams(("arbitrary", "arbitrary")),
        name="rope_tables",
    )(pos_f, invf_lane)


def _ffn_body(*refs, final_norm, with_mix, tf):
    if with_mix:
        (x_ref, mod_ref, g_ref, wi_ref, wo_ref, fg_ref,
         mmod_ref, yg_ref, yn_ref, wm_ref, o_ref, act_scr) = refs
        mix = _dot(yg_ref[0], wm_ref[0:GDN_WIDTH, :]) + _dot(yn_ref[0], wm_ref[GDN_WIDTH:, :])
        x = x_ref[0] + mmod_ref[0, 2:3, :] * mix
    else:
        x_ref, mod_ref, g_ref, wi_ref, wo_ref, fg_ref, o_ref, act_scr = refs
        x = x_ref[0]
    y = _rms_rows(x, g_ref[...])
    h = (y * (1.0 + mod_ref[0, 1:2, :]) + mod_ref[0, 0:1, :]).astype(BF16)
    ff = wo_ref.shape[0]
    for j in range(ff // tf):
        a = _dot(h, wi_ref[:, j * tf:(j + 1) * tf])
        u = _dot(h, wi_ref[:, ff + j * tf:ff + (j + 1) * tf])
        act_scr[:, j * tf:(j + 1) * tf] = (_silu(a) * u).astype(BF16)
    out = x + (0.5 * mod_ref[0, 2:3, :]) * _dot(act_scr[...], wo_ref[...])
    if final_norm:
        out = _rms_rows(out, fg_ref[...])
    o_ref[0] = out


def _ffn(x, mod8, g, w_in_all, w_out_all, layer, idx, fg, *, final_norm, mix=None, tm=1024, tf=256):
    b, t, d = x.shape
    ff = w_out_all.shape[2]
    row = lambda n: pl.BlockSpec((1, tm, n), lambda i, j: (i, j, 0))
    mod_spec = pl.BlockSpec((1, 8, d), lambda i, j: (i, 0, 0))
    vec_spec = pl.BlockSpec((1, d), lambda i, j: (0, 0))
    stacked = lambda r, c, *lead: pl.BlockSpec((None,) * len(lead) + (r, c), lambda i, j: lead + (0, 0),
                                               pipeline_mode=pl.Buffered(1))
    in_specs = [row(d), mod_spec, vec_spec, stacked(d, 2 * ff, layer, idx), stacked(ff, d, layer, idx), vec_spec]
    args = [x, mod8, g, w_in_all, w_out_all, fg]
    if mix is not None:
        mmod8, yg, yn, w_mix_all = mix
        in_specs += [mod_spec, row(GDN_WIDTH), row(NSA_WIDTH), stacked(d, d, layer)]
        args += [mmod8, yg, yn, w_mix_all]
    return pl.pallas_call(
        functools.partial(_ffn_body, final_norm=final_norm, with_mix=mix is not None, tf=tf),
        grid=(b, t // tm),
        in_specs=in_specs,
        out_specs=row(d),
        out_shape=jax.ShapeDtypeStruct((b, t, d), F32),
        scratch_shapes=[pltpu.VMEM((tm, ff), BF16)],
        compiler_params=_cparams(("arbitrary", "arbitrary")),
        name=("ffn_mix" if mix is not None else "ffn") + ("_final" if final_norm else ""),
    )(*args)


_O_QKV, _O_Z, _O_Q = 0, 1536, 2048
_O_KC, _O_VC, _O_KS, _O_VS, _O_KW, _O_VW, _O_SM = 2560, 2688, 2816, 2944, 3072, 3200, 3328
_IN_COLS = 3456
_SMALL_ROWS = 32


def _rope_tile(x, cosf, sint, first_half):
    fwd = pltpu.roll(x, 8, axis=1)
    bwd = pltpu.roll(x, LANES - 8, axis=1)
    return x * cosf + jnp.where(first_half, -bwd, fwd) * sint


def _inproj_body(x_ref, mod_ref, g_ref, w_ref, wst_ref, cos_ref, sin_ref,
                 qkv_ref, z_ref, ba_ref, bat_ref, qpad_ref,
                 kc_ref, vc_ref, ks_ref, vs_ref, kw_ref, vw_ref):
    y = _rms_rows(x_ref[0], g_ref[...])
    h = (y * (1.0 + mod_ref[0, 1:2, :]) + mod_ref[0, 0:1, :]).astype(BF16)
    cosf = cos_ref[0]
    sint = sin_ref[0]
    lane = lax.broadcasted_iota(jnp.int32, cosf.shape, 1)
    first_half = (lane % NSA_HEAD_DIM) < (ROT_DIM // 2)
    low = lane < NSA_HEAD_DIM

    for j in range(3):
        qkv_ref[0, :, j * 512:(j + 1) * 512] = _dot(h, w_ref[:, _O_QKV + j * 512:_O_QKV + (j + 1) * 512])
    z_ref[0] = _dot(h, w_ref[:, _O_Z:_O_Z + 512])
    ba_ref[0] = _dot(h, w_ref[:, _O_SM:_O_SM + LANES])
    bat_ref[0] = _dot_nt(wst_ref[...], h)

    scale = NSA_HEAD_DIM ** -0.5 * math.log2(math.e)
    def pair(off):
        res = _dot(h, w_ref[:, off:off + 2 * LANES])
        return res[:, :LANES], res[:, LANES:]

    q_tiles = pair(_O_Q) + pair(_O_Q + 2 * LANES)
    for k in range(NSA_HEADS // 2):
        tile = _rope_tile(q_tiles[k], cosf, sint, first_half) * scale
        swapped = pltpu.roll(tile, NSA_HEAD_DIM, axis=1)
        grp = (2 * k) // NSA_REP
        if grp == 0:
            even = jnp.where(low, tile, 0.0)
            odd = jnp.where(low, swapped, 0.0)
        else:
            even = jnp.where(low, 0.0, swapped)
            odd = jnp.where(low, 0.0, tile)
        qpad_ref[0, 2 * k] = even.astype(BF16)
        qpad_ref[0, 2 * k + 1] = odd.astype(BF16)

    kc, vc = pair(_O_KC)
    kc_ref[0] = _rope_tile(kc, cosf, sint, first_half)
    vc_ref[0] = vc
    ks, vs = pair(_O_KS)
    ks = _rope_tile(ks, cosf, sint, first_half)
    tok = pl.program_id(1) * ks.shape[0] + lax.broadcasted_iota(jnp.int32, ks.shape, 0)
    blk = lax.shift_right_logical(tok, 6) & (SEL_LANES - 1)
    for grp in range(NSA_KV_HEADS):
        onehot = jnp.where(lane - sel_lane_base(grp) == blk, 1.0, 0.0)
        mixed = jnp.where(low, ks, onehot) if grp == 0 else jnp.where(low, onehot, ks)
        ks_ref[0, grp] = mixed.astype(BF16)
    vs_ref[0] = vs.astype(BF16)
    kw, vw = pair(_O_KW)
    kw_ref[0] = _rope_tile(kw, cosf, sint, first_half).astype(BF16)
    vw_ref[0] = vw.astype(BF16)


def _inproj(x, mod8, g, w, wst, cosf, sint, tm=512):
    b, t, d = x.shape
    row = lambda n: pl.BlockSpec((1, tm, n), lambda i, j: (i, j, 0))
    sds = lambda n, dt: jax.ShapeDtypeStruct((b, t, n), dt)
    return pl.pallas_call(
        _inproj_body,
        grid=(b, t // tm),
        in_specs=[
            row(d),
            pl.BlockSpec((1, 8, d), lambda i, j: (i, 0, 0)),
            pl.BlockSpec((1, d), lambda i, j: (0, 0)),
            pl.BlockSpec((d, _IN_COLS), lambda i, j: (0, 0)),
            pl.BlockSpec((_SMALL_ROWS, d), lambda i, j: (0, 0)),
            row(LANES), row(LANES),
        ],
        out_specs=[
            row(3 * GDN_WIDTH), row(GDN_WIDTH), row(LANES),
            pl.BlockSpec((1, _SMALL_ROWS, tm), lambda i, j: (i, 0, j)),
            pl.BlockSpec((1, NSA_HEADS, tm, LANES), lambda i, j: (i, 0, j, 0)),
            row(LANES), row(LANES),
            pl.BlockSpec((1, NSA_KV_HEADS, tm, LANES), lambda i, j: (i, 0, j, 0)),
            row(LANES), row(LANES), row(LANES),
        ],
        out_shape=[
            sds(3 * GDN_WIDTH, F32), sds(GDN_WIDTH, F32), sds(LANES, F32),
            jax.ShapeDtypeStruct((b, _SMALL_ROWS, t), F32),
            jax.ShapeDtypeStruct((b, NSA_HEADS, t, LANES), BF16),
            sds(LANES, F32), sds(LANES, F32),
            jax.ShapeDtypeStruct((b, NSA_KV_HEADS, t, LANES), BF16),
            sds(LANES, BF16), sds(LANES, BF16), sds(LANES, BF16),
        ],
        compiler_params=_cparams(("arbitrary", "arbitrary")),
        name="mix_inproj",
    )(x, mod8, g, w, wst, cosf, sint)


def _softplus(x):
    return jnp.maximum(x, 0.0) + jnp.log1p(jnp.exp(-jnp.abs(x)))


def _dot_inv(a, b):
    ah = a.astype(BF16)
    al = (a - ah.astype(F32)).astype(BF16)
    bh = b.astype(BF16)
    bl = (b - bh.astype(F32)).astype(BF16)
    m = a.shape[0]
    top = _dot(jnp.concatenate([ah, al], axis=0), bh)
    return top[:m] + top[m:] + _dot(ah, bl)


def _unit_lower_inverses(lows):
    n = lows[0].shape[0]
    r = lax.broadcasted_iota(jnp.int32, (n, n), 0)
    c = lax.broadcasted_iota(jnp.int32, (n, n), 1)
    eye = jnp.where(r == c, 1.0, 0.0)
    ps = [-low for low in lows]
    invs = [eye + p for p in ps]
    for _ in range(int(math.log2(n)) - 1):
        ps = [_dot_inv(p, p) for p in ps]
        invs = [inv + _dot_inv(inv, p) for inv, p in zip(invs, ps)]
    return invs


def _gdn_prep_body(qkv_ref, ba_ref, bat_ref, cw_ref, av_ref, avt_ref,
                   u_ref, wq_ref, ak_ref, egl_ref, buf):
    step = pl.program_id(1)
    c64 = GDN_CHUNK
    hd = GDN_HEAD_DIM
    nt = GDN_PREP_STEP

    @pl.when(step == 0)
    def _():
        buf[0:8, :] = jnp.zeros((8, 3 * GDN_WIDTH), F32)

    x = qkv_ref[0]
    buf[8:8 + nt, :] = x
    y = x * cw_ref[3:4, :]
    for j in range(GDN_CONV - 1):
        sh = GDN_CONV - 1 - j
        y = y + buf[8 - sh:8 - sh + nt, :] * cw_ref[j:j + 1, :]
    buf[0:8, :] = x[nt - 8:nt, :]
    y = _silu(y)

    ba = ba_ref[0]
    beta_all = _sigmoid(ba)
    g_col_all = av_ref[0:1, :] * _softplus(ba + av_ref[1:2, :])
    bat = bat_ref[0]
    g_row_all = avt_ref[:, 0:1] * _softplus(bat + avt_ref[:, 1:2])

    r64 = lax.broadcasted_iota(jnp.int32, (c64, c64), 0)
    k64 = lax.broadcasted_iota(jnp.int32, (c64, c64), 1)
    causal = r64 >= k64
    strict = r64 > k64
    tri = jnp.where(causal, 1.0, 0.0)
    tri_t = jnp.where(r64 <= k64, 1.0, 0.0)

    nch = nt // c64
    units = [(ch, h) for ch in range(nch) for h in range(GDN_HEADS)]
    rows_of = lambda ch: slice(ch * c64, (ch + 1) * c64)
    gc_cols = [_dot_hi(tri, g_col_all[rows_of(ch), :]) for ch in range(nch)]
    gc_rows = [_dot_hi(g_row_all[:, rows_of(ch)], tri_t) for ch in range(nch)]

    def l2n(v):
        return v * lax.rsqrt(jnp.sum(v * v, axis=-1, keepdims=True) + NORM_EPS)

    khs, kbs, decays, lows = [], [], [], []
    for ch, h in units:
        kh = l2n(y[rows_of(ch), GDN_WIDTH + h * hd:GDN_WIDTH + (h + 1) * hd])
        gcol = gc_cols[ch][:, GDN_HEADS + h:GDN_HEADS + h + 1]
        grow = gc_rows[ch][GDN_HEADS + h:GDN_HEADS + h + 1, :]
        decay = jnp.exp(jnp.where(causal, gcol - grow, -jnp.inf))
        kb = kh * beta_all[rows_of(ch), h:h + 1]
        low = jnp.where(strict, _dot_nt(kb.astype(BF16), kh.astype(BF16)) * decay, 0.0)
        khs.append(kh), kbs.append(kb), decays.append(decay), lows.append(low)

    invs = _unit_lower_inverses(lows)

    egl_rows = [[] for _ in range(nch)]
    for i, (ch, h) in enumerate(units):
        rows = rows_of(ch)
        kh, kb, decay = khs[i], kbs[i], decays[i]
        qh = l2n(y[rows, h * hd:(h + 1) * hd]) * (hd ** -0.5)
        vh = y[rows, 2 * GDN_WIDTH + h * hd:2 * GDN_WIDTH + (h + 1) * hd]
        gcol = gc_cols[ch][:, GDN_HEADS + h:GDN_HEADS + h + 1]
        glast = gc_rows[ch][GDN_HEADS + h:GDN_HEADS + h + 1, c64 - 1:c64]
        eg = jnp.exp(gcol)
        rhs = jnp.concatenate([vh * beta_all[rows, h:h + 1], kb * eg], axis=1)
        sol = _dot_inv(invs[i], rhs)
        attn = _dot_nt(qh.astype(BF16), kh.astype(BF16)) * decay
        kd_t = (kh * jnp.exp(glast - gcol)).T
        u_ref[0, h, rows, :] = sol[:, :hd]
        wq_ref[0, h, ch, 0:c64, :] = sol[:, hd:].astype(BF16)
        wq_ref[0, h, ch, c64:2 * c64, :] = (qh * eg).astype(BF16)
        ak_ref[0, h, ch, 0:c64, :] = attn.astype(BF16)
        ak_ref[0, h, ch, c64:c64 + hd, :] = kd_t.astype(BF16)
        egl_rows[ch].append(jnp.broadcast_to(jnp.exp(glast), (1, LANES)))
    for ch in range(nch):
        egl_ref[0, ch] = jnp.concatenate(egl_rows[ch] + [jnp.zeros((8 - GDN_HEADS, LANES), F32)], axis=0)


def _gdn_prep(qkv, ba, bat, conv_w, avec, avect):
    b, t, _ = qkv.shape
    nt = GDN_PREP_STEP
    nch = nt // GDN_CHUNK
    hd = GDN_HEAD_DIM
    row = lambda n: pl.BlockSpec((1, nt, n), lambda i, j: (i, j, 0))
    return pl.pallas_call(
        _gdn_prep_body,
        grid=(b, t // nt),
        in_specs=[
            row(3 * GDN_WIDTH), row(LANES),
            pl.BlockSpec((1, _SMALL_ROWS, nt), lambda i, j: (i, 0, j)),
            pl.BlockSpec((GDN_CONV, 3 * GDN_WIDTH), lambda i, j: (0, 0)),
            pl.BlockSpec((8, LANES), lambda i, j: (0, 0)),
            pl.BlockSpec((_SMALL_ROWS, LANES), lambda i, j: (0, 0)),
        ],
        out_specs=[
            pl.BlockSpec((1, GDN_HEADS, nt, hd), lambda i, j: (i, 0, j, 0)),
            pl.BlockSpec((1, GDN_HEADS, nch, 2 * GDN_CHUNK, hd), lambda i, j: (i, 0, j, 0, 0)),
            pl.BlockSpec((1, GDN_HEADS, nch, GDN_CHUNK + hd, GDN_CHUNK), lambda i, j: (i, 0, j, 0, 0)),
            pl.BlockSpec((1, nch, 8, LANES), lambda i, j: (i, j, 0, 0)),
        ],
        out_shape=[
            jax.ShapeDtypeStruct((b, GDN_HEADS, t, hd), F32),
            jax.ShapeDtypeStruct((b, GDN_HEADS, t // GDN_CHUNK, 2 * GDN_CHUNK, hd), BF16),
            jax.ShapeDtypeStruct((b, GDN_HEADS, t // GDN_CHUNK, GDN_CHUNK + hd, GDN_CHUNK), BF16),
            jax.ShapeDtypeStruct((b, t // GDN_CHUNK, 8, LANES), F32),
        ],
        scratch_shapes=[pltpu.VMEM((8 + nt, 3 * GDN_WIDTH), F32)],
        compiler_params=_cparams(("arbitrary", "arbitrary")),
        name="gdn_prep",
    )(qkv, ba, bat, conv_w, avec, avect)


def _gdn_scan_body(u_ref, wq_ref, ak_ref, egl_ref, z_ref, ng_ref, o_ref, s_scr):
    c64 = GDN_CHUNK
    hd = GDN_HEAD_DIM
    nb = u_ref.shape[0]

    @pl.when(pl.program_id(0) == 0)
    def _():
        s_scr[...] = jnp.zeros_like(s_scr)

    chains = [(b, h) for b in range(nb) for h in range(GDN_HEADS)]
    states = [s_scr[b, h] for b, h in chains]
    for ch in range(GDN_SCAN_STEP // c64):
        rows = slice(ch * c64, (ch + 1) * c64)
        r1s = [_dot(wq_ref[b, h, ch], s.astype(BF16)) for (b, h), s in zip(chains, states)]
        vns = [(u_ref[b, h, rows, :] - r1[0:c64]).astype(BF16) for (b, h), r1 in zip(chains, r1s)]
        r2s = [_dot(ak_ref[b, h, ch], vn) for (b, h), vn in zip(chains, vns)]
        states = [s * egl_ref[b, ch, h:h + 1, :] + r2[c64:c64 + hd]
                  for (b, h), s, r2 in zip(chains, states, r2s)]
        for (b, h), r1, r2 in zip(chains, r1s, r2s):
            on = _rms_rows(r1[c64:2 * c64] + r2[0:c64], ng_ref[...])
            zh = z_ref[b, rows, h * hd:(h + 1) * hd]
            o_ref[b, rows, h * hd:(h + 1) * hd] = (on * _silu(zh)).astype(o_ref.dtype)
    for (b, h), s in zip(chains, states):
        s_scr[b, h] = s


def _gdn_scan(u, wq, ak, egl, z, norm_g):
    b, _, t, hd = u.shape
    nt = GDN_SCAN_STEP
    nch = nt // GDN_CHUNK
    return pl.pallas_call(
        _gdn_scan_body,
        grid=(t // nt,),
        in_specs=[
            pl.BlockSpec((b, GDN_HEADS, nt, hd), lambda j: (0, 0, j, 0)),
            pl.BlockSpec((b, GDN_HEADS, nch, 2 * GDN_CHUNK, hd), lambda j: (0, 0, j, 0, 0)),
            pl.BlockSpec((b, GDN_HEADS, nch, GDN_CHUNK + hd, GDN_CHUNK), lambda j: (0, 0, j, 0, 0)),
            pl.BlockSpec((b, nch, 8, LANES), lambda j: (0, j, 0, 0)),
            pl.BlockSpec((b, nt, GDN_WIDTH), lambda j: (0, j, 0)),
            pl.BlockSpec((1, hd), lambda j: (0, 0)),
        ],
        out_specs=pl.BlockSpec((b, nt, GDN_WIDTH), lambda j: (0, j, 0)),
        out_shape=jax.ShapeDtypeStruct((b, t, GDN_WIDTH), BF16),
        scratch_shapes=[pltpu.VMEM((b, GDN_HEADS, hd, hd), F32)],
        compiler_params=_cparams(("arbitrary",)),
        name="gdn_scan",
    )(u, wq, ak, egl, z, norm_g)


def _gdn(qkv, z, ba, bat, conv_w, avec, avect, norm_g):
    u, wq, ak, egl = _gdn_prep(qkv, ba, bat, conv_w, avec, avect)
    return _gdn_scan(u, wq, ak, egl, z, norm_g)


def _cmp_body(x_ref, pea_ref, peb_ref, wa_ref, wb_ref, w2_ref, o_ref):
    x = x_ref[0]
    a = _dot((x + pea_ref[...]).astype(BF16), wa_ref[...])
    bm = _dot((x + peb_ref[...]).astype(BF16), wb_ref[...])
    n = a.shape[0]
    h1 = a + pltpu.roll(bm, n - 1, axis=0)
    o_ref[0] = _dot(_silu(h1).astype(BF16), w2_ref[...]).astype(o_ref.dtype)


def _compress(x2, pea, peb, wa, wb, w2bd):
    b, n, wdt = x2.shape
    full = lambda s: pl.BlockSpec(s, lambda i: (0,) * len(s))
    return pl.pallas_call(
        _cmp_body,
        grid=(b,),
        in_specs=[pl.BlockSpec((1, n, wdt), lambda i: (i, 0, 0)),
                  full(pea.shape), full(peb.shape), full(wa.shape), full(wb.shape), full(w2bd.shape)],
        out_specs=pl.BlockSpec((1, n, LANES), lambda i: (i, 0, 0)),
        out_shape=jax.ShapeDtypeStruct((b, n, LANES), BF16),
        compiler_params=_cparams(("arbitrary",)),
        name="nsa_compress",
    )(x2, pea, peb, wa, wb, w2bd)


def _nsa_body(q_ref, kcmp_ref, vcmp_ref, ks_ref, vs_ref, kw_ref, vw_ref, gate_ref,
              ovl_ref, ng_ref, o_ref, qbias_scr, *, seq):
    qb = pl.program_id(1)
    s0 = qb * Q_BLOCK
    n_cmp_rows = kcmp_ref.shape[1]
    n_blk = seq // SEL_BLOCK
    rq = NSA_REP * Q_BLOCK

    t_q = s0 + lax.broadcasted_iota(jnp.int32, (Q_BLOCK, 1), 0)
    lane = lax.broadcasted_iota(jnp.int32, (Q_BLOCK, LANES), 1)
    gates = _sigmoid(gate_ref[0])

    n_iota = lax.broadcasted_iota(jnp.int32, (Q_BLOCK, n_cmp_rows), 1)
    cmask = (n_iota * CMP_STRIDE + (CMP_LEN - 1)) <= t_q

    wstart = pl.multiple_of(jnp.maximum(s0 - WINDOW, 0), Q_BLOCK)
    kp = wstart + lax.broadcasted_iota(jnp.int32, (Q_BLOCK, WINDOW + Q_BLOCK), 1)
    wmask = (kp <= t_q) & (kp > t_q - WINDOW)

    cur = lax.shift_right_logical(t_q, 6)
    forced = (lane == 0) | (lane == cur) | (lane == cur - 1)
    valid = (lane * SEL_BLOCK <= t_q) & (lane < n_blk)
    jrow8 = lax.broadcasted_iota(jnp.int32, (8, Q_BLOCK), 0)

    groups_g = range(NSA_KV_HEADS)
    qs = [q_ref[0, g * NSA_REP:(g + 1) * NSA_REP].reshape(rq, LANES) for g in groups_g]
    lmasks = [(lane >= g * NSA_HEAD_DIM) & (lane < (g + 1) * NSA_HEAD_DIM) for g in groups_g]
    kv_lane = lax.broadcasted_iota(jnp.int32, (1, LANES), 1)
    own = [(kv_lane >= g * NSA_HEAD_DIM) & (kv_lane < (g + 1) * NSA_HEAD_DIM) for g in groups_g]
    den_lane = [(1 - g) * NSA_HEAD_DIM for g in groups_g]

    def with_ones(v, g):
        return jnp.where(own[g], v, jnp.ones_like(v))

    def compressed(g, sc):
        sc = jnp.where(cmask[None], sc.reshape(NSA_REP, Q_BLOCK, n_cmp_rows), -jnp.inf)
        m = jnp.max(sc, axis=-1, keepdims=True)
        m = jnp.where(m == -jnp.inf, 0.0, m)
        p = jnp.exp2(sc - m)
        p = p * (1.0 / jnp.maximum(jnp.sum(p, axis=-1, keepdims=True), 1e-30))
        o_cmp = _dot(p.reshape(rq, n_cmp_rows).astype(BF16), vcmp_ref[0])
        psum = p[0] + p[1] + p[2] + p[3]
        hi, mid, lo = _split3(psum)
        ovl = ovl_ref[...]
        imp = _dot(hi, ovl) + _dot(mid, ovl) + _dot(lo, ovl)
        return o_cmp.reshape(NSA_REP, Q_BLOCK, LANES), imp

    def select_blocks(imp):
        key = jnp.where(forced, jnp.inf, jnp.where(valid, imp, -jnp.inf))
        key_t = key.T[:n_blk, :]
        groups = [key_t[8 * v:8 * v + 8, :] for v in range(n_blk // 8)]
        cnts = [jnp.zeros((8, Q_BLOCK), F32) for _ in groups]
        for i in range(n_blk):
            row = key_t[i:i + 1, :]
            for v, grp in enumerate(groups):
                if v < i // 8:
                    beats = jnp.where(row > grp, 1.0, 0.0)
                elif v > i // 8:
                    beats = jnp.where(row >= grp, 1.0, 0.0)
                else:
                    beats = jnp.where(jrow8 > i % 8, jnp.where(row >= grp, 1.0, 0.0),
                                      jnp.where(row > grp, 1.0, 0.0))
                cnts[v] = cnts[v] + beats
        sel_t = jnp.where(jnp.concatenate(cnts, axis=0) < float(N_SELECT), 1.0, 0.0)
        if n_blk < LANES:
            sel_t = jnp.concatenate([sel_t, jnp.zeros((LANES - n_blk, Q_BLOCK), F32)], axis=0)
        return sel_t.T

    def store_block_bias(g, sel):
        past = lane < 2 * qb
        bias = jnp.where(past, (sel - 1.0) * (-NEG_BIG), NEG_BIG)
        base = sel_lane_base(g)
        here = (lane >= base) & (lane < base + SEL_LANES)
        for c in range(seq // SEL_CHUNK):
            moved = pltpu.roll(bias, (base - c * SEL_LANES) % LANES, axis=1)
            qbias_scr[g, c] = jnp.where(here, moved, 0.0).astype(BF16)

    def sel_scores(c, g):
        k0 = pl.multiple_of(c * SEL_CHUNK, SEL_CHUNK)
        lhs = (q_ref[0, g * NSA_REP:(g + 1) * NSA_REP] + qbias_scr[g, c][None]).reshape(rq, LANES)
        return _dot_nt(lhs, ks_ref[0, g, pl.ds(k0, SEL_CHUNK), :])

    def sel_update(g, c, scores, state):
        k0 = pl.multiple_of(c * SEL_CHUNK, SEL_CHUNK)
        vch = vs_ref[0, pl.ds(k0, SEL_CHUNK), :]
        m_old, acc = state
        s = scores.reshape(NSA_REP, Q_BLOCK, SEL_CHUNK)
        m_new = jnp.maximum(m_old, jnp.max(s, axis=-1, keepdims=True))
        p = jnp.exp2(s - m_new)
        pv = _dot(p.reshape(rq, SEL_CHUNK).astype(BF16), with_ones(vch, g)).reshape(NSA_REP, Q_BLOCK, LANES)
        return m_new, jnp.exp2(m_old - m_new) * acc + pv

    kwin = kw_ref[0, pl.ds(wstart, WINDOW + Q_BLOCK), :]
    vwin = vw_ref[0, pl.ds(wstart, WINDOW + Q_BLOCK), :]
    cmp_scores = [_dot_nt(qs[g], kcmp_ref[0]) for g in groups_g]
    win_scores = [_dot_nt(qs[g], kwin) for g in groups_g]
    sdiag = pl.multiple_of(s0, Q_BLOCK)
    vdiag = vs_ref[0, pl.ds(sdiag, Q_BLOCK), :]
    diag_scores = [_dot_nt(qs[g], ks_ref[0, g, pl.ds(sdiag, Q_BLOCK), :]) for g in groups_g]

    cmp_out = [compressed(g, cmp_scores[g]) for g in groups_g]

    wbias = jnp.where(wmask, 0.0, NEG_BIG)
    sws = [win_scores[g].reshape(NSA_REP, Q_BLOCK, WINDOW + Q_BLOCK) + wbias[None] for g in groups_g]
    pws = [jnp.exp2(sw - jnp.max(sw, axis=-1, keepdims=True)) for sw in sws]
    wins = [_dot(pws[g].reshape(rq, WINDOW + Q_BLOCK).astype(BF16), with_ones(vwin, g)).reshape(NSA_REP, Q_BLOCK, LANES)
            for g in groups_g]

    for g in groups_g:
        store_block_bias(g, select_blocks(cmp_out[g][1]))

    dmask = (lax.broadcasted_iota(jnp.int32, (Q_BLOCK, Q_BLOCK), 1)
             <= lax.broadcasted_iota(jnp.int32, (Q_BLOCK, Q_BLOCK), 0))
    dbias = jnp.where(dmask, 0.0, NEG_BIG)
    states = []
    for g in groups_g:
        s = diag_scores[g].reshape(NSA_REP, Q_BLOCK, Q_BLOCK) + dbias[None]
        m0 = jnp.max(s, axis=-1, keepdims=True)
        p = jnp.exp2(s - m0).reshape(rq, Q_BLOCK).astype(BF16)
        states.append((m0, _dot(p, with_ones(vdiag, g)).reshape(NSA_REP, Q_BLOCK, LANES)))

    def sel_step(c, states):
        scores = [sel_scores(c, g) for g in groups_g]
        return tuple(sel_update(g, c, scores[g], states[g]) for g in groups_g)

    n_past = (qb + SEL_CHUNK // Q_BLOCK - 1) // (SEL_CHUNK // Q_BLOCK)
    sel_out = lax.fori_loop(0, n_past, sel_step, tuple(states))

    for g in groups_g:
        lmask = lmasks[g]
        o_cmp = cmp_out[g][0]
        acc_s = sel_out[g][1]
        o_sel = acc_s * (1.0 / acc_s[:, :, den_lane[g]:den_lane[g] + 1])
        o_win = wins[g] * (1.0 / wins[g][:, :, den_lane[g]:den_lane[g] + 1])

        ys = []
        for r in range(NSA_REP):
            col = 2 * GDN_HEADS + (g * NSA_REP + r) * 3
            o = (gates[:, col:col + 1] * o_cmp[r] + gates[:, col + 1:col + 2] * o_sel[r]
                 + gates[:, col + 2:col + 3] * o_win[r])
            o = jnp.where(lmask, o, 0.0)
            ms = jnp.sum(o * o, axis=-1, keepdims=True) * (1.0 / NSA_HEAD_DIM)
            ys.append(o * lax.rsqrt(ms + NORM_EPS) * ng_ref[...])
        for pair in range(NSA_REP // 2):
            a, bb = ys[2 * pair], ys[2 * pair + 1]
            if g == 0:
                tile = a + pltpu.roll(bb, NSA_HEAD_DIM, axis=1)
            else:
                tile = pltpu.roll(a, NSA_HEAD_DIM, axis=1) + bb
            c0 = g * NSA_REP * NSA_HEAD_DIM + pair * LANES
            o_ref[0, :, c0:c0 + LANES] = tile.astype(o_ref.dtype)


def _nsa(qpad, kcmp, vcmp, ks, vs, kw, vw, ba, ovl, ng_lane):
    b, _, t, _ = qpad.shape
    n = kcmp.shape[1]
    whole = lambda rows: pl.BlockSpec((1, rows, LANES), lambda i, j: (i, 0, 0))
    return pl.pallas_call(
        functools.partial(_nsa_body, seq=t),
        grid=(b, t // Q_BLOCK),
        in_specs=[
            pl.BlockSpec((1, NSA_HEADS, Q_BLOCK, LANES), lambda i, j: (i, 0, j, 0)),
            whole(n), whole(n),
            pl.BlockSpec((1, NSA_KV_HEADS, t, LANES), lambda i, j: (i, 0, 0, 0)),
            whole(t), whole(t), whole(t),
            pl.BlockSpec((1, Q_BLOCK, LANES), lambda i, j: (i, j, 0)),
            pl.BlockSpec((n, LANES), lambda i, j: (0, 0)),
            pl.BlockSpec((1, LANES), lambda i, j: (0, 0)),
        ],
        out_specs=pl.BlockSpec((1, Q_BLOCK, NSA_WIDTH), lambda i, j: (i, j, 0)),
        out_shape=jax.ShapeDtypeStruct((b, t, NSA_WIDTH), BF16),
        scratch_shapes=[pltpu.VMEM((NSA_KV_HEADS, t // SEL_CHUNK, Q_BLOCK, LANES), BF16)],
        compiler_params=_cparams(("arbitrary", "arbitrary")),
        name="nsa_attn",
    )(qpad, kcmp, vcmp, ks, vs, kw, vw, ba, ovl, ng_lane)


_IN_SIZES = (3 * GDN_WIDTH, GDN_WIDTH, GDN_HEADS, GDN_HEADS, NSA_WIDTH) + (NSA_KV_WIDTH,) * 6 + (3 * NSA_HEADS,)


def _prep_inproj_weight(w):
    offs = np.concatenate([[0], np.cumsum(_IN_SIZES)])
    seg = lambda i: w[:, int(offs[i]):int(offs[i + 1])]
    small = jnp.concatenate([seg(2), seg(3), seg(11)], axis=1)
    big = jnp.concatenate([seg(0), seg(1)] + [seg(i) for i in range(4, 11)]
                          + [small, jnp.zeros((w.shape[0], LANES - _SMALL_ROWS), w.dtype)], axis=1)
    return big.astype(BF16), small.T.astype(BF16)


def _prep_compress(pe, w1, w2):
    hid = CMP_HIDDEN
    g, dh = NSA_KV_HEADS, NSA_HEAD_DIM
    half = CMP_LEN // 2
    w1r = w1.reshape(CMP_LEN, dh, hid)
    eye = jnp.eye(g, dtype=w1.dtype)
    expand = lambda part: jnp.einsum("ldh,gk->lgdkh", part, eye).reshape(half * g * dh, g * hid)
    wa, wb = expand(w1r[:half]), expand(w1r[half:])
    pe_row = lambda part: jnp.broadcast_to(part[:, None, :], (half, g, dh)).reshape(1, half * g * dh)
    pea, peb = pe_row(pe[:half]), pe_row(pe[half:])
    w2bd = jnp.einsum("hd,gk->ghkd", w2, eye).reshape(g * hid, g * dh)
    return pea, peb, wa.astype(BF16), wb.astype(BF16), w2bd.astype(BF16)


def _overlap_matrix(n_rows, seq):
    n_cmp = (seq - CMP_LEN) // CMP_STRIDE + 1
    n_blk = seq // SEL_BLOCK
    n = np.arange(n_rows)[:, None]
    j = np.arange(LANES)[None, :]
    start, end = n * CMP_STRIDE, n * CMP_STRIDE + CMP_LEN - 1
    ovl = (start < j * SEL_BLOCK + SEL_BLOCK) & (end >= j * SEL_BLOCK) & (n < n_cmp) & (j < n_blk)
    return jnp.asarray(ovl.astype(np.float32), dtype=BF16)


def kernel(x, c, positions, ada_w, ada_b, norm_g, ffn_w_in, ffn_w_out, mix_w_in, gdn_conv_w, gdn_a_log, gdn_dt_bias, gdn_norm_g, cmp_pe_k, cmp_w1_k, cmp_w2_k, cmp_pe_v, cmp_w1_v, cmp_w2_v, nsa_norm_g, mix_w_out, final_norm_g):
    b, t, d = x.shape
    depth = ada_w.shape[0]
    assert t % SEL_CHUNK == 0 and t >= WINDOW + Q_BLOCK and b <= 8

    c_pad = jnp.zeros((8, d), F32).at[:b].set(c.astype(F32))
    mod = _ada_mod(c_pad, ada_w.astype(F32), ada_b.astype(F32))
    mod = mod[:, :b].reshape(depth, b, 3, 3, d)
    mod8 = jnp.concatenate([mod, jnp.zeros((depth, b, 3, 5, d), F32)], axis=3)

    half = ROT_DIM // 2
    inv_freq = jnp.power(ROPE_THETA, -jnp.arange(half, dtype=F32) * (2.0 / ROT_DIM))
    dim = np.arange(LANES) % NSA_HEAD_DIM
    invf_lane = jnp.where(jnp.asarray(dim < ROT_DIM), inv_freq[dim % half], 0.0).reshape(1, LANES)
    cosf, sint = _rope_tables(positions.astype(F32).reshape(b, t, 1), invf_lane, tm=1024)

    n_rows = t // CMP_STRIDE
    ovl = _overlap_matrix(n_rows, t)
    fg = final_norm_g.reshape(1, d).astype(F32)

    w_in_all = ffn_w_in.astype(BF16)
    w_out_all = ffn_w_out.astype(BF16)
    w_mix_all = mix_w_out.astype(BF16)

    for l in range(depth):
        ng = lambda i: norm_g[l, i].reshape(1, d).astype(F32)

        x = _ffn(x, mod8[l, :, 0], ng(0), w_in_all, w_out_all, l, 0, fg, final_norm=False)

        w_big, w_small_t = _prep_inproj_weight(mix_w_in[l])
        (qkv, z, ba, bat, qpad, kc, vc, ks, vs, kw, vw) = _inproj(
            x, mod8[l, :, 1], ng(1), w_big, w_small_t, cosf, sint)

        neg_a = -jnp.exp(gdn_a_log[l].astype(F32))
        dtb = gdn_dt_bias[l].astype(F32)
        avec = jnp.zeros((8, LANES), F32).at[0, GDN_HEADS:2 * GDN_HEADS].set(neg_a)
        avec = avec.at[1, GDN_HEADS:2 * GDN_HEADS].set(dtb)
        avect = jnp.zeros((_SMALL_ROWS, LANES), F32).at[GDN_HEADS:2 * GDN_HEADS, 0].set(neg_a)
        avect = avect.at[GDN_HEADS:2 * GDN_HEADS, 1].set(dtb)
        y_gdn = _gdn(qkv, z, ba, bat, gdn_conv_w[l].astype(F32), avec, avect,
                     gdn_norm_g[l].reshape(1, GDN_HEAD_DIM).astype(F32))

        kcmp = _compress(kc.reshape(b, n_rows, CMP_STRIDE * LANES),
                         *_prep_compress(cmp_pe_k[l], cmp_w1_k[l], cmp_w2_k[l]))
        vcmp = _compress(vc.reshape(b, n_rows, CMP_STRIDE * LANES),
                         *_prep_compress(cmp_pe_v[l], cmp_w1_v[l], cmp_w2_v[l]))
        ng_lane = jnp.tile(nsa_norm_g[l].astype(F32), NSA_KV_HEADS).reshape(1, LANES)
        y_nsa = _nsa(qpad, kcmp, vcmp, ks, vs, kw, vw, ba, ovl, ng_lane)

        x = _ffn(x, mod8[l, :, 2], ng(2), w_in_all, w_out_all, l, 1, fg, final_norm=(l == depth - 1),
                 mix=(mod8[l, :, 1], y_gdn, y_nsa, w_mix_all))
    return x
```

```python
import functools
import math

import jax
import jax.numpy as jnp
import numpy as np
from jax import lax
from jax.experimental import pallas as pl
from jax.experimental.pallas import tpu as pltpu

F32 = jnp.float32
BF16 = jnp.bfloat16

NORM_EPS = 1e-6
LANES = 128
GDN_HEADS = 4
GDN_HEAD_DIM = 128
GDN_WIDTH = GDN_HEADS * GDN_HEAD_DIM
GDN_CONV = 4
GDN_CHUNK = 64
GDN_PREP_STEP = 4 * GDN_CHUNK
GDN_SCAN_STEP = 2 * GDN_CHUNK
NSA_HEADS = 8
NSA_KV_HEADS = 2
NSA_REP = NSA_HEADS // NSA_KV_HEADS
NSA_HEAD_DIM = 64
NSA_WIDTH = NSA_HEADS * NSA_HEAD_DIM
NSA_KV_WIDTH = NSA_KV_HEADS * NSA_HEAD_DIM
CMP_LEN = 32
CMP_STRIDE = 16
CMP_HIDDEN = 2 * NSA_HEAD_DIM
SEL_BLOCK = 64
N_SELECT = 16
WINDOW = 512
Q_BLOCK = 128
SEL_CHUNK = 1024
SEL_LANES = SEL_CHUNK // SEL_BLOCK


def sel_lane_base(g):
    return NSA_HEAD_DIM if g == 0 else 0
ROPE_THETA = 500000.0
ROT_DIM = NSA_HEAD_DIM // 4
N_ADA = 9
NEG_BIG = -1e30
VMEM_LIMIT = 56 * 1024 * 1024

HIGHEST = lax.Precision.HIGHEST


def _cparams(sem):
    return pltpu.CompilerParams(dimension_semantics=sem, vmem_limit_bytes=VMEM_LIMIT)


def _dot(a, b):
    return jnp.dot(a, b, preferred_element_type=F32)


def _dot_nt(a, b):
    return lax.dot_general(a, b, (((1,), (1,)), ((), ())), preferred_element_type=F32)


def _dot_hi(a, b):
    return jnp.dot(a, b, preferred_element_type=F32, precision=HIGHEST)


def _split3(x):
    hi = x.astype(BF16)
    r1 = x - hi.astype(F32)
    mid = r1.astype(BF16)
    lo = (r1 - mid.astype(F32)).astype(BF16)
    return hi, mid, lo


def _sigmoid(x):
    return 1.0 / (1.0 + jnp.exp(-x))


def _silu(x):
    return x * _sigmoid(x)


def _rms_rows(x, g):
    ms = jnp.mean(x * x, axis=-1, keepdims=True)
    return x * lax.rsqrt(ms + NORM_EPS) * g


def _ada_body(c_ref, w_ref, b_ref, o_ref):
    cond = _silu(c_ref[...])
    o_ref[0] = _dot_hi(cond, w_ref[0]) + b_ref[0]


def _ada_mod(c_pad, ada_w, ada_b):
    depth, d, n = ada_w.shape
    tn = n // N_ADA
    rows = c_pad.shape[0]
    return pl.pallas_call(
        _ada_body,
        grid=(depth, n // tn),
        in_specs=[
            pl.BlockSpec((rows, d), lambda l, j: (0, 0)),
            pl.BlockSpec((1, d, tn), lambda l, j: (l, 0, j)),
            pl.BlockSpec((1, 1, tn), lambda l, j: (l, 0, j)),
        ],
        out_specs=pl.BlockSpec((1, rows, tn), lambda l, j: (l, 0, j)),
        out_shape=jax.ShapeDtypeStruct((depth, rows, n), F32),
        compiler_params=_cparams(("arbitrary", "arbitrary")),
        name="ada_mod",
    )(c_pad, ada_w, ada_b.reshape(depth, 1, n))


def _rope_body(pos_ref, invf_ref, cos_ref, sin_ref):
    ang = pos_ref[0] * invf_ref[...]
    cos_ref[0] = jnp.cos(ang)
    sin_ref[0] = jnp.sin(ang)


def _rope_tables(pos_f, invf_lane, tm):
    b, t, _ = pos_f.shape
    spec = pl.BlockSpec((1, tm, LANES), lambda i, j: (i, j, 0))
    return pl.pallas_call(
        _rope_body,
        grid=(b, t // tm),
        in_specs=[pl.BlockSpec((1, tm, 1), lambda i, j: (i, j, 0)),
                  pl.BlockSpec((1, LANES), lambda i, j: (0, 0))],
        out_specs=[spec, spec],
        out_shape=[jax.ShapeDtypeStruct((b, t, LANES), F32)] * 2,
        compiler_params=_cparams(("arbitrary", "arbitrary")),
        name="rope_tables",
    )(pos_f, invf_lane)


def _ffn_body(*refs, final_norm, with_mix, tf):
    if with_mix:
        (x_ref, mod_ref, g_ref, wi_ref, wo_ref, fg_ref,
         mmod_ref, yg_ref, yn_ref, wm_ref, o_ref, act_scr) = refs
        mix = _dot(yg_ref[0], wm_ref[0:GDN_WIDTH, :]) + _dot(yn_ref[0], wm_ref[GDN_WIDTH:, :])
        x = x_ref[0] + mmod_ref[0, 2:3, :] * mix
    else:
        x_ref, mod_ref, g_ref, wi_ref, wo_ref, fg_ref, o_ref, act_scr = refs
        x = x_ref[0]
    y = _rms_rows(x, g_ref[...])
    h = (y * (1.0 + mod_ref[0, 1:2, :]) + mod_ref[0, 0:1, :]).astype(BF16)
    ff = wo_ref.shape[0]
    for j in range(ff // tf):
        a = _dot(h, wi_ref[:, j * tf:(j + 1) * tf])
        u = _dot(h, wi_ref[:, ff + j * tf:ff + (j + 1) * tf])
        act_scr[:, j * tf:(j + 1) * tf] = (_silu(a) * u).astype(BF16)
    out = x + (0.5 * mod_ref[0, 2:3, :]) * _dot(act_scr[...], wo_ref[...])
    if final_norm:
        out = _rms_rows(out, fg_ref[...])
    o_ref[0] = out


def _ffn(x, mod8, g, w_in_all, w_out_all, layer, idx, fg, *, final_norm, mix=None, tm=1024, tf=256):
    b, t, d = x.shape
    ff = w_out_all.shape[2]
    row = lambda n: pl.BlockSpec((1, tm, n), lambda i, j: (i, j, 0))
    mod_spec = pl.BlockSpec((1, 8, d), lambda i, j: (i, 0, 0))
    vec_spec = pl.BlockSpec((1, d), lambda i, j: (0, 0))
    stacked = lambda r, c, *lead: pl.BlockSpec((None,) * len(lead) + (r, c), lambda i, j: lead + (0, 0),
                                               pipeline_mode=pl.Buffered(1))
    in_specs = [row(d), mod_spec, vec_spec, stacked(d, 2 * ff, layer, idx), stacked(ff, d, layer, idx), vec_spec]
    args = [x, mod8, g, w_in_all, w_out_all, fg]
    if mix is not None:
        mmod8, yg, yn, w_mix_all = mix
        in_specs += [mod_spec, row(GDN_WIDTH), row(NSA_WIDTH), stacked(d, d, layer)]
        args += [mmod8, yg, yn, w_mix_all]
    return pl.pallas_call(
        functools.partial(_ffn_body, final_norm=final_norm, with_mix=mix is not None, tf=tf),
        grid=(b, t // tm),
        in_specs=in_specs,
        out_specs=row(d),
        out_shape=jax.ShapeDtypeStruct((b, t, d), F32),
        scratch_shapes=[pltpu.VMEM((tm, ff), BF16)],
        compiler_params=_cparams(("arbitrary", "arbitrary")),
        name=("ffn_mix" if mix is not None else "ffn") + ("_final" if final_norm else ""),
    )(*args)


_O_QKV, _O_Z, _O_Q = 0, 1536, 2048
_O_KC, _O_VC, _O_KS, _O_VS, _O_KW, _O_VW, _O_SM = 2560, 2688, 2816, 2944, 3072, 3200, 3328
_IN_COLS = 3456
_SMALL_ROWS = 32


def _rope_tile(x, cosf, sint, first_half):
    fwd = pltpu.roll(x, 8, axis=1)
    bwd = pltpu.roll(x, LANES - 8, axis=1)
    return x * cosf + jnp.where(first_half, -bwd, fwd) * sint


def _inproj_body(x_ref, mod_ref, g_ref, w_ref, cos_ref, sin_ref,
                 qkv_ref, z_ref, ba_ref, bat_ref, qpad_ref,
                 kc_ref, vc_ref, ks_ref, vs_ref, kw_ref, vw_ref):
    y = _rms_rows(x_ref[0], g_ref[...])
    h = (y * (1.0 + mod_ref[0, 1:2, :]) + mod_ref[0, 0:1, :]).astype(BF16)
    cosf = cos_ref[0]
    sint = sin_ref[0]
    lane = lax.broadcasted_iota(jnp.int32, cosf.shape, 1)
    first_half = (lane % NSA_HEAD_DIM) < (ROT_DIM // 2)
    low = lane < NSA_HEAD_DIM

    for j in range(3):
        qkv_ref[0, :, j * 512:(j + 1) * 512] = _dot(h, w_ref[:, _O_QKV + j * 512:_O_QKV + (j + 1) * 512])
    z_ref[0] = _dot(h, w_ref[:, _O_Z:_O_Z + 512])
    ba = _dot(h, w_ref[:, _O_SM:_O_SM + LANES])
    ba_ref[0] = ba
    bat_ref[0] = ba.T[:_SMALL_ROWS, :]

    scale = NSA_HEAD_DIM ** -0.5 * math.log2(math.e)
    def pair(off):
        res = _dot(h, w_ref[:, off:off + 2 * LANES])
        return res[:, :LANES], res[:, LANES:]

    q_tiles = pair(_O_Q) + pair(_O_Q + 2 * LANES)
    for k in range(NSA_HEADS // 2):
        tile = _rope_tile(q_tiles[k], cosf, sint, first_half) * scale
        swapped = pltpu.roll(tile, NSA_HEAD_DIM, axis=1)
        grp = (2 * k) // NSA_REP
        if grp == 0:
            even = jnp.where(low, tile, 0.0)
            odd = jnp.where(low, swapped, 0.0)
        else:
            even = jnp.where(low, 0.0, swapped)
            odd = jnp.where(low, 0.0, tile)
        qpad_ref[0, 2 * k] = even.astype(BF16)
        qpad_ref[0, 2 * k + 1] = odd.astype(BF16)

    kc, vc = pair(_O_KC)
    kc_ref[0] = _rope_tile(kc, cosf, sint, first_half)
    vc_ref[0] = vc
    ks, vs = pair(_O_KS)
    ks = _rope_tile(ks, cosf, sint, first_half)
    tok = pl.program_id(1) * ks.shape[0] + lax.broadcasted_iota(jnp.int32, ks.shape, 0)
    blk = lax.shift_right_logical(tok, 6) & (SEL_LANES - 1)
    for grp in range(NSA_KV_HEADS):
        onehot = jnp.where(lane - sel_lane_base(grp) == blk, 1.0, 0.0)
        mixed = jnp.where(low, ks, onehot) if grp == 0 else jnp.where(low, onehot, ks)
        ks_ref[0, grp] = mixed.astype(BF16)
    vs_ref[0] = vs.astype(BF16)
    kw, vw = pair(_O_KW)
    kw_ref[0] = _rope_tile(kw, cosf, sint, first_half).astype(BF16)
    vw_ref[0] = vw.astype(BF16)


def _inproj(x, mod8, g, w, cosf, sint, tm=512):
    b, t, d = x.shape
    row = lambda n: pl.BlockSpec((1, tm, n), lambda i, j: (i, j, 0))
    sds = lambda n, dt: jax.ShapeDtypeStruct((b, t, n), dt)
    return pl.pallas_call(
        _inproj_body,
        grid=(b, t // tm),
        in_specs=[
            row(d),
            pl.BlockSpec((1, 8, d), lambda i, j: (i, 0, 0)),
            pl.BlockSpec((1, d), lambda i, j: (0, 0)),
            pl.BlockSpec((d, _IN_COLS), lambda i, j: (0, 0)),
            row(LANES), row(LANES),
        ],
        out_specs=[
            row(3 * GDN_WIDTH), row(GDN_WIDTH), row(LANES),
            pl.BlockSpec((1, _SMALL_ROWS, tm), lambda i, j: (i, 0, j)),
            pl.BlockSpec((1, NSA_HEADS, tm, LANES), lambda i, j: (i, 0, j, 0)),
            row(LANES), row(LANES),
            pl.BlockSpec((1, NSA_KV_HEADS, tm, LANES), lambda i, j: (i, 0, j, 0)),
            row(LANES), row(LANES), row(LANES),
        ],
        out_shape=[
            sds(3 * GDN_WIDTH, F32), sds(GDN_WIDTH, F32), sds(LANES, F32),
            jax.ShapeDtypeStruct((b, _SMALL_ROWS, t), F32),
            jax.ShapeDtypeStruct((b, NSA_HEADS, t, LANES), BF16),
            sds(LANES, F32), sds(LANES, F32),
            jax.ShapeDtypeStruct((b, NSA_KV_HEADS, t, LANES), BF16),
            sds(LANES, BF16), sds(LANES, BF16), sds(LANES, BF16),
        ],
        compiler_params=_cparams(("arbitrary", "arbitrary")),
        name="mix_inproj",
    )(x, mod8, g, w, cosf, sint)


def _softplus(x):
    return jnp.maximum(x, 0.0) + jnp.log1p(jnp.exp(-jnp.abs(x)))


def _dot_inv(a, b):
    ah = a.astype(BF16)
    al = (a - ah.astype(F32)).astype(BF16)
    bh = b.astype(BF16)
    bl = (b - bh.astype(F32)).astype(BF16)
    m = a.shape[0]
    top = _dot(jnp.concatenate([ah, al], axis=0), bh)
    return top[:m] + top[m:] + _dot(ah, bl)


def _unit_lower_inverses(lows, order):
    n = lows[0].shape[0]
    r = lax.broadcasted_iota(jnp.int32, (n, n), 0)
    c = lax.broadcasted_iota(jnp.int32, (n, n), 1)
    eye = jnp.where(r == c, 1.0, 0.0)
    ps = [-low for low in lows]
    invs = [eye + p for p in ps]
    for _ in range(int(math.log2(order)) - 1):
        ps = [_dot_inv(p, p) for p in ps]
        invs = [inv + _dot_inv(inv, p) for inv, p in zip(invs, ps)]
    return invs


def _gdn_prep_body(qkv_ref, ba_ref, bat_ref, cw_ref, av_ref, avt_ref,
                   u_ref, wq_ref, ak_ref, egl_ref, buf):
    step = pl.program_id(1)
    c64 = GDN_CHUNK
    hd = GDN_HEAD_DIM
    nt = GDN_PREP_STEP

    @pl.when(step == 0)
    def _():
        buf[0:8, :] = jnp.zeros((8, 3 * GDN_WIDTH), F32)

    x = qkv_ref[0]
    buf[8:8 + nt, :] = x
    y = x * cw_ref[3:4, :]
    for j in range(GDN_CONV - 1):
        sh = GDN_CONV - 1 - j
        y = y + buf[8 - sh:8 - sh + nt, :] * cw_ref[j:j + 1, :]
    buf[0:8, :] = x[nt - 8:nt, :]
    y = _silu(y)

    ba = ba_ref[0]
    beta_all = _sigmoid(ba)
    g_col_all = av_ref[0:1, :] * _softplus(ba + av_ref[1:2, :])
    bat = bat_ref[0]
    g_row_all = avt_ref[:, 0:1] * _softplus(bat + avt_ref[:, 1:2])

    pw = 2 * c64
    rr = lax.broadcasted_iota(jnp.int32, (pw, pw), 0)
    kk = lax.broadcasted_iota(jnp.int32, (pw, pw), 1)
    same = lax.shift_right_logical(rr, 6) == lax.shift_right_logical(kk, 6)
    causal = same & (rr >= kk)
    strict = same & (rr > kk)
    tri = jnp.where(causal, 1.0, 0.0)
    tri_t = jnp.where(same & (rr <= kk), 1.0, 0.0)
    first = lax.broadcasted_iota(jnp.int32, (pw, 1), 0) < c64

    npair = nt // pw
    units = [(pc, h) for pc in range(npair) for h in range(GDN_HEADS)]
    rows_of = lambda pc: slice(pc * pw, (pc + 1) * pw)
    gc_cols = [_dot_hi(tri, g_col_all[rows_of(pc), :]) for pc in range(npair)]
    gc_rows = [_dot_hi(g_row_all[:, rows_of(pc)], tri_t) for pc in range(npair)]

    def l2n(v):
        return v * lax.rsqrt(jnp.sum(v * v, axis=-1, keepdims=True) + NORM_EPS)

    khs, kbs, decays, lows = [], [], [], []
    for pc, h in units:
        kh = l2n(y[rows_of(pc), GDN_WIDTH + h * hd:GDN_WIDTH + (h + 1) * hd])
        gcol = gc_cols[pc][:, GDN_HEADS + h:GDN_HEADS + h + 1]
        grow = gc_rows[pc][GDN_HEADS + h:GDN_HEADS + h + 1, :]
        decay = jnp.exp(jnp.where(causal, gcol - grow, -jnp.inf))
        kb = kh * beta_all[rows_of(pc), h:h + 1]
        low = jnp.where(strict, _dot_nt(kb.astype(BF16), kh.astype(BF16)) * decay, 0.0)
        khs.append(kh), kbs.append(kb), decays.append(decay), lows.append(low)

    invs = _unit_lower_inverses(lows, c64)

    egl_rows = [[] for _ in range(nt // c64)]
    for i, (pc, h) in enumerate(units):
        rows = rows_of(pc)
        kh, kb, decay = khs[i], kbs[i], decays[i]
        qh = l2n(y[rows, h * hd:(h + 1) * hd]) * (hd ** -0.5)
        vh = y[rows, 2 * GDN_WIDTH + h * hd:2 * GDN_WIDTH + (h + 1) * hd]
        gcol = gc_cols[pc][:, GDN_HEADS + h:GDN_HEADS + h + 1]
        grow = gc_rows[pc][GDN_HEADS + h:GDN_HEADS + h + 1, :]
        glasts = [grow[:, c64 - 1:c64], grow[:, pw - 1:pw]]
        glast = jnp.where(first, glasts[0], glasts[1])
        eg = jnp.exp(gcol)
        rhs = jnp.concatenate([vh * beta_all[rows, h:h + 1], kb * eg], axis=1)
        sol = _dot_inv(invs[i], rhs)
        attn = _dot_nt(qh.astype(BF16), kh.astype(BF16)) * decay
        kd_t = (kh * jnp.exp(glast - gcol)).T
        qe = (qh * eg).astype(BF16)
        u_ref[0, h, rows, :] = sol[:, :hd]
        for half in range(2):
            ch = 2 * pc + half
            part = slice(half * c64, (half + 1) * c64)
            wq_ref[0, h, ch, 0:c64, :] = sol[part, hd:].astype(BF16)
            wq_ref[0, h, ch, c64:2 * c64, :] = qe[part, :]
            ak_ref[0, h, ch, 0:c64, :] = attn[part, part].astype(BF16)
            ak_ref[0, h, ch, c64:c64 + hd, :] = kd_t[:, part].astype(BF16)
            egl_rows[ch].append(jnp.broadcast_to(jnp.exp(glasts[half]), (1, LANES)))
    for ch in range(nt // c64):
        egl_ref[0, ch] = jnp.concatenate(egl_rows[ch] + [jnp.zeros((8 - GDN_HEADS, LANES), F32)], axis=0)


def _gdn_prep(qkv, ba, bat, conv_w, avec, avect):
    b, t, _ = qkv.shape
    nt = GDN_PREP_STEP
    nch = nt // GDN_CHUNK
    hd = GDN_HEAD_DIM
    row = lambda n: pl.BlockSpec((1, nt, n), lambda i, j: (i, j, 0))
    return pl.pallas_call(
        _gdn_prep_body,
        grid=(b, t // nt),
        in_specs=[
            row(3 * GDN_WIDTH), row(LANES),
            pl.BlockSpec((1, _SMALL_ROWS, nt), lambda i, j: (i, 0, j)),
            pl.BlockSpec((GDN_CONV, 3 * GDN_WIDTH), lambda i, j: (0, 0)),
            pl.BlockSpec((8, LANES), lambda i, j: (0, 0)),
            pl.BlockSpec((_SMALL_ROWS, LANES), lambda i, j: (0, 0)),
        ],
        out_specs=[
            pl.BlockSpec((1, GDN_HEADS, nt, hd), lambda i, j: (i, 0, j, 0)),
            pl.BlockSpec((1, GDN_HEADS, nch, 2 * GDN_CHUNK, hd), lambda i, j: (i, 0, j, 0, 0)),
            pl.BlockSpec((1, GDN_HEADS, nch, GDN_CHUNK + hd, GDN_CHUNK), lambda i, j: (i, 0, j, 0, 0)),
            pl.BlockSpec((1, nch, 8, LANES), lambda i, j: (i, j, 0, 0)),
        ],
        out_shape=[
            jax.ShapeDtypeStruct((b, GDN_HEADS, t, hd), F32),
            jax.ShapeDtypeStruct((b, GDN_HEADS, t // GDN_CHUNK, 2 * GDN_CHUNK, hd), BF16),
            jax.ShapeDtypeStruct((b, GDN_HEADS, t // GDN_CHUNK, GDN_CHUNK + hd, GDN_CHUNK), BF16),
            jax.ShapeDtypeStruct((b, t // GDN_CHUNK, 8, LANES), F32),
        ],
        scratch_shapes=[pltpu.VMEM((8 + nt, 3 * GDN_WIDTH), F32)],
        compiler_params=_cparams(("arbitrary", "arbitrary")),
        name="gdn_prep",
    )(qkv, ba, bat, conv_w, avec, avect)


def _gdn_scan_body(u_ref, wq_ref, ak_ref, egl_ref, z_ref, ng_ref, o_ref, s_scr):
    c64 = GDN_CHUNK
    hd = GDN_HEAD_DIM
    nb = u_ref.shape[0]

    @pl.when(pl.program_id(0) == 0)
    def _():
        s_scr[...] = jnp.zeros_like(s_scr)

    chains = [(b, h) for b in range(nb) for h in range(GDN_HEADS)]
    states = [s_scr[b, h] for b, h in chains]
    for ch in range(GDN_SCAN_STEP // c64):
        rows = slice(ch * c64, (ch + 1) * c64)
        r1s = [_dot(wq_ref[b, h, ch], s.astype(BF16)) for (b, h), s in zip(chains, states)]
        vns = [(u_ref[b, h, rows, :] - r1[0:c64]).astype(BF16) for (b, h), r1 in zip(chains, r1s)]
        r2s = [_dot(ak_ref[b, h, ch], vn) for (b, h), vn in zip(chains, vns)]
        states = [s * egl_ref[b, ch, h:h + 1, :] + r2[c64:c64 + hd]
                  for (b, h), s, r2 in zip(chains, states, r2s)]
        for (b, h), r1, r2 in zip(chains, r1s, r2s):
            on = _rms_rows(r1[c64:2 * c64] + r2[0:c64], ng_ref[...])
            zh = z_ref[b, rows, h * hd:(h + 1) * hd]
            o_ref[b, rows, h * hd:(h + 1) * hd] = (on * _silu(zh)).astype(o_ref.dtype)
    for (b, h), s in zip(chains, states):
        s_scr[b, h] = s


def _gdn_scan(u, wq, ak, egl, z, norm_g):
    b, _, t, hd = u.shape
    nt = GDN_SCAN_STEP
    nch = nt // GDN_CHUNK
    return pl.pallas_call(
        _gdn_scan_body,
        grid=(t // nt,),
        in_specs=[
            pl.BlockSpec((b, GDN_HEADS, nt, hd), lambda j: (0, 0, j, 0)),
            pl.BlockSpec((b, GDN_HEADS, nch, 2 * GDN_CHUNK, hd), lambda j: (0, 0, j, 0, 0)),
            pl.BlockSpec((b, GDN_HEADS, nch, GDN_CHUNK + hd, GDN_CHUNK), lambda j: (0, 0, j, 0, 0)),
            pl.BlockSpec((b, nch, 8, LANES), lambda j: (0, j, 0, 0)),
            pl.BlockSpec((b, nt, GDN_WIDTH), lambda j: (0, j, 0)),
            pl.BlockSpec((1, hd), lambda j: (0, 0)),
        ],
        out_specs=pl.BlockSpec((b, nt, GDN_WIDTH), lambda j: (0, j, 0)),
        out_shape=jax.ShapeDtypeStruct((b, t, GDN_WIDTH), BF16),
        scratch_shapes=[pltpu.VMEM((b, GDN_HEADS, hd, hd), F32)],
        compiler_params=_cparams(("arbitrary",)),
        name="gdn_scan",
    )(u, wq, ak, egl, z, norm_g)


def _gdn(qkv, z, ba, bat, conv_w, avec, avect, norm_g):
    u, wq, ak, egl = _gdn_prep(qkv, ba, bat, conv_w, avec, avect)
    return _gdn_scan(u, wq, ak, egl, z, norm_g)


def _cmp_body(x_ref, pea_ref, peb_ref, wa_ref, wb_ref, w2_ref, o_ref):
    n_rows = x_ref.shape[1] // CMP_STRIDE
    x = jnp.concatenate([x_ref[0, pl.ds(l, n_rows, stride=CMP_STRIDE), :] for l in range(CMP_STRIDE)], axis=1)
    a = _dot((x + pea_ref[...]).astype(BF16), wa_ref[...])
    bm = _dot((x + peb_ref[...]).astype(BF16), wb_ref[...])
    n = a.shape[0]
    h1 = a + pltpu.roll(bm, n - 1, axis=0)
    o_ref[0] = _dot(_silu(h1).astype(BF16), w2_ref[...]).astype(o_ref.dtype)


def _compress(x, pea, peb, wa, wb, w2bd):
    b, t, wdt = x.shape
    n = t // CMP_STRIDE
    full = lambda s: pl.BlockSpec(s, lambda i: (0,) * len(s))
    return pl.pallas_call(
        _cmp_body,
        grid=(b,),
        in_specs=[pl.BlockSpec((1, t, wdt), lambda i: (i, 0, 0)),
                  full(pea.shape), full(peb.shape), full(wa.shape), full(wb.shape), full(w2bd.shape)],
        out_specs=pl.BlockSpec((1, n, LANES), lambda i: (i, 0, 0)),
        out_shape=jax.ShapeDtypeStruct((b, n, LANES), BF16),
        compiler_params=_cparams(("arbitrary",)),
        name="nsa_compress",
    )(x, pea, peb, wa, wb, w2bd)


def _nsa_body(q_ref, kcmp_ref, vcmp_ref, ks_ref, vs_ref, kw_ref, vw_ref, gate_ref,
              ovl_ref, ng_ref, o_ref, qbias_scr, *, seq):
    qb = pl.program_id(1)
    s0 = qb * Q_BLOCK
    n_cmp_rows = kcmp_ref.shape[1]
    n_blk = seq // SEL_BLOCK
    rq = NSA_REP * Q_BLOCK

    t_q = s0 + lax.broadcasted_iota(jnp.int32, (Q_BLOCK, 1), 0)
    lane = lax.broadcasted_iota(jnp.int32, (Q_BLOCK, LANES), 1)
    gates = _sigmoid(gate_ref[0])

    n_iota = lax.broadcasted_iota(jnp.int32, (Q_BLOCK, n_cmp_rows), 1)
    cmask = (n_iota * CMP_STRIDE + (CMP_LEN - 1)) <= t_q

    wstart = pl.multiple_of(jnp.maximum(s0 - WINDOW, 0), Q_BLOCK)
    kp = wstart + lax.broadcasted_iota(jnp.int32, (Q_BLOCK, WINDOW + Q_BLOCK), 1)
    wmask = (kp <= t_q) & (kp > t_q - WINDOW)

    cur = lax.shift_right_logical(t_q, 6)
    forced = (lane == 0) | (lane == cur) | (lane == cur - 1)
    valid = (lane * SEL_BLOCK <= t_q) & (lane < n_blk)
    jrow8 = lax.broadcasted_iota(jnp.int32, (8, Q_BLOCK), 0)

    groups_g = range(NSA_KV_HEADS)
    qs = [q_ref[0, g * NSA_REP:(g + 1) * NSA_REP].reshape(rq, LANES) for g in groups_g]
    lmasks = [(lane >= g * NSA_HEAD_DIM) & (lane < (g + 1) * NSA_HEAD_DIM) for g in groups_g]
    kv_lane = lax.broadcasted_iota(jnp.int32, (1, LANES), 1)
    own = [(kv_lane >= g * NSA_HEAD_DIM) & (kv_lane < (g + 1) * NSA_HEAD_DIM) for g in groups_g]
    den_lane = [(1 - g) * NSA_HEAD_DIM for g in groups_g]

    def with_ones(v, g):
        return jnp.where(own[g], v, jnp.ones_like(v))

    def compressed(g, sc):
        sc = jnp.where(cmask[None], sc.reshape(NSA_REP, Q_BLOCK, n_cmp_rows), -jnp.inf)
        m = jnp.max(sc, axis=-1, keepdims=True)
        m = jnp.where(m == -jnp.inf, 0.0, m)
        p = jnp.exp2(sc - m)
        p = p * (1.0 / jnp.maximum(jnp.sum(p, axis=-1, keepdims=True), 1e-30))
        o_cmp = _dot(p.reshape(rq, n_cmp_rows).astype(BF16), vcmp_ref[0])
        psum = p[0] + p[1] + p[2] + p[3]
        hi, mid, lo = _split3(psum)
        ovl = ovl_ref[...]
        imp = _dot(hi, ovl) + _dot(mid, ovl) + _dot(lo, ovl)
        return o_cmp.reshape(NSA_REP, Q_BLOCK, LANES), imp

    def select_blocks(imp):
        key = jnp.where(forced, jnp.inf, jnp.where(valid, imp, -jnp.inf))
        key_t = key.T[:n_blk, :]
        groups = [key_t[8 * v:8 * v + 8, :] for v in range(n_blk // 8)]
        cnts = [jnp.zeros((8, Q_BLOCK), F32) for _ in groups]
        for i in range(n_blk):
            row = key_t[i:i + 1, :]
            for v, grp in enumerate(groups):
                if v < i // 8:
                    beats = jnp.where(row > grp, 1.0, 0.0)
                elif v > i // 8:
                    beats = jnp.where(row >= grp, 1.0, 0.0)
                else:
                    beats = jnp.where(jrow8 > i % 8, jnp.where(row >= grp, 1.0, 0.0),
                                      jnp.where(row > grp, 1.0, 0.0))
                cnts[v] = cnts[v] + beats
        sel_t = jnp.where(jnp.concatenate(cnts, axis=0) < float(N_SELECT), 1.0, 0.0)
        if n_blk < LANES:
            sel_t = jnp.concatenate([sel_t, jnp.zeros((LANES - n_blk, Q_BLOCK), F32)], axis=0)
        return sel_t.T

    def store_block_bias(g, sel):
        past = lane < 2 * qb
        bias = jnp.where(past, (sel - 1.0) * (-NEG_BIG), NEG_BIG)
        base = sel_lane_base(g)
        here = (lane >= base) & (lane < base + SEL_LANES)
        for c in range(seq // SEL_CHUNK):
            moved = pltpu.roll(bias, (base - c * SEL_LANES) % LANES, axis=1)
            qbias_scr[g, c] = jnp.where(here, moved, 0.0).astype(BF16)

    def sel_scores(c, g):
        k0 = pl.multiple_of(c * SEL_CHUNK, SEL_CHUNK)
        lhs = (q_ref[0, g * NSA_REP:(g + 1) * NSA_REP] + qbias_scr[g, c][None]).reshape(rq, LANES)
        return _dot_nt(lhs, ks_ref[0, g, pl.ds(k0, SEL_CHUNK), :])

    def sel_update(g, c, scores, state):
        k0 = pl.multiple_of(c * SEL_CHUNK, SEL_CHUNK)
        vch = vs_ref[0, pl.ds(k0, SEL_CHUNK), :]
        m_old, acc = state
        s = scores.reshape(NSA_REP, Q_BLOCK, SEL_CHUNK)
        m_new = jnp.maximum(m_old, jnp.max(s, axis=-1, keepdims=True))
        p = jnp.exp2(s - m_new)
        pv = _dot(p.reshape(rq, SEL_CHUNK).astype(BF16), with_ones(vch, g)).reshape(NSA_REP, Q_BLOCK, LANES)
        return m_new, jnp.exp2(m_old - m_new) * acc + pv

    kwin = kw_ref[0, pl.ds(wstart, WINDOW + Q_BLOCK), :]
    vwin = vw_ref[0, pl.ds(wstart, WINDOW + Q_BLOCK), :]
    cmp_scores = [_dot_nt(qs[g], kcmp_ref[0]) for g in groups_g]
    win_scores = [_dot_nt(qs[g], kwin) for g in groups_g]
    sdiag = pl.multiple_of(s0, Q_BLOCK)
    vdiag = vs_ref[0, pl.ds(sdiag, Q_BLOCK), :]
    diag_scores = [_dot_nt(qs[g], ks_ref[0, g, pl.ds(sdiag, Q_BLOCK), :]) for g in groups_g]

    cmp_out = [compressed(g, cmp_scores[g]) for g in groups_g]

    wbias = jnp.where(wmask, 0.0, NEG_BIG)
    sws = [win_scores[g].reshape(NSA_REP, Q_BLOCK, WINDOW + Q_BLOCK) + wbias[None] for g in groups_g]
    pws = [jnp.exp2(sw - jnp.max(sw, axis=-1, keepdims=True)) for sw in sws]
    wins = [_dot(pws[g].reshape(rq, WINDOW + Q_BLOCK).astype(BF16), with_ones(vwin, g)).reshape(NSA_REP, Q_BLOCK, LANES)
            for g in groups_g]

    for g in groups_g:
        store_block_bias(g, select_blocks(cmp_out[g][1]))

    dmask = (lax.broadcasted_iota(jnp.int32, (Q_BLOCK, Q_BLOCK), 1)
             <= lax.broadcasted_iota(jnp.int32, (Q_BLOCK, Q_BLOCK), 0))
    dbias = jnp.where(dmask, 0.0, NEG_BIG)
    states = []
    for g in groups_g:
        s = diag_scores[g].reshape(NSA_REP, Q_BLOCK, Q_BLOCK) + dbias[None]
        m0 = jnp.max(s, axis=-1, keepdims=True)
        p = jnp.exp2(s - m0).reshape(rq, Q_BLOCK).astype(BF16)
        states.append((m0, _dot(p, with_ones(vdiag, g)).reshape(NSA_REP, Q_BLOCK, LANES)))

    def sel_step(c, states):
        scores = [sel_scores(c, g) for g in groups_g]
        return tuple(sel_update(g, c, scores[g], states[g]) for g in groups_g)

    n_past = (qb + SEL_CHUNK // Q_BLOCK - 1) // (SEL_CHUNK // Q_BLOCK)
    sel_out = lax.fori_loop(0, n_past, sel_step, tuple(states))

    for g in groups_g:
        lmask = lmasks[g]
        o_cmp = cmp_out[g][0]
        acc_s = sel_out[g][1]
        o_sel = acc_s * (1.0 / acc_s[:, :, den_lane[g]:den_lane[g] + 1])
        o_win = wins[g] * (1.0 / wins[g][:, :, den_lane[g]:den_lane[g] + 1])

        ys = []
        for r in range(NSA_REP):
            col = 2 * GDN_HEADS + (g * NSA_REP + r) * 3
            o = (gates[:, col:col + 1] * o_cmp[r] + gates[:, col + 1:col + 2] * o_sel[r]
                 + gates[:, col + 2:col + 3] * o_win[r])
            o = jnp.where(lmask, o, 0.0)
            ms = jnp.sum(o * o, axis=-1, keepdims=True) * (1.0 / NSA_HEAD_DIM)
            ys.append(o * lax.rsqrt(ms + NORM_EPS) * ng_ref[...])
        for pair in range(NSA_REP // 2):
            a, bb = ys[2 * pair], ys[2 * pair + 1]
            if g == 0:
                tile = a + pltpu.roll(bb, NSA_HEAD_DIM, axis=1)
            else:
                tile = pltpu.roll(a, NSA_HEAD_DIM, axis=1) + bb
            c0 = g * NSA_REP * NSA_HEAD_DIM + pair * LANES
            o_ref[0, :, c0:c0 + LANES] = tile.astype(o_ref.dtype)


def _nsa(qpad, kcmp, vcmp, ks, vs, kw, vw, ba, ovl, ng_lane):
    b, _, t, _ = qpad.shape
    n = kcmp.shape[1]
    whole = lambda rows: pl.BlockSpec((1, rows, LANES), lambda i, j: (i, 0, 0))
    return pl.pallas_call(
        functools.partial(_nsa_body, seq=t),
        grid=(b, t // Q_BLOCK),
        in_specs=[
            pl.BlockSpec((1, NSA_HEADS, Q_BLOCK, LANES), lambda i, j: (i, 0, j, 0)),
            whole(n), whole(n),
            pl.BlockSpec((1, NSA_KV_HEADS, t, LANES), lambda i, j: (i, 0, 0, 0)),
            whole(t), whole(t), whole(t),
            pl.BlockSpec((1, Q_BLOCK, LANES), lambda i, j: (i, j, 0)),
            pl.BlockSpec((n, LANES), lambda i, j: (0, 0)),
            pl.BlockSpec((1, LANES), lambda i, j: (0, 0)),
        ],
        out_specs=pl.BlockSpec((1, Q_BLOCK, NSA_WIDTH), lambda i, j: (i, j, 0)),
        out_shape=jax.ShapeDtypeStruct((b, t, NSA_WIDTH), BF16),
        scratch_shapes=[pltpu.VMEM((NSA_KV_HEADS, t // SEL_CHUNK, Q_BLOCK, LANES), BF16)],
        compiler_params=_cparams(("arbitrary", "arbitrary")),
        name="nsa_attn",
    )(qpad, kcmp, vcmp, ks, vs, kw, vw, ba, ovl, ng_lane)


_IN_SIZES = (3 * GDN_WIDTH, GDN_WIDTH, GDN_HEADS, GDN_HEADS, NSA_WIDTH) + (NSA_KV_WIDTH,) * 6 + (3 * NSA_HEADS,)


def _prep_inproj_weight(w):
    offs = np.concatenate([[0], np.cumsum(_IN_SIZES)])
    seg = lambda i: w[:, int(offs[i]):int(offs[i + 1])]
    small = jnp.concatenate([seg(2), seg(3), seg(11)], axis=1)
    big = jnp.concatenate([seg(0), seg(1)] + [seg(i) for i in range(4, 11)]
                          + [small, jnp.zeros((w.shape[0], LANES - _SMALL_ROWS), w.dtype)], axis=1)
    return big.astype(BF16)


def _prep_compress(pe, w1, w2):
    hid = CMP_HIDDEN
    g, dh = NSA_KV_HEADS, NSA_HEAD_DIM
    half = CMP_LEN // 2
    w1r = w1.reshape(CMP_LEN, dh, hid)
    eye = jnp.eye(g, dtype=w1.dtype)
    expand = lambda part: jnp.einsum("ldh,gk->lgdkh", part, eye).reshape(half * g * dh, g * hid)
    wa, wb = expand(w1r[:half]), expand(w1r[half:])
    pe_row = lambda part: jnp.broadcast_to(part[:, None, :], (half, g, dh)).reshape(1, half * g * dh)
    pea, peb = pe_row(pe[:half]), pe_row(pe[half:])
    w2bd = jnp.einsum("hd,gk->ghkd", w2, eye).reshape(g * hid, g * dh)
    return pea, peb, wa.astype(BF16), wb.astype(BF16), w2bd.astype(BF16)


def _overlap_matrix(n_rows, seq):
    n_cmp = (seq - CMP_LEN) // CMP_STRIDE + 1
    n_blk = seq // SEL_BLOCK
    n = np.arange(n_rows)[:, None]
    j = np.arange(LANES)[None, :]
    start, end = n * CMP_STRIDE, n * CMP_STRIDE + CMP_LEN - 1
    ovl = (start < j * SEL_BLOCK + SEL_BLOCK) & (end >= j * SEL_BLOCK) & (n < n_cmp) & (j < n_blk)
    return jnp.asarray(ovl.astype(np.float32), dtype=BF16)


def kernel(x, c, positions, ada_w, ada_b, norm_g, ffn_w_in, ffn_w_out, mix_w_in, gdn_conv_w, gdn_a_log, gdn_dt_bias, gdn_norm_g, cmp_pe_k, cmp_w1_k, cmp_w2_k, cmp_pe_v, cmp_w1_v, cmp_w2_v, nsa_norm_g, mix_w_out, final_norm_g):
    b, t, d = x.shape
    depth = ada_w.shape[0]
    assert t % SEL_CHUNK == 0 and t >= WINDOW + Q_BLOCK and b <= 8

    c_pad = jnp.zeros((8, d), F32).at[:b].set(c.astype(F32))
    mod = _ada_mod(c_pad, ada_w.astype(F32), ada_b.astype(F32))
    mod = mod[:, :b].reshape(depth, b, 3, 3, d)
    mod8 = jnp.concatenate([mod, jnp.zeros((depth, b, 3, 5, d), F32)], axis=3)

    half = ROT_DIM // 2
    inv_freq = jnp.power(ROPE_THETA, -jnp.arange(half, dtype=F32) * (2.0 / ROT_DIM))
    dim = np.arange(LANES) % NSA_HEAD_DIM
    invf_lane = jnp.where(jnp.asarray(dim < ROT_DIM), inv_freq[dim % half], 0.0).reshape(1, LANES)
    cosf, sint = _rope_tables(positions.astype(F32).reshape(b, t, 1), invf_lane, tm=1024)

    n_rows = t // CMP_STRIDE
    ovl = _overlap_matrix(n_rows, t)
    fg = final_norm_g.reshape(1, d).astype(F32)

    w_in_all = ffn_w_in.astype(BF16)
    w_out_all = ffn_w_out.astype(BF16)
    w_mix_all = mix_w_out.astype(BF16)

    for l in range(depth):
        ng = lambda i: norm_g[l, i].reshape(1, d).astype(F32)

        x = _ffn(x, mod8[l, :, 0], ng(0), w_in_all, w_out_all, l, 0, fg, final_norm=False)

        (qkv, z, ba, bat, qpad, kc, vc, ks, vs, kw, vw) = _inproj(
            x, mod8[l, :, 1], ng(1), _prep_inproj_weight(mix_w_in[l]), cosf, sint)

        neg_a = -jnp.exp(gdn_a_log[l].astype(F32))
        dtb = gdn_dt_bias[l].astype(F32)
        avec = jnp.zeros((8, LANES), F32).at[0, GDN_HEADS:2 * GDN_HEADS].set(neg_a)
        avec = avec.at[1, GDN_HEADS:2 * GDN_HEADS].set(dtb)
        avect = jnp.zeros((_SMALL_ROWS, LANES), F32).at[GDN_HEADS:2 * GDN_HEADS, 0].set(neg_a)
        avect = avect.at[GDN_HEADS:2 * GDN_HEADS, 1].set(dtb)
        y_gdn = _gdn(qkv, z, ba, bat, gdn_conv_w[l].astype(F32), avec, avect,
                     gdn_norm_g[l].reshape(1, GDN_HEAD_DIM).astype(F32))

        kcmp = _compress(kc, *_prep_compress(cmp_pe_k[l], cmp_w1_k[l], cmp_w2_k[l]))
        vcmp = _compress(vc, *_prep_compress(cmp_pe_v[l], cmp_w1_v[l], cmp_w2_v[l]))
        ng_lane = jnp.tile(nsa_norm_g[l].astype(F32), NSA_KV_HEADS).reshape(1, LANES)
        y_nsa = _nsa(qpad, kcmp, vcmp, ks, vs, kw, vw, ba, ovl, ng_lane)

        x = _ffn(x, mod8[l, :, 2], ng(2), w_in_all, w_out_all, l, 1, fg, final_norm=(l == depth - 1),
                 mix=(mod8[l, :, 1], y_gdn, y_nsa, w_mix_all))
    return x
```

```python
import functools
import math

import jax
import jax.numpy as jnp
import numpy as np
from jax import lax
from jax.experimental import pallas as pl
from jax.experimental.pallas import tpu as pltpu

F32 = jnp.float32
BF16 = jnp.bfloat16

NORM_EPS = 1e-6
LANES = 128
GDN_HEADS = 4
GDN_HEAD_DIM = 128
GDN_WIDTH = GDN_HEADS * GDN_HEAD_DIM
GDN_CONV = 4
GDN_CHUNK = 64
GDN_PREP_STEP = 4 * GDN_CHUNK
GDN_SCAN_STEP = 2 * GDN_CHUNK
NSA_HEADS = 8
NSA_KV_HEADS = 2
NSA_REP = NSA_HEADS // NSA_KV_HEADS
NSA_HEAD_DIM = 64
NSA_WIDTH = NSA_HEADS * NSA_HEAD_DIM
NSA_KV_WIDTH = NSA_KV_HEADS * NSA_HEAD_DIM
CMP_LEN = 32
CMP_STRIDE = 16
CMP_HIDDEN = 2 * NSA_HEAD_DIM
SEL_BLOCK = 64
N_SELECT = 16
WINDOW = 512
Q_BLOCK = 128
SEL_CHUNK = 1024
SEL_LANES = SEL_CHUNK // SEL_BLOCK


def sel_lane_base(g):
    return NSA_HEAD_DIM if g == 0 else 0
ROPE_THETA = 500000.0
ROT_DIM = NSA_HEAD_DIM // 4
N_ADA = 9
NEG_BIG = -1e30
VMEM_LIMIT = 56 * 1024 * 1024

HIGHEST = lax.Precision.HIGHEST


def _cparams(sem):
    return pltpu.CompilerParams(dimension_semantics=sem, vmem_limit_bytes=VMEM_LIMIT)


def _dot(a, b):
    return jnp.dot(a, b, preferred_element_type=F32)


def _dot_nt(a, b):
    return lax.dot_general(a, b, (((1,), (1,)), ((), ())), preferred_element_type=F32)


def _dot_hi(a, b):
    return jnp.dot(a, b, preferred_element_type=F32, precision=HIGHEST)


def _split3(x):
    hi = x.astype(BF16)
    r1 = x - hi.astype(F32)
    mid = r1.astype(BF16)
    lo = (r1 - mid.astype(F32)).astype(BF16)
    return hi, mid, lo


def _sigmoid(x):
    return 1.0 / (1.0 + jnp.exp(-x))


def _silu(x):
    return x * _sigmoid(x)


def _rms_rows(x, g):
    ms = jnp.mean(x * x, axis=-1, keepdims=True)
    return x * lax.rsqrt(ms + NORM_EPS) * g


def _ada_body(c_ref, w_ref, b_ref, o_ref):
    cond = _silu(c_ref[...])
    hi, mid, lo = [t.astype(F32) for t in _split3(cond)]
    rows = cond.shape[0]
    w = w_ref[0]
    w_hi = w.astype(BF16)
    w_lo = (w - w_hi.astype(F32)).astype(BF16)
    first = _dot(jnp.concatenate([hi, mid, lo, jnp.zeros_like(hi)], axis=0).astype(BF16), w_hi)
    second = _dot(jnp.concatenate([hi, mid], axis=0).astype(BF16), w_lo)
    o_ref[0] = ((first[:rows] + first[rows:2 * rows]) + (first[2 * rows:3 * rows] + second[:rows])
                + second[rows:]) + b_ref[0]


def _ada_mod(c_pad, ada_w, ada_b):
    depth, d, n = ada_w.shape
    tn = n // N_ADA
    rows = c_pad.shape[0]
    return pl.pallas_call(
        _ada_body,
        grid=(depth, n // tn),
        in_specs=[
            pl.BlockSpec((rows, d), lambda l, j: (0, 0)),
            pl.BlockSpec((1, d, tn), lambda l, j: (l, 0, j)),
            pl.BlockSpec((1, 1, tn), lambda l, j: (l, 0, j)),
        ],
        out_specs=pl.BlockSpec((1, rows, tn), lambda l, j: (l, 0, j)),
        out_shape=jax.ShapeDtypeStruct((depth, rows, n), F32),
        compiler_params=_cparams(("arbitrary", "arbitrary")),
        name="ada_mod",
    )(c_pad, ada_w, ada_b.reshape(depth, 1, n))


def _rope_body(pos_ref, invf_ref, cos_ref, sin_ref):
    ang = pos_ref[0] * invf_ref[...]
    cos_ref[0] = jnp.cos(ang)
    sin_ref[0] = jnp.sin(ang)


def _rope_tables(pos_f, invf_lane, tm):
    b, t, _ = pos_f.shape
    spec = pl.BlockSpec((1, tm, LANES), lambda i, j: (i, j, 0))
    return pl.pallas_call(
        _rope_body,
        grid=(b, t // tm),
        in_specs=[pl.BlockSpec((1, tm, 1), lambda i, j: (i, j, 0)),
                  pl.BlockSpec((1, LANES), lambda i, j: (0, 0))],
        out_specs=[spec, spec],
        out_shape=[jax.ShapeDtypeStruct((b, t, LANES), F32)] * 2,
        compiler_params=_cparams(("arbitrary", "arbitrary")),
        name="rope_tables",
    )(pos_f, invf_lane)


def _ffn_body(*refs, final_norm, with_mix, tf):
    if with_mix:
        (x_ref, mod_ref, g_ref, wi_ref, wo_ref, fg_ref,
         mmod_ref, yg_ref, yn_ref, wm_ref, o_ref, act_scr) = refs
        mix = _dot(yg_ref[0], wm_ref[0:GDN_WIDTH, :]) + _dot(yn_ref[0], wm_ref[GDN_WIDTH:, :])
        x = x_ref[0] + mmod_ref[0, 2:3, :] * mix
    else:
        x_ref, mod_ref, g_ref, wi_ref, wo_ref, fg_ref, o_ref, act_scr = refs
        x = x_ref[0]
    y = _rms_rows(x, g_ref[...])
    h = (y * (1.0 + mod_ref[0, 1:2, :]) + mod_ref[0, 0:1, :]).astype(BF16)
    ff = wo_ref.shape[0]
    for j in range(ff // tf):
        a = _dot(h, wi_ref[:, j * tf:(j + 1) * tf])
        u = _dot(h, wi_ref[:, ff + j * tf:ff + (j + 1) * tf])
        act_scr[:, j * tf:(j + 1) * tf] = (_silu(a) * u).astype(BF16)
    out = x + (0.5 * mod_ref[0, 2:3, :]) * _dot(act_scr[...], wo_ref[...])
    if final_norm:
        out = _rms_rows(out, fg_ref[...])
    o_ref[0] = out


def _mod_spec(d, layer, sub):
    return pl.BlockSpec((None, 1, None, 8, d), lambda i, j: (layer, i, sub, 0, 0))


def _gain_spec(d, layer, sub):
    return pl.BlockSpec((None, 1, d), lambda i, j: (layer * 3 + sub, 0, 0))


def _ffn(x, mod_all, gain_all, w_in_all, w_out_all, layer, idx, fg, *, final_norm, mix=None, tm=1024, tf=256):
    b, t, d = x.shape
    ff = w_out_all.shape[2]
    sub = 2 * idx
    row = lambda n: pl.BlockSpec((1, tm, n), lambda i, j: (i, j, 0))
    vec_spec = pl.BlockSpec((1, d), lambda i, j: (0, 0))
    stacked = lambda r, c, *lead: pl.BlockSpec((None,) * len(lead) + (r, c), lambda i, j: lead + (0, 0),
                                               pipeline_mode=pl.Buffered(1))
    in_specs = [row(d), _mod_spec(d, layer, sub), _gain_spec(d, layer, sub),
                stacked(d, 2 * ff, layer, idx), stacked(ff, d, layer, idx), vec_spec]
    args = [x, mod_all, gain_all, w_in_all, w_out_all, fg]
    if mix is not None:
        yg, yn, w_mix_all = mix
        in_specs += [_mod_spec(d, layer, 1), row(GDN_WIDTH), row(NSA_WIDTH), stacked(d, d, layer)]
        args += [mod_all, yg, yn, w_mix_all]
    return pl.pallas_call(
        functools.partial(_ffn_body, final_norm=final_norm, with_mix=mix is not None, tf=tf),
        grid=(b, t // tm),
        in_specs=in_specs,
        out_specs=row(d),
        out_shape=jax.ShapeDtypeStruct((b, t, d), F32),
        scratch_shapes=[pltpu.VMEM((tm, ff), BF16)],
        compiler_params=_cparams(("arbitrary", "arbitrary")),
        name=("ffn_mix" if mix is not None else "ffn") + ("_final" if final_norm else ""),
    )(*args)


_O_QKV, _O_Z, _O_Q = 0, 1536, 2048
_O_KC, _O_VC, _O_KS, _O_VS, _O_KW, _O_VW, _O_SM = 2560, 2688, 2816, 2944, 3072, 3200, 3328
_IN_COLS = 3456
_SMALL_ROWS = 32


def _rope_tile(x, cosf, sint, first_half):
    fwd = pltpu.roll(x, 8, axis=1)
    bwd = pltpu.roll(x, LANES - 8, axis=1)
    return x * cosf + jnp.where(first_half, -bwd, fwd) * sint


def _inproj_body(x_ref, mod_ref, g_ref, w_ref, cos_ref, sin_ref,
                 qkv_ref, z_ref, ba_ref, bat_ref, qpad_ref,
                 kc_ref, vc_ref, ks_ref, vs_ref, kw_ref, vw_ref):
    y = _rms_rows(x_ref[0], g_ref[...])
    h = (y * (1.0 + mod_ref[0, 1:2, :]) + mod_ref[0, 0:1, :]).astype(BF16)
    cosf = cos_ref[0]
    sint = sin_ref[0]
    lane = lax.broadcasted_iota(jnp.int32, cosf.shape, 1)
    first_half = (lane % NSA_HEAD_DIM) < (ROT_DIM // 2)
    low = lane < NSA_HEAD_DIM

    for j in range(3):
        qkv_ref[0, :, j * 512:(j + 1) * 512] = _dot(h, w_ref[:, _O_QKV + j * 512:_O_QKV + (j + 1) * 512])
    z_ref[0] = _dot(h, w_ref[:, _O_Z:_O_Z + 512])
    ba = _dot(h, w_ref[:, _O_SM:_O_SM + LANES])
    ba_ref[0] = ba
    bat_ref[0] = ba.T[:_SMALL_ROWS, :]

    scale = NSA_HEAD_DIM ** -0.5 * math.log2(math.e)
    def pair(off):
        res = _dot(h, w_ref[:, off:off + 2 * LANES])
        return res[:, :LANES], res[:, LANES:]

    q_tiles = pair(_O_Q) + pair(_O_Q + 2 * LANES)
    for k in range(NSA_HEADS // 2):
        tile = _rope_tile(q_tiles[k], cosf, sint, first_half) * scale
        swapped = pltpu.roll(tile, NSA_HEAD_DIM, axis=1)
        grp = (2 * k) // NSA_REP
        if grp == 0:
            even = jnp.where(low, tile, 0.0)
            odd = jnp.where(low, swapped, 0.0)
        else:
            even = jnp.where(low, 0.0, swapped)
            odd = jnp.where(low, 0.0, tile)
        qpad_ref[0, 2 * k] = even.astype(BF16)
        qpad_ref[0, 2 * k + 1] = odd.astype(BF16)

    kc, vc = pair(_O_KC)
    kc_ref[0] = _rope_tile(kc, cosf, sint, first_half)
    vc_ref[0] = vc
    ks, vs = pair(_O_KS)
    ks = _rope_tile(ks, cosf, sint, first_half)
    tok = pl.program_id(1) * ks.shape[0] + lax.broadcasted_iota(jnp.int32, ks.shape, 0)
    blk = lax.shift_right_logical(tok, 6) & (SEL_LANES - 1)
    for grp in range(NSA_KV_HEADS):
        onehot = jnp.where(lane - sel_lane_base(grp) == blk, 1.0, 0.0)
        mixed = jnp.where(low, ks, onehot) if grp == 0 else jnp.where(low, onehot, ks)
        ks_ref[0, grp] = mixed.astype(BF16)
    vs_ref[0] = vs.astype(BF16)
    kw, vw = pair(_O_KW)
    kw_ref[0] = _rope_tile(kw, cosf, sint, first_half).astype(BF16)
    vw_ref[0] = vw.astype(BF16)


def _inproj(x, mod_all, gain_all, layer, w, cosf, sint, tm=512):
    b, t, d = x.shape
    row = lambda n: pl.BlockSpec((1, tm, n), lambda i, j: (i, j, 0))
    sds = lambda n, dt: jax.ShapeDtypeStruct((b, t, n), dt)
    return pl.pallas_call(
        _inproj_body,
        grid=(b, t // tm),
        in_specs=[
            row(d),
            _mod_spec(d, layer, 1),
            _gain_spec(d, layer, 1),
            pl.BlockSpec((d, _IN_COLS), lambda i, j: (0, 0)),
            row(LANES), row(LANES),
        ],
        out_specs=[
            row(3 * GDN_WIDTH), row(GDN_WIDTH), row(LANES),
            pl.BlockSpec((1, _SMALL_ROWS, tm), lambda i, j: (i, 0, j)),
            pl.BlockSpec((1, NSA_HEADS, tm, LANES), lambda i, j: (i, 0, j, 0)),
            row(LANES), row(LANES),
            pl.BlockSpec((1, NSA_KV_HEADS, tm, LANES), lambda i, j: (i, 0, j, 0)),
            row(LANES), row(LANES), row(LANES),
        ],
        out_shape=[
            sds(3 * GDN_WIDTH, F32), sds(GDN_WIDTH, F32), sds(LANES, F32),
            jax.ShapeDtypeStruct((b, _SMALL_ROWS, t), F32),
            jax.ShapeDtypeStruct((b, NSA_HEADS, t, LANES), BF16),
            sds(LANES, F32), sds(LANES, F32),
            jax.ShapeDtypeStruct((b, NSA_KV_HEADS, t, LANES), BF16),
            sds(LANES, BF16), sds(LANES, BF16), sds(LANES, BF16),
        ],
        compiler_params=_cparams(("arbitrary", "arbitrary")),
        name="mix_inproj",
    )(x, mod_all, gain_all, w, cosf, sint)


def _softplus(x):
    return jnp.maximum(x, 0.0) + jnp.log1p(jnp.exp(-jnp.abs(x)))


def _dot_inv(a, b):
    ah = a.astype(BF16)
    al = (a - ah.astype(F32)).astype(BF16)
    bh = b.astype(BF16)
    bl = (b - bh.astype(F32)).astype(BF16)
    m = a.shape[0]
    top = _dot(jnp.concatenate([ah, al], axis=0), bh)
    return top[:m] + top[m:] + _dot(ah, bl)


def _unit_lower_inverses(lows, order):
    n = lows[0].shape[0]
    r = lax.broadcasted_iota(jnp.int32, (n, n), 0)
    c = lax.broadcasted_iota(jnp.int32, (n, n), 1)
    eye = jnp.where(r == c, 1.0, 0.0)
    ps = [-low for low in lows]
    invs = [eye + p for p in ps]
    for _ in range(int(math.log2(order)) - 1):
        ps = [_dot_inv(p, p) for p in ps]
        invs = [inv + _dot_inv(inv, p) for inv, p in zip(invs, ps)]
    return invs


def _gdn_prep_body(qkv_ref, ba_ref, bat_ref, cw_ref, av_ref, avt_ref,
                   u_ref, wq_ref, ak_ref, egl_ref, buf):
    step = pl.program_id(1)
    c64 = GDN_CHUNK
    hd = GDN_HEAD_DIM
    nt = GDN_PREP_STEP

    @pl.when(step == 0)
    def _():
        buf[0:8, :] = jnp.zeros((8, 3 * GDN_WIDTH), F32)

    x = qkv_ref[0]
    buf[8:8 + nt, :] = x
    y = x * cw_ref[3:4, :]
    for j in range(GDN_CONV - 1):
        sh = GDN_CONV - 1 - j
        y = y + buf[8 - sh:8 - sh + nt, :] * cw_ref[j:j + 1, :]
    buf[0:8, :] = x[nt - 8:nt, :]
    y = _silu(y)

    ba = ba_ref[0]
    beta_all = _sigmoid(ba)
    g_col_all = av_ref[0:1, :] * _softplus(ba + av_ref[1:2, :])
    bat = bat_ref[0]
    g_row_all = avt_ref[:, 0:1] * _softplus(bat + avt_ref[:, 1:2])

    pw = 2 * c64
    rr = lax.broadcasted_iota(jnp.int32, (pw, pw), 0)
    kk = lax.broadcasted_iota(jnp.int32, (pw, pw), 1)
    same = lax.shift_right_logical(rr, 6) == lax.shift_right_logical(kk, 6)
    causal = same & (rr >= kk)
    strict = same & (rr > kk)
    tri = jnp.where(causal, 1.0, 0.0)
    tri_t = jnp.where(same & (rr <= kk), 1.0, 0.0)
    first = lax.broadcasted_iota(jnp.int32, (pw, 1), 0) < c64

    npair = nt // pw
    units = [(pc, h) for pc in range(npair) for h in range(GDN_HEADS)]
    rows_of = lambda pc: slice(pc * pw, (pc + 1) * pw)
    gc_cols = [_dot_hi(tri, g_col_all[rows_of(pc), :]) for pc in range(npair)]
    gc_rows = [_dot_hi(g_row_all[:, rows_of(pc)], tri_t) for pc in range(npair)]

    def l2n(v):
        return v * lax.rsqrt(jnp.sum(v * v, axis=-1, keepdims=True) + NORM_EPS)

    khs, kbs, decays, lows = [], [], [], []
    for pc, h in units:
        kh = l2n(y[rows_of(pc), GDN_WIDTH + h * hd:GDN_WIDTH + (h + 1) * hd])
        gcol = gc_cols[pc][:, GDN_HEADS + h:GDN_HEADS + h + 1]
        grow = gc_rows[pc][GDN_HEADS + h:GDN_HEADS + h + 1, :]
        decay = jnp.exp(jnp.where(causal, gcol - grow, -jnp.inf))
        kb = kh * beta_all[rows_of(pc), h:h + 1]
        low = jnp.where(strict, _dot_nt(kb.astype(BF16), kh.astype(BF16)) * decay, 0.0)
        khs.append(kh), kbs.append(kb), decays.append(decay), lows.append(low)

    invs = _unit_lower_inverses(lows, c64)

    egl_rows = [[] for _ in range(nt // c64)]
    for i, (pc, h) in enumerate(units):
        rows = rows_of(pc)
        kh, kb, decay = khs[i], kbs[i], decays[i]
        qh = l2n(y[rows, h * hd:(h + 1) * hd]) * (hd ** -0.5)
        vh = y[rows, 2 * GDN_WIDTH + h * hd:2 * GDN_WIDTH + (h + 1) * hd]
        gcol = gc_cols[pc][:, GDN_HEADS + h:GDN_HEADS + h + 1]
        grow = gc_rows[pc][GDN_HEADS + h:GDN_HEADS + h + 1, :]
        glasts = [grow[:, c64 - 1:c64], grow[:, pw - 1:pw]]
        glast = jnp.where(first, glasts[0], glasts[1])
        eg = jnp.exp(gcol)
        rhs = jnp.concatenate([vh * beta_all[rows, h:h + 1], kb * eg], axis=1)
        sol = _dot_inv(invs[i], rhs)
        attn = _dot_nt(qh.astype(BF16), kh.astype(BF16)) * decay
        kd_t = (kh * jnp.exp(glast - gcol)).T
        qe = (qh * eg).astype(BF16)
        u_ref[0, h, rows, :] = sol[:, :hd]
        for half in range(2):
            ch = 2 * pc + half
            part = slice(half * c64, (half + 1) * c64)
            wq_ref[0, h, ch, 0:c64, :] = sol[part, hd:].astype(BF16)
            wq_ref[0, h, ch, c64:2 * c64, :] = qe[part, :]
            ak_ref[0, h, ch, 0:c64, :] = attn[part, part].astype(BF16)
            ak_ref[0, h, ch, c64:c64 + hd, :] = kd_t[:, part].astype(BF16)
            egl_rows[ch].append(jnp.broadcast_to(jnp.exp(glasts[half]), (1, LANES)))
    for ch in range(nt // c64):
        egl_ref[0, ch] = jnp.concatenate(egl_rows[ch] + [jnp.zeros((8 - GDN_HEADS, LANES), F32)], axis=0)


def _gdn_prep(qkv, ba, bat, conv_w, avec, avect):
    b, t, _ = qkv.shape
    nt = GDN_PREP_STEP
    nch = nt // GDN_CHUNK
    hd = GDN_HEAD_DIM
    row = lambda n: pl.BlockSpec((1, nt, n), lambda i, j: (i, j, 0))
    return pl.pallas_call(
        _gdn_prep_body,
        grid=(b, t // nt),
        in_specs=[
            row(3 * GDN_WIDTH), row(LANES),
            pl.BlockSpec((1, _SMALL_ROWS, nt), lambda i, j: (i, 0, j)),
            pl.BlockSpec((GDN_CONV, 3 * GDN_WIDTH), lambda i, j: (0, 0)),
            pl.BlockSpec((8, LANES), lambda i, j: (0, 0)),
            pl.BlockSpec((_SMALL_ROWS, LANES), lambda i, j: (0, 0)),
        ],
        out_specs=[
            pl.BlockSpec((1, GDN_HEADS, nt, hd), lambda i, j: (i, 0, j, 0)),
            pl.BlockSpec((1, GDN_HEADS, nch, 2 * GDN_CHUNK, hd), lambda i, j: (i, 0, j, 0, 0)),
            pl.BlockSpec((1, GDN_HEADS, nch, GDN_CHUNK + hd, GDN_CHUNK), lambda i, j: (i, 0, j, 0, 0)),
            pl.BlockSpec((1, nch, 8, LANES), lambda i, j: (i, j, 0, 0)),
        ],
        out_shape=[
            jax.ShapeDtypeStruct((b, GDN_HEADS, t, hd), F32),
            jax.ShapeDtypeStruct((b, GDN_HEADS, t // GDN_CHUNK, 2 * GDN_CHUNK, hd), BF16),
            jax.ShapeDtypeStruct((b, GDN_HEADS, t // GDN_CHUNK, GDN_CHUNK + hd, GDN_CHUNK), BF16),
            jax.ShapeDtypeStruct((b, t // GDN_CHUNK, 8, LANES), F32),
        ],
        scratch_shapes=[pltpu.VMEM((8 + nt, 3 * GDN_WIDTH), F32)],
        compiler_params=_cparams(("arbitrary", "arbitrary")),
        name="gdn_prep",
    )(qkv, ba, bat, conv_w, avec, avect)


def _gdn_scan_body(u_ref, wq_ref, ak_ref, egl_ref, z_ref, ng_ref, o_ref, s_scr):
    c64 = GDN_CHUNK
    hd = GDN_HEAD_DIM
    nb = u_ref.shape[0]

    @pl.when(pl.program_id(0) == 0)
    def _():
        s_scr[...] = jnp.zeros_like(s_scr)

    chains = [(b, h) for b in range(nb) for h in range(GDN_HEADS)]
    states = [s_scr[b, h] for b, h in chains]
    for ch in range(GDN_SCAN_STEP // c64):
        rows = slice(ch * c64, (ch + 1) * c64)
        r1s = [_dot(wq_ref[b, h, ch], s.astype(BF16)) for (b, h), s in zip(chains, states)]
        vns = [(u_ref[b, h, rows, :] - r1[0:c64]).astype(BF16) for (b, h), r1 in zip(chains, r1s)]
        r2s = [_dot(ak_ref[b, h, ch], vn) for (b, h), vn in zip(chains, vns)]
        states = [s * egl_ref[b, ch, h:h + 1, :] + r2[c64:c64 + hd]
                  for (b, h), s, r2 in zip(chains, states, r2s)]
        for (b, h), r1, r2 in zip(chains, r1s, r2s):
            on = _rms_rows(r1[c64:2 * c64] + r2[0:c64], ng_ref[...])
            zh = z_ref[b, rows, h * hd:(h + 1) * hd]
            o_ref[b, rows, h * hd:(h + 1) * hd] = (on * _silu(zh)).astype(o_ref.dtype)
    for (b, h), s in zip(chains, states):
        s_scr[b, h] = s


def _gdn_scan(u, wq, ak, egl, z, norm_g):
    b, _, t, hd = u.shape
    nt = GDN_SCAN_STEP
    nch = nt // GDN_CHUNK
    return pl.pallas_call(
        _gdn_scan_body,
        grid=(t // nt,),
        in_specs=[
            pl.BlockSpec((b, GDN_HEADS, nt, hd), lambda j: (0, 0, j, 0)),
            pl.BlockSpec((b, GDN_HEADS, nch, 2 * GDN_CHUNK, hd), lambda j: (0, 0, j, 0, 0)),
            pl.BlockSpec((b, GDN_HEADS, nch, GDN_CHUNK + hd, GDN_CHUNK), lambda j: (0, 0, j, 0, 0)),
            pl.BlockSpec((b, nch, 8, LANES), lambda j: (0, j, 0, 0)),
            pl.BlockSpec((b, nt, GDN_WIDTH), lambda j: (0, j, 0)),
            pl.BlockSpec((1, hd), lambda j: (0, 0)),
        ],
        out_specs=pl.BlockSpec((b, nt, GDN_WIDTH), lambda j: (0, j, 0)),
        out_shape=jax.ShapeDtypeStruct((b, t, GDN_WIDTH), BF16),
        scratch_shapes=[pltpu.VMEM((b, GDN_HEADS, hd, hd), F32)],
        compiler_params=_cparams(("arbitrary",)),
        name="gdn_scan",
    )(u, wq, ak, egl, z, norm_g)


def _gdn(qkv, z, ba, bat, conv_w, avec, avect, norm_g):
    u, wq, ak, egl = _gdn_prep(qkv, ba, bat, conv_w, avec, avect)
    return _gdn_scan(u, wq, ak, egl, z, norm_g)


def _cmp_body(x_ref, pea_ref, peb_ref, wa_ref, wb_ref, w2_ref, o_ref):
    n_rows = x_ref.shape[1] // CMP_STRIDE
    x = jnp.concatenate([x_ref[0, pl.ds(l, n_rows, stride=CMP_STRIDE), :] for l in range(CMP_STRIDE)], axis=1)
    a = _dot((x + pea_ref[...]).astype(BF16), wa_ref[...])
    bm = _dot((x + peb_ref[...]).astype(BF16), wb_ref[...])
    n = a.shape[0]
    h1 = a + pltpu.roll(bm, n - 1, axis=0)
    o_ref[0] = _dot(_silu(h1).astype(BF16), w2_ref[...]).astype(o_ref.dtype)


def _compress(x, pea, peb, wa, wb, w2bd):
    b, t, wdt = x.shape
    n = t // CMP_STRIDE
    full = lambda s: pl.BlockSpec(s, lambda i: (0,) * len(s))
    return pl.pallas_call(
        _cmp_body,
        grid=(b,),
        in_specs=[pl.BlockSpec((1, t, wdt), lambda i: (i, 0, 0)),
                  full(pea.shape), full(peb.shape), full(wa.shape), full(wb.shape), full(w2bd.shape)],
        out_specs=pl.BlockSpec((1, n, LANES), lambda i: (i, 0, 0)),
        out_shape=jax.ShapeDtypeStruct((b, n, LANES), BF16),
        compiler_params=_cparams(("arbitrary",)),
        name="nsa_compress",
    )(x, pea, peb, wa, wb, w2bd)


def _nsa_body(q_ref, kcmp_ref, vcmp_ref, ks_ref, vs_ref, kw_ref, vw_ref, gate_ref,
              ovl_ref, ng_ref, o_ref, qbias_scr, *, seq):
    qb = pl.program_id(1)
    s0 = qb * Q_BLOCK
    n_cmp_rows = kcmp_ref.shape[1]
    n_blk = seq // SEL_BLOCK
    rq = NSA_REP * Q_BLOCK

    t_q = s0 + lax.broadcasted_iota(jnp.int32, (Q_BLOCK, 1), 0)
    lane = lax.broadcasted_iota(jnp.int32, (Q_BLOCK, LANES), 1)
    gates = _sigmoid(gate_ref[0])

    n_iota = lax.broadcasted_iota(jnp.int32, (Q_BLOCK, n_cmp_rows), 1)
    cmask = (n_iota * CMP_STRIDE + (CMP_LEN - 1)) <= t_q

    wstart = pl.multiple_of(jnp.maximum(s0 - WINDOW, 0), Q_BLOCK)
    kp = wstart + lax.broadcasted_iota(jnp.int32, (Q_BLOCK, WINDOW + Q_BLOCK), 1)
    wmask = (kp <= t_q) & (kp > t_q - WINDOW)

    cur = lax.shift_right_logical(t_q, 6)
    forced = (lane == 0) | (lane == cur) | (lane == cur - 1)
    valid = (lane * SEL_BLOCK <= t_q) & (lane < n_blk)
    jrow8 = lax.broadcasted_iota(jnp.int32, (8, Q_BLOCK), 0)

    groups_g = range(NSA_KV_HEADS)
    qs = [q_ref[0, g * NSA_REP:(g + 1) * NSA_REP].reshape(rq, LANES) for g in groups_g]
    lmasks = [(lane >= g * NSA_HEAD_DIM) & (lane < (g + 1) * NSA_HEAD_DIM) for g in groups_g]
    kv_lane = lax.broadcasted_iota(jnp.int32, (1, LANES), 1)
    own = [(kv_lane >= g * NSA_HEAD_DIM) & (kv_lane < (g + 1) * NSA_HEAD_DIM) for g in groups_g]
    den_lane = [(1 - g) * NSA_HEAD_DIM for g in groups_g]

    def with_ones(v, g):
        return jnp.where(own[g], v, jnp.ones_like(v))

    def compressed(g, sc):
        sc = jnp.where(cmask[None], sc.reshape(NSA_REP, Q_BLOCK, n_cmp_rows), -jnp.inf)
        m = jnp.max(sc, axis=-1, keepdims=True)
        m = jnp.where(m == -jnp.inf, 0.0, m)
        p = jnp.exp2(sc - m)
        p = p * (1.0 / jnp.maximum(jnp.sum(p, axis=-1, keepdims=True), 1e-30))
        o_cmp = _dot(p.reshape(rq, n_cmp_rows).astype(BF16), vcmp_ref[0])
        psum = p[0] + p[1] + p[2] + p[3]
        hi, mid, lo = _split3(psum)
        ovl = ovl_ref[...]
        imp = _dot(hi, ovl) + _dot(mid, ovl) + _dot(lo, ovl)
        return o_cmp.reshape(NSA_REP, Q_BLOCK, LANES), imp

    def select_blocks(imp):
        key = jnp.where(forced, jnp.inf, jnp.where(valid, imp, -jnp.inf))
        key_t = key.T[:n_blk, :]
        groups = [key_t[8 * v:8 * v + 8, :] for v in range(n_blk // 8)]
        cnts = [jnp.zeros((8, Q_BLOCK), F32) for _ in groups]
        for i in range(n_blk):
            row = key_t[i:i + 1, :]
            for v, grp in enumerate(groups):
                if v < i // 8:
                    beats = jnp.where(row > grp, 1.0, 0.0)
                elif v > i // 8:
                    beats = jnp.where(row >= grp, 1.0, 0.0)
                else:
                    beats = jnp.where(jrow8 > i % 8, jnp.where(row >= grp, 1.0, 0.0),
                                      jnp.where(row > grp, 1.0, 0.0))
                cnts[v] = cnts[v] + beats
        sel_t = jnp.where(jnp.concatenate(cnts, axis=0) < float(N_SELECT), 1.0, 0.0)
        if n_blk < LANES:
            sel_t = jnp.concatenate([sel_t, jnp.zeros((LANES - n_blk, Q_BLOCK), F32)], axis=0)
        return sel_t.T

    def store_block_bias(g, sel):
        past = lane < 2 * qb
        bias = jnp.where(past, (sel - 1.0) * (-NEG_BIG), NEG_BIG)
        base = sel_lane_base(g)
        here = (lane >= base) & (lane < base + SEL_LANES)
        for c in range(seq // SEL_CHUNK):
            moved = pltpu.roll(bias, (base - c * SEL_LANES) % LANES, axis=1)
            qbias_scr[g, c] = jnp.where(here, moved, 0.0).astype(BF16)

    def sel_scores(c, g):
        k0 = pl.multiple_of(c * SEL_CHUNK, SEL_CHUNK)
        lhs = (q_ref[0, g * NSA_REP:(g + 1) * NSA_REP] + qbias_scr[g, c][None]).reshape(rq, LANES)
        return _dot_nt(lhs, ks_ref[0, g, pl.ds(k0, SEL_CHUNK), :])

    def sel_update(g, c, scores, state):
        k0 = pl.multiple_of(c * SEL_CHUNK, SEL_CHUNK)
        vch = vs_ref[0, pl.ds(k0, SEL_CHUNK), :]
        m_old, acc = state
        s = scores.reshape(NSA_REP, Q_BLOCK, SEL_CHUNK)
        m_new = jnp.maximum(m_old, jnp.max(s, axis=-1, keepdims=True))
        p = jnp.exp2(s - m_new)
        pv = _dot(p.reshape(rq, SEL_CHUNK).astype(BF16), with_ones(vch, g)).reshape(NSA_REP, Q_BLOCK, LANES)
        return m_new, jnp.exp2(m_old - m_new) * acc + pv

    kwin = kw_ref[0, pl.ds(wstart, WINDOW + Q_BLOCK), :]
    vwin = vw_ref[0, pl.ds(wstart, WINDOW + Q_BLOCK), :]
    cmp_scores = [_dot_nt(qs[g], kcmp_ref[0]) for g in groups_g]
    win_scores = [_dot_nt(qs[g], kwin) for g in groups_g]
    sdiag = pl.multiple_of(s0, Q_BLOCK)
    vdiag = vs_ref[0, pl.ds(sdiag, Q_BLOCK), :]
    diag_scores = [_dot_nt(qs[g], ks_ref[0, g, pl.ds(sdiag, Q_BLOCK), :]) for g in groups_g]

    cmp_out = [compressed(g, cmp_scores[g]) for g in groups_g]

    wbias = jnp.where(wmask, 0.0, NEG_BIG)
    sws = [win_scores[g].reshape(NSA_REP, Q_BLOCK, WINDOW + Q_BLOCK) + wbias[None] for g in groups_g]
    pws = [jnp.exp2(sw - jnp.max(sw, axis=-1, keepdims=True)) for sw in sws]
    wins = [_dot(pws[g].reshape(rq, WINDOW + Q_BLOCK).astype(BF16), with_ones(vwin, g)).reshape(NSA_REP, Q_BLOCK, LANES)
            for g in groups_g]

    for g in groups_g:
        store_block_bias(g, select_blocks(cmp_out[g][1]))

    dmask = (lax.broadcasted_iota(jnp.int32, (Q_BLOCK, Q_BLOCK), 1)
             <= lax.broadcasted_iota(jnp.int32, (Q_BLOCK, Q_BLOCK), 0))
    dbias = jnp.where(dmask, 0.0, NEG_BIG)
    states = []
    for g in groups_g:
        s = diag_scores[g].reshape(NSA_REP, Q_BLOCK, Q_BLOCK) + dbias[None]
        m0 = jnp.max(s, axis=-1, keepdims=True)
        p = jnp.exp2(s - m0).reshape(rq, Q_BLOCK).astype(BF16)
        states.append((m0, _dot(p, with_ones(vdiag, g)).reshape(NSA_REP, Q_BLOCK, LANES)))

    def sel_step(c, states):
        scores = [sel_scores(c, g) for g in groups_g]
        return tuple(sel_update(g, c, scores[g], states[g]) for g in groups_g)

    n_past = (qb + SEL_CHUNK // Q_BLOCK - 1) // (SEL_CHUNK // Q_BLOCK)
    sel_out = lax.fori_loop(0, n_past, sel_step, tuple(states))

    for g in groups_g:
        lmask = lmasks[g]
        o_cmp = cmp_out[g][0]
        acc_s = sel_out[g][1]
        o_sel = acc_s * (1.0 / acc_s[:, :, den_lane[g]:den_lane[g] + 1])
        o_win = wins[g] * (1.0 / wins[g][:, :, den_lane[g]:den_lane[g] + 1])

        ys = []
        for r in range(NSA_REP):
            col = 2 * GDN_HEADS + (g * NSA_REP + r) * 3
            o = (gates[:, col:col + 1] * o_cmp[r] + gates[:, col + 1:col + 2] * o_sel[r]
                 + gates[:, col + 2:col + 3] * o_win[r])
            o = jnp.where(lmask, o, 0.0)
            ms = jnp.sum(o * o, axis=-1, keepdims=True) * (1.0 / NSA_HEAD_DIM)
            ys.append(o * lax.rsqrt(ms + NORM_EPS) * ng_ref[...])
        for pair in range(NSA_REP // 2):
            a, bb = ys[2 * pair], ys[2 * pair + 1]
            if g == 0:
                tile = a + pltpu.roll(bb, NSA_HEAD_DIM, axis=1)
            else:
                tile = pltpu.roll(a, NSA_HEAD_DIM, axis=1) + bb
            c0 = g * NSA_REP * NSA_HEAD_DIM + pair * LANES
            o_ref[0, :, c0:c0 + LANES] = tile.astype(o_ref.dtype)


def _nsa(qpad, kcmp, vcmp, ks, vs, kw, vw, ba, ovl, ng_lane):
    b, _, t, _ = qpad.shape
    n = kcmp.shape[1]
    whole = lambda rows: pl.BlockSpec((1, rows, LANES), lambda i, j: (i, 0, 0))
    return pl.pallas_call(
        functools.partial(_nsa_body, seq=t),
        grid=(b, t // Q_BLOCK),
        in_specs=[
            pl.BlockSpec((1, NSA_HEADS, Q_BLOCK, LANES), lambda i, j: (i, 0, j, 0)),
            whole(n), whole(n),
            pl.BlockSpec((1, NSA_KV_HEADS, t, LANES), lambda i, j: (i, 0, 0, 0)),
            whole(t), whole(t), whole(t),
            pl.BlockSpec((1, Q_BLOCK, LANES), lambda i, j: (i, j, 0)),
            pl.BlockSpec((n, LANES), lambda i, j: (0, 0)),
            pl.BlockSpec((1, LANES), lambda i, j: (0, 0)),
        ],
        out_specs=pl.BlockSpec((1, Q_BLOCK, NSA_WIDTH), lambda i, j: (i, j, 0)),
        out_shape=jax.ShapeDtypeStruct((b, t, NSA_WIDTH), BF16),
        scratch_shapes=[pltpu.VMEM((NSA_KV_HEADS, t // SEL_CHUNK, Q_BLOCK, LANES), BF16)],
        compiler_params=_cparams(("arbitrary", "arbitrary")),
        name="nsa_attn",
    )(qpad, kcmp, vcmp, ks, vs, kw, vw, ba, ovl, ng_lane)


_IN_SIZES = (3 * GDN_WIDTH, GDN_WIDTH, GDN_HEADS, GDN_HEADS, NSA_WIDTH) + (NSA_KV_WIDTH,) * 6 + (3 * NSA_HEADS,)


def _prep_inproj_weight(w):
    offs = np.concatenate([[0], np.cumsum(_IN_SIZES)])
    seg = lambda i: w[:, int(offs[i]):int(offs[i + 1])]
    small = jnp.concatenate([seg(2), seg(3), seg(11)], axis=1)
    big = jnp.concatenate([seg(0), seg(1)] + [seg(i) for i in range(4, 11)]
                          + [small, jnp.zeros((w.shape[0], LANES - _SMALL_ROWS), w.dtype)], axis=1)
    return big.astype(BF16)


def _prep_compress(pe, w1, w2):
    hid = CMP_HIDDEN
    g, dh = NSA_KV_HEADS, NSA_HEAD_DIM
    half = CMP_LEN // 2
    w1r = w1.reshape(CMP_LEN, dh, hid)
    eye = jnp.eye(g, dtype=w1.dtype)
    expand = lambda part: jnp.einsum("ldh,gk->lgdkh", part, eye).reshape(half * g * dh, g * hid)
    wa, wb = expand(w1r[:half]), expand(w1r[half:])
    pe_row = lambda part: jnp.broadcast_to(part[:, None, :], (half, g, dh)).reshape(1, half * g * dh)
    pea, peb = pe_row(pe[:half]), pe_row(pe[half:])
    w2bd = jnp.einsum("hd,gk->ghkd", w2, eye).reshape(g * hid, g * dh)
    return pea, peb, wa.astype(BF16), wb.astype(BF16), w2bd.astype(BF16)


def _overlap_matrix(n_rows, seq):
    n_cmp = (seq - CMP_LEN) // CMP_STRIDE + 1
    n_blk = seq // SEL_BLOCK
    n = np.arange(n_rows)[:, None]
    j = np.arange(LANES)[None, :]
    start, end = n * CMP_STRIDE, n * CMP_STRIDE + CMP_LEN - 1
    ovl = (start < j * SEL_BLOCK + SEL_BLOCK) & (end >= j * SEL_BLOCK) & (n < n_cmp) & (j < n_blk)
    return jnp.asarray(ovl.astype(np.float32), dtype=BF16)


def kernel(x, c, positions, ada_w, ada_b, norm_g, ffn_w_in, ffn_w_out, mix_w_in, gdn_conv_w, gdn_a_log, gdn_dt_bias, gdn_norm_g, cmp_pe_k, cmp_w1_k, cmp_w2_k, cmp_pe_v, cmp_w1_v, cmp_w2_v, nsa_norm_g, mix_w_out, final_norm_g):
    b, t, d = x.shape
    depth = ada_w.shape[0]
    assert t % SEL_CHUNK == 0 and t >= WINDOW + Q_BLOCK and b <= 8

    c_pad = jnp.zeros((8, d), F32).at[:b].set(c.astype(F32))
    mod = _ada_mod(c_pad, ada_w.astype(F32), ada_b.astype(F32))
    mod = mod[:, :b].reshape(depth, b, 3, 3, d)
    mod8 = jnp.concatenate([mod, jnp.zeros((depth, b, 3, 5, d), F32)], axis=3)

    half = ROT_DIM // 2
    inv_freq = jnp.power(ROPE_THETA, -jnp.arange(half, dtype=F32) * (2.0 / ROT_DIM))
    dim = np.arange(LANES) % NSA_HEAD_DIM
    invf_lane = jnp.where(jnp.asarray(dim < ROT_DIM), inv_freq[dim % half], 0.0).reshape(1, LANES)
    cosf, sint = _rope_tables(positions.astype(F32).reshape(b, t, 1), invf_lane, tm=1024)

    n_rows = t // CMP_STRIDE
    ovl = _overlap_matrix(n_rows, t)
    fg = final_norm_g.reshape(1, d).astype(F32)

    w_in_all = ffn_w_in.astype(BF16)
    w_out_all = ffn_w_out.astype(BF16)
    w_mix_all = mix_w_out.astype(BF16)

    gain_all = norm_g.astype(F32).reshape(depth * 3, 1, d)

    for l in range(depth):
        x = _ffn(x, mod8, gain_all, w_in_all, w_out_all, l, 0, fg, final_norm=False)

        (qkv, z, ba, bat, qpad, kc, vc, ks, vs, kw, vw) = _inproj(
            x, mod8, gain_all, l, _prep_inproj_weight(mix_w_in[l]), cosf, sint)

        neg_a = -jnp.exp(gdn_a_log[l].astype(F32))
        dtb = gdn_dt_bias[l].astype(F32)
        avec = jnp.zeros((8, LANES), F32).at[0, GDN_HEADS:2 * GDN_HEADS].set(neg_a)
        avec = avec.at[1, GDN_HEADS:2 * GDN_HEADS].set(dtb)
        avect = jnp.zeros((_SMALL_ROWS, LANES), F32).at[GDN_HEADS:2 * GDN_HEADS, 0].set(neg_a)
        avect = avect.at[GDN_HEADS:2 * GDN_HEADS, 1].set(dtb)
        y_gdn = _gdn(qkv, z, ba, bat, gdn_conv_w[l].astype(F32), avec, avect,
                     gdn_norm_g[l].reshape(1, GDN_HEAD_DIM).astype(F32))

        kcmp = _compress(kc, *_prep_compress(cmp_pe_k[l], cmp_w1_k[l], cmp_w2_k[l]))
        vcmp = _compress(vc, *_prep_compress(cmp_pe_v[l], cmp_w1_v[l], cmp_w2_v[l]))
        ng_lane = jnp.tile(nsa_norm_g[l].astype(F32), NSA_KV_HEADS).reshape(1, LANES)
        y_nsa = _nsa(qpad, kcmp, vcmp, ks, vs, kw, vw, ba, ovl, ng_lane)

        x = _ffn(x, mod8, gain_all, w_in_all, w_out_all, l, 1, fg, final_norm=(l == depth - 1),
                 mix=(y_gdn, y_nsa, w_mix_all))
    return x
```

```python
import functools
import math

import jax
import jax.numpy as jnp
import numpy as np
from jax import lax
from jax.experimental import pallas as pl
from jax.experimental.pallas import tpu as pltpu

F32 = jnp.float32
BF16 = jnp.bfloat16

NORM_EPS = 1e-6
LANES = 128
GDN_HEADS = 4
GDN_HEAD_DIM = 128
GDN_WIDTH = GDN_HEADS * GDN_HEAD_DIM
GDN_CONV = 4
GDN_CHUNK = 64
GDN_PREP_STEP = 4 * GDN_CHUNK
GDN_SCAN_STEP = 2 * GDN_CHUNK
NSA_HEADS = 8
NSA_KV_HEADS = 2
NSA_REP = NSA_HEADS // NSA_KV_HEADS
NSA_HEAD_DIM = 64
NSA_WIDTH = NSA_HEADS * NSA_HEAD_DIM
NSA_KV_WIDTH = NSA_KV_HEADS * NSA_HEAD_DIM
CMP_LEN = 32
CMP_STRIDE = 16
CMP_HIDDEN = 2 * NSA_HEAD_DIM
SEL_BLOCK = 64
N_SELECT = 16
WINDOW = 512
Q_BLOCK = 128
SEL_CHUNK = 1024
SEL_LANES = SEL_CHUNK // SEL_BLOCK


def sel_lane_base(g):
    return NSA_HEAD_DIM if g == 0 else 0
ROPE_THETA = 500000.0
ROT_DIM = NSA_HEAD_DIM // 4
N_ADA = 9
NEG_BIG = -1e30
VMEM_LIMIT = 56 * 1024 * 1024

HIGHEST = lax.Precision.HIGHEST


def _cparams(sem):
    return pltpu.CompilerParams(dimension_semantics=sem, vmem_limit_bytes=VMEM_LIMIT)


def _dot(a, b):
    return jnp.dot(a, b, preferred_element_type=F32)


def _dot_nt(a, b):
    return lax.dot_general(a, b, (((1,), (1,)), ((), ())), preferred_element_type=F32)


def _dot_hi(a, b):
    return jnp.dot(a, b, preferred_element_type=F32, precision=HIGHEST)


def _split3(x):
    hi = x.astype(BF16)
    r1 = x - hi.astype(F32)
    mid = r1.astype(BF16)
    lo = (r1 - mid.astype(F32)).astype(BF16)
    return hi, mid, lo


def _sigmoid(x):
    return 1.0 / (1.0 + jnp.exp(-x))


def _silu(x):
    return x * _sigmoid(x)


def _rms_rows(x, g):
    ms = jnp.mean(x * x, axis=-1, keepdims=True)
    return x * lax.rsqrt(ms + NORM_EPS) * g


def _ada_body(c_ref, w_ref, b_ref, o_ref):
    cond = _silu(c_ref[...])
    hi, mid, lo = [t.astype(F32) for t in _split3(cond)]
    rows = cond.shape[0]
    w = w_ref[0]
    w_hi = w.astype(BF16)
    w_lo = (w - w_hi.astype(F32)).astype(BF16)
    first = _dot(jnp.concatenate([hi, mid, lo, jnp.zeros_like(hi)], axis=0).astype(BF16), w_hi)
    second = _dot(jnp.concatenate([hi, mid], axis=0).astype(BF16), w_lo)
    o_ref[0] = ((first[:rows] + first[rows:2 * rows]) + (first[2 * rows:3 * rows] + second[:rows])
                + second[rows:]) + b_ref[0]


def _ada_mod(c_pad, ada_w, ada_b):
    depth, d, n = ada_w.shape
    tn = n // N_ADA
    rows = c_pad.shape[0]
    return pl.pallas_call(
        _ada_body,
        grid=(depth, n // tn),
        in_specs=[
            pl.BlockSpec((rows, d), lambda l, j: (0, 0)),
            pl.BlockSpec((1, d, tn), lambda l, j: (l, 0, j)),
            pl.BlockSpec((1, 1, tn), lambda l, j: (l, 0, j)),
        ],
        out_specs=pl.BlockSpec((1, rows, tn), lambda l, j: (l, 0, j)),
        out_shape=jax.ShapeDtypeStruct((depth, rows, n), F32),
        compiler_params=_cparams(("arbitrary", "arbitrary")),
        name="ada_mod",
    )(c_pad, ada_w, ada_b.reshape(depth, 1, n))


def _rope_body(pos_ref, invf_ref, cos_ref, sin_ref):
    ang = pos_ref[0] * invf_ref[...]
    cos_ref[0] = jnp.cos(ang)
    sin_ref[0] = jnp.sin(ang)


def _rope_tables(pos_f, invf_lane, tm):
    b, t, _ = pos_f.shape
    spec = pl.BlockSpec((1, tm, LANES), lambda i, j: (i, j, 0))
    return pl.pallas_call(
        _rope_body,
        grid=(b, t // tm),
        in_specs=[pl.BlockSpec((1, tm, 1), lambda i, j: (i, j, 0)),
                  pl.BlockSpec((1, LANES), lambda i, j: (0, 0))],
        out_specs=[spec, spec],
        out_shape=[jax.ShapeDtypeStruct((b, t, LANES), F32)] * 2,
        compiler_params=_cparams(("arbitrary", "arbitrary")),
        name="rope_tables",
    )(pos_f, invf_lane)


def _ffn_body(*refs, final_norm, with_mix, tf):
    if with_mix:
        (x_ref, mod_ref, g_ref, wi_ref, wo_ref, fg_ref,
         mmod_ref, yg_ref, yn_ref, wm_ref, o_ref, act_scr) = refs
        mix = _dot(yg_ref[0], wm_ref[0:GDN_WIDTH, :]) + _dot(yn_ref[0], wm_ref[GDN_WIDTH:, :])
        x = x_ref[0] + mmod_ref[0, 2:3, :] * mix
    else:
        x_ref, mod_ref, g_ref, wi_ref, wo_ref, fg_ref, o_ref, act_scr = refs
        x = x_ref[0]
    y = _rms_rows(x, g_ref[...])
    h = (y * (1.0 + mod_ref[0, 1:2, :]) + mod_ref[0, 0:1, :]).astype(BF16)
    ff = wo_ref.shape[0]
    for j in range(ff // tf):
        a = _dot(h, wi_ref[:, j * tf:(j + 1) * tf])
        u = _dot(h, wi_ref[:, ff + j * tf:ff + (j + 1) * tf])
        act_scr[:, j * tf:(j + 1) * tf] = (_silu(a) * u).astype(BF16)
    out = x + (0.5 * mod_ref[0, 2:3, :]) * _dot(act_scr[...], wo_ref[...])
    if final_norm:
        out = _rms_rows(out, fg_ref[...])
    o_ref[0] = out


def _mod_spec(d, layer, sub):
    return pl.BlockSpec((None, 1, None, 8, d), lambda i, j: (layer, i, sub, 0, 0))


def _gain_spec(d, layer, sub):
    return pl.BlockSpec((None, 1, d), lambda i, j: (layer * 3 + sub, 0, 0))


def _ffn(x, mod_all, gain_all, w_in_all, w_out_all, layer, idx, fg, *, final_norm, mix=None, tm=1024, tf=256):
    b, t, d = x.shape
    ff = w_out_all.shape[2]
    sub = 2 * idx
    row = lambda n: pl.BlockSpec((1, tm, n), lambda i, j: (i, j, 0))
    vec_spec = pl.BlockSpec((1, d), lambda i, j: (0, 0))
    stacked = lambda r, c, *lead: pl.BlockSpec((None,) * len(lead) + (r, c), lambda i, j: lead + (0, 0),
                                               pipeline_mode=pl.Buffered(1))
    in_specs = [row(d), _mod_spec(d, layer, sub), _gain_spec(d, layer, sub),
                stacked(d, 2 * ff, layer, idx), stacked(ff, d, layer, idx), vec_spec]
    args = [x, mod_all, gain_all, w_in_all, w_out_all, fg]
    if mix is not None:
        yg, yn, w_mix_all = mix
        in_specs += [_mod_spec(d, layer, 1), row(GDN_WIDTH), row(NSA_WIDTH), stacked(d, d, layer)]
        args += [mod_all, yg, yn, w_mix_all]
    return pl.pallas_call(
        functools.partial(_ffn_body, final_norm=final_norm, with_mix=mix is not None, tf=tf),
        grid=(b, t // tm),
        in_specs=in_specs,
        out_specs=row(d),
        out_shape=jax.ShapeDtypeStruct((b, t, d), F32),
        scratch_shapes=[pltpu.VMEM((tm, ff), BF16)],
        compiler_params=_cparams(("arbitrary", "arbitrary")),
        name=("ffn_mix" if mix is not None else "ffn") + ("_final" if final_norm else ""),
    )(*args)


_O_QKV, _O_Z, _O_Q = 0, 1536, 2048
_O_KC, _O_VC, _O_KS, _O_VS, _O_KW, _O_VW, _O_SM = 2560, 2688, 2816, 2944, 3072, 3200, 3328
_IN_COLS = 3456
_SMALL_ROWS = 32


def _rope_tile(x, cosf, sint, first_half):
    fwd = pltpu.roll(x, 8, axis=1)
    bwd = pltpu.roll(x, LANES - 8, axis=1)
    return x * cosf + jnp.where(first_half, -bwd, fwd) * sint


def _inproj_body(x_ref, mod_ref, g_ref, w_ref, cos_ref, sin_ref,
                 qkv_ref, z_ref, ba_ref, bat_ref, qpad_ref,
                 kc_ref, vc_ref, ks_ref, vs_ref, kw_ref, vw_ref):
    y = _rms_rows(x_ref[0], g_ref[...])
    h = (y * (1.0 + mod_ref[0, 1:2, :]) + mod_ref[0, 0:1, :]).astype(BF16)
    cosf = cos_ref[0]
    sint = sin_ref[0]
    lane = lax.broadcasted_iota(jnp.int32, cosf.shape, 1)
    first_half = (lane % NSA_HEAD_DIM) < (ROT_DIM // 2)
    low = lane < NSA_HEAD_DIM

    for j in range(3):
        qkv_ref[0, :, j * 512:(j + 1) * 512] = _dot(h, w_ref[:, _O_QKV + j * 512:_O_QKV + (j + 1) * 512])
    z_ref[0] = _dot(h, w_ref[:, _O_Z:_O_Z + 512])
    ba = _dot(h, w_ref[:, _O_SM:_O_SM + LANES])
    ba_ref[0] = ba
    bat_ref[0] = ba.T[:_SMALL_ROWS, :]

    scale = NSA_HEAD_DIM ** -0.5 * math.log2(math.e)
    def pair(off):
        res = _dot(h, w_ref[:, off:off + 2 * LANES])
        return res[:, :LANES], res[:, LANES:]

    q_tiles = pair(_O_Q) + pair(_O_Q + 2 * LANES)
    for k in range(NSA_HEADS // 2):
        tile = _rope_tile(q_tiles[k], cosf, sint, first_half) * scale
        swapped = pltpu.roll(tile, NSA_HEAD_DIM, axis=1)
        grp = (2 * k) // NSA_REP
        if grp == 0:
            even = jnp.where(low, tile, 0.0)
            odd = jnp.where(low, swapped, 0.0)
        else:
            even = jnp.where(low, 0.0, swapped)
            odd = jnp.where(low, 0.0, tile)
        qpad_ref[0, 2 * k] = even.astype(BF16)
        qpad_ref[0, 2 * k + 1] = odd.astype(BF16)

    kc, vc = pair(_O_KC)
    kc_ref[0] = _rope_tile(kc, cosf, sint, first_half)
    vc_ref[0] = vc
    ks, vs = pair(_O_KS)
    ks = _rope_tile(ks, cosf, sint, first_half)
    tok = pl.program_id(1) * ks.shape[0] + lax.broadcasted_iota(jnp.int32, ks.shape, 0)
    blk = lax.shift_right_logical(tok, 6) & (SEL_LANES - 1)
    for grp in range(NSA_KV_HEADS):
        onehot = jnp.where(lane - sel_lane_base(grp) == blk, 1.0, 0.0)
        mixed = jnp.where(low, ks, onehot) if grp == 0 else jnp.where(low, onehot, ks)
        ks_ref[0, grp] = mixed.astype(BF16)
    vs_ref[0] = vs.astype(BF16)
    kw, vw = pair(_O_KW)
    kw_ref[0] = _rope_tile(kw, cosf, sint, first_half).astype(BF16)
    vw_ref[0] = vw.astype(BF16)


def _inproj(x, mod_all, gain_all, layer, w, cosf, sint, tm=512):
    b, t, d = x.shape
    row = lambda n: pl.BlockSpec((1, tm, n), lambda i, j: (i, j, 0))
    sds = lambda n, dt: jax.ShapeDtypeStruct((b, t, n), dt)
    return pl.pallas_call(
        _inproj_body,
        grid=(b, t // tm),
        in_specs=[
            row(d),
            _mod_spec(d, layer, 1),
            _gain_spec(d, layer, 1),
            pl.BlockSpec((d, _IN_COLS), lambda i, j: (0, 0)),
            row(LANES), row(LANES),
        ],
        out_specs=[
            row(3 * GDN_WIDTH), row(GDN_WIDTH), row(LANES),
            pl.BlockSpec((1, _SMALL_ROWS, tm), lambda i, j: (i, 0, j)),
            pl.BlockSpec((1, NSA_HEADS, tm, LANES), lambda i, j: (i, 0, j, 0)),
            row(LANES), row(LANES),
            pl.BlockSpec((1, NSA_KV_HEADS, tm, LANES), lambda i, j: (i, 0, j, 0)),
            row(LANES), row(LANES), row(LANES),
        ],
        out_shape=[
            sds(3 * GDN_WIDTH, F32), sds(GDN_WIDTH, F32), sds(LANES, F32),
            jax.ShapeDtypeStruct((b, _SMALL_ROWS, t), F32),
            jax.ShapeDtypeStruct((b, NSA_HEADS, t, LANES), BF16),
            sds(LANES, F32), sds(LANES, F32),
            jax.ShapeDtypeStruct((b, NSA_KV_HEADS, t, LANES), BF16),
            sds(LANES, BF16), sds(LANES, BF16), sds(LANES, BF16),
        ],
        compiler_params=_cparams(("arbitrary", "arbitrary")),
        name="mix_inproj",
    )(x, mod_all, gain_all, w, cosf, sint)


def _softplus(x):
    return jnp.maximum(x, 0.0) + jnp.log1p(jnp.exp(-jnp.abs(x)))


def _dot_inv(a, b):
    ah = a.astype(BF16)
    al = (a - ah.astype(F32)).astype(BF16)
    bh = b.astype(BF16)
    bl = (b - bh.astype(F32)).astype(BF16)
    m = a.shape[0]
    top = _dot(jnp.concatenate([ah, al], axis=0), bh)
    return top[:m] + top[m:] + _dot(ah, bl)


def _unit_lower_inverses(lows, order):
    n = lows[0].shape[0]
    r = lax.broadcasted_iota(jnp.int32, (n, n), 0)
    c = lax.broadcasted_iota(jnp.int32, (n, n), 1)
    eye = jnp.where(r == c, 1.0, 0.0)
    ps = [-low for low in lows]
    invs = [eye + p for p in ps]
    for _ in range(int(math.log2(order)) - 1):
        ps = [_dot_inv(p, p) for p in ps]
        invs = [inv + _dot_inv(inv, p) for inv, p in zip(invs, ps)]
    return invs


def _gdn_prep_body(qkv_ref, ba_ref, bat_ref, cw_ref, av_ref, avt_ref,
                   u_ref, wq_ref, ak_ref, egl_ref, buf):
    step = pl.program_id(1)
    c64 = GDN_CHUNK
    hd = GDN_HEAD_DIM
    nt = GDN_PREP_STEP

    @pl.when(step == 0)
    def _():
        buf[0:8, :] = jnp.zeros((8, 3 * GDN_WIDTH), F32)

    x = qkv_ref[0]
    buf[8:8 + nt, :] = x
    y = x * cw_ref[3:4, :]
    for j in range(GDN_CONV - 1):
        sh = GDN_CONV - 1 - j
        y = y + buf[8 - sh:8 - sh + nt, :] * cw_ref[j:j + 1, :]
    buf[0:8, :] = x[nt - 8:nt, :]
    y = _silu(y)

    ba = ba_ref[0]
    beta_all = _sigmoid(ba)
    g_col_all = av_ref[0:1, :] * _softplus(ba + av_ref[1:2, :])
    bat = bat_ref[0]
    g_row_all = avt_ref[:, 0:1] * _softplus(bat + avt_ref[:, 1:2])

    pw = 2 * c64
    rr = lax.broadcasted_iota(jnp.int32, (pw, pw), 0)
    kk = lax.broadcasted_iota(jnp.int32, (pw, pw), 1)
    same = lax.shift_right_logical(rr, 6) == lax.shift_right_logical(kk, 6)
    causal = same & (rr >= kk)
    strict = same & (rr > kk)
    tri = jnp.where(causal, 1.0, 0.0)
    tri_t = jnp.where(same & (rr <= kk), 1.0, 0.0)
    first = lax.broadcasted_iota(jnp.int32, (pw, 1), 0) < c64

    npair = nt // pw
    units = [(pc, h) for pc in range(npair) for h in range(GDN_HEADS)]
    rows_of = lambda pc: slice(pc * pw, (pc + 1) * pw)
    gc_cols = [_dot_hi(tri, g_col_all[rows_of(pc), :]) for pc in range(npair)]
    gc_rows = [_dot_hi(g_row_all[:, rows_of(pc)], tri_t) for pc in range(npair)]

    def l2n(v):
        return v * lax.rsqrt(jnp.sum(v * v, axis=-1, keepdims=True) + NORM_EPS)

    khs, kbs, decays, lows = [], [], [], []
    for pc, h in units:
        kh = l2n(y[rows_of(pc), GDN_WIDTH + h * hd:GDN_WIDTH + (h + 1) * hd])
        gcol = gc_cols[pc][:, GDN_HEADS + h:GDN_HEADS + h + 1]
        grow = gc_rows[pc][GDN_HEADS + h:GDN_HEADS + h + 1, :]
        decay = jnp.exp(jnp.where(causal, gcol - grow, -jnp.inf))
        kb = kh * beta_all[rows_of(pc), h:h + 1]
        low = jnp.where(strict, _dot_nt(kb.astype(BF16), kh.astype(BF16)) * decay, 0.0)
        khs.append(kh), kbs.append(kb), decays.append(decay), lows.append(low)

    invs = _unit_lower_inverses(lows, c64)

    egl_rows = [[] for _ in range(nt // c64)]
    for i, (pc, h) in enumerate(units):
        rows = rows_of(pc)
        kh, kb, decay = khs[i], kbs[i], decays[i]
        qh = l2n(y[rows, h * hd:(h + 1) * hd]) * (hd ** -0.5)
        vh = y[rows, 2 * GDN_WIDTH + h * hd:2 * GDN_WIDTH + (h + 1) * hd]
        gcol = gc_cols[pc][:, GDN_HEADS + h:GDN_HEADS + h + 1]
        grow = gc_rows[pc][GDN_HEADS + h:GDN_HEADS + h + 1, :]
        glasts = [grow[:, c64 - 1:c64], grow[:, pw - 1:pw]]
        glast = jnp.where(first, glasts[0], glasts[1])
        eg = jnp.exp(gcol)
        rhs = jnp.concatenate([vh * beta_all[rows, h:h + 1], kb * eg], axis=1)
        sol = _dot_inv(invs[i], rhs)
        attn = _dot_nt(qh.astype(BF16), kh.astype(BF16)) * decay
        kd_t = (kh * jnp.exp(glast - gcol)).T
        qe = (qh * eg).astype(BF16)
        u_ref[0, h, rows, :] = sol[:, :hd]
        for half in range(2):
            ch = 2 * pc + half
            part = slice(half * c64, (half + 1) * c64)
            wq_ref[0, h, ch, 0:c64, :] = sol[part, hd:].astype(BF16)
            wq_ref[0, h, ch, c64:2 * c64, :] = qe[part, :]
            ak_ref[0, h, ch, 0:c64, :] = attn[part, part].astype(BF16)
            ak_ref[0, h, ch, c64:c64 + hd, :] = kd_t[:, part].astype(BF16)
            egl_rows[ch].append(jnp.broadcast_to(jnp.exp(glasts[half]), (1, LANES)))
    for ch in range(nt // c64):
        egl_ref[0, ch] = jnp.concatenate(egl_rows[ch] + [jnp.zeros((8 - GDN_HEADS, LANES), F32)], axis=0)


def _gdn_prep(qkv, ba, bat, conv_w, avec, avect):
    b, t, _ = qkv.shape
    nt = GDN_PREP_STEP
    nch = nt // GDN_CHUNK
    hd = GDN_HEAD_DIM
    row = lambda n: pl.BlockSpec((1, nt, n), lambda i, j: (i, j, 0))
    return pl.pallas_call(
        _gdn_prep_body,
        grid=(b, t // nt),
        in_specs=[
            row(3 * GDN_WIDTH), row(LANES),
            pl.BlockSpec((1, _SMALL_ROWS, nt), lambda i, j: (i, 0, j)),
            pl.BlockSpec((GDN_CONV, 3 * GDN_WIDTH), lambda i, j: (0, 0)),
            pl.BlockSpec((8, LANES), lambda i, j: (0, 0)),
            pl.BlockSpec((_SMALL_ROWS, LANES), lambda i, j: (0, 0)),
        ],
        out_specs=[
            pl.BlockSpec((1, GDN_HEADS, nt, hd), lambda i, j: (i, 0, j, 0)),
            pl.BlockSpec((1, GDN_HEADS, nch, 2 * GDN_CHUNK, hd), lambda i, j: (i, 0, j, 0, 0)),
            pl.BlockSpec((1, GDN_HEADS, nch, GDN_CHUNK + hd, GDN_CHUNK), lambda i, j: (i, 0, j, 0, 0)),
            pl.BlockSpec((1, nch, 8, LANES), lambda i, j: (i, j, 0, 0)),
        ],
        out_shape=[
            jax.ShapeDtypeStruct((b, GDN_HEADS, t, hd), F32),
            jax.ShapeDtypeStruct((b, GDN_HEADS, t // GDN_CHUNK, 2 * GDN_CHUNK, hd), BF16),
            jax.ShapeDtypeStruct((b, GDN_HEADS, t // GDN_CHUNK, GDN_CHUNK + hd, GDN_CHUNK), BF16),
            jax.ShapeDtypeStruct((b, t // GDN_CHUNK, 8, LANES), F32),
        ],
        scratch_shapes=[pltpu.VMEM((8 + nt, 3 * GDN_WIDTH), F32)],
        compiler_params=_cparams(("arbitrary", "arbitrary")),
        name="gdn_prep",
    )(qkv, ba, bat, conv_w, avec, avect)


def _gdn_scan_body(u_ref, wq_ref, ak_ref, egl_ref, z_ref, ng_ref, o_ref, s_scr):
    c64 = GDN_CHUNK
    hd = GDN_HEAD_DIM
    nb = u_ref.shape[0]

    @pl.when(pl.program_id(0) == 0)
    def _():
        s_scr[...] = jnp.zeros_like(s_scr)

    chains = [(b, h) for b in range(nb) for h in range(GDN_HEADS)]
    states = [s_scr[b, h] for b, h in chains]
    for ch in range(GDN_SCAN_STEP // c64):
        rows = slice(ch * c64, (ch + 1) * c64)
        r1s = [_dot(wq_ref[b, h, ch], s.astype(BF16)) for (b, h), s in zip(chains, states)]
        vns = [(u_ref[b, h, rows, :] - r1[0:c64]).astype(BF16) for (b, h), r1 in zip(chains, r1s)]
        r2s = [_dot(ak_ref[b, h, ch], vn) for (b, h), vn in zip(chains, vns)]
        states = [s * egl_ref[b, ch, h:h + 1, :] + r2[c64:c64 + hd]
                  for (b, h), s, r2 in zip(chains, states, r2s)]
        for (b, h), r1, r2 in zip(chains, r1s, r2s):
            on = _rms_rows(r1[c64:2 * c64] + r2[0:c64], ng_ref[...])
            zh = z_ref[b, rows, h * hd:(h + 1) * hd]
            o_ref[b, rows, h * hd:(h + 1) * hd] = (on * _silu(zh)).astype(o_ref.dtype)
    for (b, h), s in zip(chains, states):
        s_scr[b, h] = s


def _gdn_scan(u, wq, ak, egl, z, norm_g):
    b, _, t, hd = u.shape
    nt = GDN_SCAN_STEP
    nch = nt // GDN_CHUNK
    return pl.pallas_call(
        _gdn_scan_body,
        grid=(t // nt,),
        in_specs=[
            pl.BlockSpec((b, GDN_HEADS, nt, hd), lambda j: (0, 0, j, 0)),
            pl.BlockSpec((b, GDN_HEADS, nch, 2 * GDN_CHUNK, hd), lambda j: (0, 0, j, 0, 0)),
            pl.BlockSpec((b, GDN_HEADS, nch, GDN_CHUNK + hd, GDN_CHUNK), lambda j: (0, 0, j, 0, 0)),
            pl.BlockSpec((b, nch, 8, LANES), lambda j: (0, j, 0, 0)),
            pl.BlockSpec((b, nt, GDN_WIDTH), lambda j: (0, j, 0)),
            pl.BlockSpec((1, hd), lambda j: (0, 0)),
        ],
        out_specs=pl.BlockSpec((b, nt, GDN_WIDTH), lambda j: (0, j, 0)),
        out_shape=jax.ShapeDtypeStruct((b, t, GDN_WIDTH), BF16),
        scratch_shapes=[pltpu.VMEM((b, GDN_HEADS, hd, hd), F32)],
        compiler_params=_cparams(("arbitrary",)),
        name="gdn_scan",
    )(u, wq, ak, egl, z, norm_g)


def _gdn(qkv, z, ba, bat, conv_w, avec, avect, norm_g):
    u, wq, ak, egl = _gdn_prep(qkv, ba, bat, conv_w, avec, avect)
    return _gdn_scan(u, wq, ak, egl, z, norm_g)


def _cmp_body(x_ref, pea_ref, peb_ref, wa_ref, wb_ref, w2_ref, o_ref):
    n_rows = x_ref.shape[1] // CMP_STRIDE
    x = jnp.concatenate([x_ref[0, pl.ds(l, n_rows, stride=CMP_STRIDE), :] for l in range(CMP_STRIDE)], axis=1)
    a = _dot((x + pea_ref[...]).astype(BF16), wa_ref[...])
    bm = _dot((x + peb_ref[...]).astype(BF16), wb_ref[...])
    n = a.shape[0]
    h1 = a + pltpu.roll(bm, n - 1, axis=0)
    o_ref[0] = _dot(_silu(h1).astype(BF16), w2_ref[...]).astype(o_ref.dtype)


def _compress(x, pea, peb, wa, wb, w2bd):
    b, t, wdt = x.shape
    n = t // CMP_STRIDE
    full = lambda s: pl.BlockSpec(s, lambda i: (0,) * len(s))
    return pl.pallas_call(
        _cmp_body,
        grid=(b,),
        in_specs=[pl.BlockSpec((1, t, wdt), lambda i: (i, 0, 0)),
                  full(pea.shape), full(peb.shape), full(wa.shape), full(wb.shape), full(w2bd.shape)],
        out_specs=pl.BlockSpec((1, n, LANES), lambda i: (i, 0, 0)),
        out_shape=jax.ShapeDtypeStruct((b, n, LANES), BF16),
        compiler_params=_cparams(("arbitrary",)),
        name="nsa_compress",
    )(x, pea, peb, wa, wb, w2bd)


def _nsa_body(q_ref, kcmp_ref, vcmp_ref, ks_ref, vs_ref, kw_ref, vw_ref, gate_ref,
              ovl_ref, ng_ref, o_ref, qbias_scr, *, seq):
    qb = pl.program_id(1)
    s0 = qb * Q_BLOCK
    n_cmp_rows = kcmp_ref.shape[1]
    n_blk = seq // SEL_BLOCK
    rq = NSA_REP * Q_BLOCK

    t_q = s0 + lax.broadcasted_iota(jnp.int32, (Q_BLOCK, 1), 0)
    lane = lax.broadcasted_iota(jnp.int32, (Q_BLOCK, LANES), 1)
    gates = _sigmoid(gate_ref[0])

    n_iota = lax.broadcasted_iota(jnp.int32, (Q_BLOCK, n_cmp_rows), 1)
    cmask = (n_iota * CMP_STRIDE + (CMP_LEN - 1)) <= t_q

    wstart = pl.multiple_of(jnp.maximum(s0 - WINDOW, 0), Q_BLOCK)
    kp = wstart + lax.broadcasted_iota(jnp.int32, (Q_BLOCK, WINDOW + Q_BLOCK), 1)
    wmask = (kp <= t_q) & (kp > t_q - WINDOW)

    cur = lax.shift_right_logical(t_q, 6)
    forced = (lane == 0) | (lane == cur) | (lane == cur - 1)
    valid = (lane * SEL_BLOCK <= t_q) & (lane < n_blk)
    jrow8 = lax.broadcasted_iota(jnp.int32, (8, Q_BLOCK), 0)

    groups_g = range(NSA_KV_HEADS)
    qs = [q_ref[0, g * NSA_REP:(g + 1) * NSA_REP].reshape(rq, LANES) for g in groups_g]
    lmasks = [(lane >= g * NSA_HEAD_DIM) & (lane < (g + 1) * NSA_HEAD_DIM) for g in groups_g]
    kv_lane = lax.broadcasted_iota(jnp.int32, (1, LANES), 1)
    own = [(kv_lane >= g * NSA_HEAD_DIM) & (kv_lane < (g + 1) * NSA_HEAD_DIM) for g in groups_g]
    den_lane = [(1 - g) * NSA_HEAD_DIM for g in groups_g]

    def with_ones(v, g):
        return jnp.where(own[g], v, jnp.ones_like(v))

    def compressed(g, sc):
        sc = jnp.where(cmask[None], sc.reshape(NSA_REP, Q_BLOCK, n_cmp_rows), -jnp.inf)
        m = jnp.max(sc, axis=-1, keepdims=True)
        m = jnp.where(m == -jnp.inf, 0.0, m)
        p = jnp.exp2(sc - m)
        p = p * (1.0 / jnp.maximum(jnp.sum(p, axis=-1, keepdims=True), 1e-30))
        o_cmp = _dot(p.reshape(rq, n_cmp_rows).astype(BF16), vcmp_ref[0])
        psum = p[0] + p[1] + p[2] + p[3]
        hi, mid, lo = _split3(psum)
        ovl = ovl_ref[...]
        imp = _dot(hi, ovl) + _dot(mid, ovl) + _dot(lo, ovl)
        return o_cmp.reshape(NSA_REP, Q_BLOCK, LANES), imp

    def select_blocks(imp):
        key = jnp.where(forced, jnp.inf, jnp.where(valid, imp, -jnp.inf))
        key_t = key.T[:n_blk, :]
        groups = [key_t[8 * v:8 * v + 8, :] for v in range(n_blk // 8)]
        cnts = [jnp.zeros((8, Q_BLOCK), F32) for _ in groups]
        for i in range(n_blk):
            row = key_t[i:i + 1, :]
            for v, grp in enumerate(groups):
                if v < i // 8:
                    beats = jnp.where(row > grp, 1.0, 0.0)
                elif v > i // 8:
                    beats = jnp.where(row >= grp, 1.0, 0.0)
                else:
                    beats = jnp.where(jrow8 > i % 8, jnp.where(row >= grp, 1.0, 0.0),
                                      jnp.where(row > grp, 1.0, 0.0))
                cnts[v] = cnts[v] + beats
        sel_t = jnp.where(jnp.concatenate(cnts, axis=0) < float(N_SELECT), 1.0, 0.0)
        if n_blk < LANES:
            sel_t = jnp.concatenate([sel_t, jnp.zeros((LANES - n_blk, Q_BLOCK), F32)], axis=0)
        return sel_t.T

    def store_block_bias(g, sel):
        past = lane < 2 * qb
        bias = jnp.where(past, (sel - 1.0) * (-NEG_BIG), NEG_BIG)
        base = sel_lane_base(g)
        here = (lane >= base) & (lane < base + SEL_LANES)
        for c in range(seq // SEL_CHUNK):
            moved = pltpu.roll(bias, (base - c * SEL_LANES) % LANES, axis=1)
            qbias_scr[g, c] = jnp.where(here, moved, 0.0).astype(BF16)

    def sel_scores(c, g):
        k0 = pl.multiple_of(c * SEL_CHUNK, SEL_CHUNK)
        lhs = (q_ref[0, g * NSA_REP:(g + 1) * NSA_REP] + qbias_scr[g, c][None]).reshape(rq, LANES)
        return _dot_nt(lhs, ks_ref[0, g, pl.ds(k0, SEL_CHUNK), :])

    def sel_update(g, c, scores, state):
        k0 = pl.multiple_of(c * SEL_CHUNK, SEL_CHUNK)
        vch = vs_ref[0, pl.ds(k0, SEL_CHUNK), :]
        m_old, acc = state
        s = scores.reshape(NSA_REP, Q_BLOCK, SEL_CHUNK)
        m_new = jnp.maximum(m_old, jnp.max(s, axis=-1, keepdims=True))
        p = jnp.exp2(s - m_new)
        pv = _dot(p.reshape(rq, SEL_CHUNK).astype(BF16), with_ones(vch, g)).reshape(NSA_REP, Q_BLOCK, LANES)
        return m_new, jnp.exp2(m_old - m_new) * acc + pv

    kwin = kw_ref[0, pl.ds(wstart, WINDOW + Q_BLOCK), :]
    vwin = vw_ref[0, pl.ds(wstart, WINDOW + Q_BLOCK), :]
    cmp_scores = [_dot_nt(qs[g], kcmp_ref[0]) for g in groups_g]
    win_scores = [_dot_nt(qs[g], kwin) for g in groups_g]
    sdiag = pl.multiple_of(s0, Q_BLOCK)
    vdiag = vs_ref[0, pl.ds(sdiag, Q_BLOCK), :]
    diag_scores = [_dot_nt(qs[g], ks_ref[0, g, pl.ds(sdiag, Q_BLOCK), :]) for g in groups_g]

    cmp_out = [compressed(g, cmp_scores[g]) for g in groups_g]

    wbias = jnp.where(wmask, 0.0, NEG_BIG)
    sws = [win_scores[g].reshape(NSA_REP, Q_BLOCK, WINDOW + Q_BLOCK) + wbias[None] for g in groups_g]
    pws = [jnp.exp2(sw - jnp.max(sw, axis=-1, keepdims=True)) for sw in sws]
    wins = [_dot(pws[g].reshape(rq, WINDOW + Q_BLOCK).astype(BF16), with_ones(vwin, g)).reshape(NSA_REP, Q_BLOCK, LANES)
            for g in groups_g]

    for g in groups_g:
        store_block_bias(g, select_blocks(cmp_out[g][1]))

    dmask = (lax.broadcasted_iota(jnp.int32, (Q_BLOCK, Q_BLOCK), 1)
             <= lax.broadcasted_iota(jnp.int32, (Q_BLOCK, Q_BLOCK), 0))
    dbias = jnp.where(dmask, 0.0, NEG_BIG)
    states = []
    for g in groups_g:
        s = diag_scores[g].reshape(NSA_REP, Q_BLOCK, Q_BLOCK) + dbias[None]
        m0 = jnp.max(s, axis=-1, keepdims=True)
        p = jnp.exp2(s - m0).reshape(rq, Q_BLOCK).astype(BF16)
        states.append((m0, _dot(p, with_ones(vdiag, g)).reshape(NSA_REP, Q_BLOCK, LANES)))

    def sel_step(c, states):
        scores = [sel_scores(c, g) for g in groups_g]
        return tuple(sel_update(g, c, scores[g], states[g]) for g in groups_g)

    n_past = (qb + SEL_CHUNK // Q_BLOCK - 1) // (SEL_CHUNK // Q_BLOCK)
    sel_out = lax.fori_loop(0, n_past, sel_step, tuple(states))

    heads = [(g, r) for g in groups_g for r in range(NSA_REP)]
    col_of = lambda g, r: 2 * GDN_HEADS + (g * NSA_REP + r) * 3
    c_sel = [gates[:, col_of(g, r) + 1:col_of(g, r) + 2]
             * (1.0 / sel_out[g][1][r][:, den_lane[g]:den_lane[g] + 1]) for g, r in heads]
    c_win = [gates[:, col_of(g, r) + 2:col_of(g, r) + 3]
             * (1.0 / wins[g][r][:, den_lane[g]:den_lane[g] + 1]) for g, r in heads]
    outs = [gates[:, col_of(g, r):col_of(g, r) + 1] * cmp_out[g][0][r] + c_sel[i] * sel_out[g][1][r]
            + c_win[i] * wins[g][r] for i, (g, r) in enumerate(heads)]
    outs = [jnp.where(lmasks[g], o, 0.0) for (g, r), o in zip(heads, outs)]
    mss = [jnp.sum(o * o, axis=-1, keepdims=True) * (1.0 / NSA_HEAD_DIM) for o in outs]
    ys = [o * lax.rsqrt(ms + NORM_EPS) * ng_ref[...] for o, ms in zip(outs, mss)]
    for g in groups_g:
        for pair in range(NSA_REP // 2):
            a, bb = ys[g * NSA_REP + 2 * pair], ys[g * NSA_REP + 2 * pair + 1]
            if g == 0:
                tile = a + pltpu.roll(bb, NSA_HEAD_DIM, axis=1)
            else:
                tile = pltpu.roll(a, NSA_HEAD_DIM, axis=1) + bb
            c0 = g * NSA_REP * NSA_HEAD_DIM + pair * LANES
            o_ref[0, :, c0:c0 + LANES] = tile.astype(o_ref.dtype)


def _nsa(qpad, kcmp, vcmp, ks, vs, kw, vw, ba, ovl, ng_lane):
    b, _, t, _ = qpad.shape
    n = kcmp.shape[1]
    whole = lambda rows: pl.BlockSpec((1, rows, LANES), lambda i, j: (i, 0, 0))
    return pl.pallas_call(
        functools.partial(_nsa_body, seq=t),
        grid=(b, t // Q_BLOCK),
        in_specs=[
            pl.BlockSpec((1, NSA_HEADS, Q_BLOCK, LANES), lambda i, j: (i, 0, j, 0)),
            whole(n), whole(n),
            pl.BlockSpec((1, NSA_KV_HEADS, t, LANES), lambda i, j: (i, 0, 0, 0)),
            whole(t), whole(t), whole(t),
            pl.BlockSpec((1, Q_BLOCK, LANES), lambda i, j: (i, j, 0)),
            pl.BlockSpec((n, LANES), lambda i, j: (0, 0)),
            pl.BlockSpec((1, LANES), lambda i, j: (0, 0)),
        ],
        out_specs=pl.BlockSpec((1, Q_BLOCK, NSA_WIDTH), lambda i, j: (i, j, 0)),
        out_shape=jax.ShapeDtypeStruct((b, t, NSA_WIDTH), BF16),
        scratch_shapes=[pltpu.VMEM((NSA_KV_HEADS, t // SEL_CHUNK, Q_BLOCK, LANES), BF16)],
        compiler_params=_cparams(("arbitrary", "arbitrary")),
        name="nsa_attn",
    )(qpad, kcmp, vcmp, ks, vs, kw, vw, ba, ovl, ng_lane)


_IN_SIZES = (3 * GDN_WIDTH, GDN_WIDTH, GDN_HEADS, GDN_HEADS, NSA_WIDTH) + (NSA_KV_WIDTH,) * 6 + (3 * NSA_HEADS,)


def _prep_inproj_weight(w):
    offs = np.concatenate([[0], np.cumsum(_IN_SIZES)])
    seg = lambda i: w[:, int(offs[i]):int(offs[i + 1])]
    small = jnp.concatenate([seg(2), seg(3), seg(11)], axis=1)
    big = jnp.concatenate([seg(0), seg(1)] + [seg(i) for i in range(4, 11)]
                          + [small, jnp.zeros((w.shape[0], LANES - _SMALL_ROWS), w.dtype)], axis=1)
    return big.astype(BF16)


def _prep_compress(pe, w1, w2):
    hid = CMP_HIDDEN
    g, dh = NSA_KV_HEADS, NSA_HEAD_DIM
    half = CMP_LEN // 2
    w1r = w1.reshape(CMP_LEN, dh, hid)
    eye = jnp.eye(g, dtype=w1.dtype)
    expand = lambda part: jnp.einsum("ldh,gk->lgdkh", part, eye).reshape(half * g * dh, g * hid)
    wa, wb = expand(w1r[:half]), expand(w1r[half:])
    pe_row = lambda part: jnp.broadcast_to(part[:, None, :], (half, g, dh)).reshape(1, half * g * dh)
    pea, peb = pe_row(pe[:half]), pe_row(pe[half:])
    w2bd = jnp.einsum("hd,gk->ghkd", w2, eye).reshape(g * hid, g * dh)
    return pea, peb, wa.astype(BF16), wb.astype(BF16), w2bd.astype(BF16)


def _overlap_matrix(n_rows, seq):
    n_cmp = (seq - CMP_LEN) // CMP_STRIDE + 1
    n_blk = seq // SEL_BLOCK
    n = np.arange(n_rows)[:, None]
    j = np.arange(LANES)[None, :]
    start, end = n * CMP_STRIDE, n * CMP_STRIDE + CMP_LEN - 1
    ovl = (start < j * SEL_BLOCK + SEL_BLOCK) & (end >= j * SEL_BLOCK) & (n < n_cmp) & (j < n_blk)
    return jnp.asarray(ovl.astype(np.float32), dtype=BF16)


def kernel(x, c, positions, ada_w, ada_b, norm_g, ffn_w_in, ffn_w_out, mix_w_in, gdn_conv_w, gdn_a_log, gdn_dt_bias, gdn_norm_g, cmp_pe_k, cmp_w1_k, cmp_w2_k, cmp_pe_v, cmp_w1_v, cmp_w2_v, nsa_norm_g, mix_w_out, final_norm_g):
    b, t, d = x.shape
    depth = ada_w.shape[0]
    assert t % SEL_CHUNK == 0 and t >= WINDOW + Q_BLOCK and b <= 8

    c_pad = jnp.zeros((8, d), F32).at[:b].set(c.astype(F32))
    mod = _ada_mod(c_pad, ada_w.astype(F32), ada_b.astype(F32))
    mod = mod[:, :b].reshape(depth, b, 3, 3, d)
    mod8 = jnp.concatenate([mod, jnp.zeros((depth, b, 3, 5, d), F32)], axis=3)

    half = ROT_DIM // 2
    inv_freq = jnp.power(ROPE_THETA, -jnp.arange(half, dtype=F32) * (2.0 / ROT_DIM))
    dim = np.arange(LANES) % NSA_HEAD_DIM
    invf_lane = jnp.where(jnp.asarray(dim < ROT_DIM), inv_freq[dim % half], 0.0).reshape(1, LANES)
    cosf, sint = _rope_tables(positions.astype(F32).reshape(b, t, 1), invf_lane, tm=1024)

    n_rows = t // CMP_STRIDE
    ovl = _overlap_matrix(n_rows, t)
    fg = final_norm_g.reshape(1, d).astype(F32)

    w_in_all = ffn_w_in.astype(BF16)
    w_out_all = ffn_w_out.astype(BF16)
    w_mix_all = mix_w_out.astype(BF16)

    gain_all = norm_g.astype(F32).reshape(depth * 3, 1, d)

    for l in range(depth):
        x = _ffn(x, mod8, gain_all, w_in_all, w_out_all, l, 0, fg, final_norm=False)

        (qkv, z, ba, bat, qpad, kc, vc, ks, vs, kw, vw) = _inproj(
            x, mod8, gain_all, l, _prep_inproj_weight(mix_w_in[l]), cosf, sint)

        neg_a = -jnp.exp(gdn_a_log[l].astype(F32))
        dtb = gdn_dt_bias[l].astype(F32)
        avec = jnp.zeros((8, LANES), F32).at[0, GDN_HEADS:2 * GDN_HEADS].set(neg_a)
        avec = avec.at[1, GDN_HEADS:2 * GDN_HEADS].set(dtb)
        avect = jnp.zeros((_SMALL_ROWS, LANES), F32).at[GDN_HEADS:2 * GDN_HEADS, 0].set(neg_a)
        avect = avect.at[GDN_HEADS:2 * GDN_HEADS, 1].set(dtb)
        y_gdn = _gdn(qkv, z, ba, bat, gdn_conv_w[l].astype(F32), avec, avect,
                     gdn_norm_g[l].reshape(1, GDN_HEAD_DIM).astype(F32))

        kcmp = _compress(kc, *_prep_compress(cmp_pe_k[l], cmp_w1_k[l], cmp_w2_k[l]))
        vcmp = _compress(vc, *_prep_compress(cmp_pe_v[l], cmp_w1_v[l], cmp_w2_v[l]))
        ng_lane = jnp.tile(nsa_norm_g[l].astype(F32), NSA_KV_HEADS).reshape(1, LANES)
        y_nsa = _nsa(qpad, kcmp, vcmp, ks, vs, kw, vw, ba, ovl, ng_lane)

        x = _ffn(x, mod8, gain_all, w_in_all, w_out_all, l, 1, fg, final_norm=(l == depth - 1),
                 mix=(y_gdn, y_nsa, w_mix_all))
    return x
```

```python
import functools
import math

import jax
import jax.numpy as jnp
import numpy as np
from jax import lax
from jax.experimental import pallas as pl
from jax.experimental.pallas import tpu as pltpu

F32 = jnp.float32
BF16 = jnp.bfloat16

NORM_EPS = 1e-6
LANES = 128
GDN_HEADS = 4
GDN_HEAD_DIM = 128
GDN_WIDTH = GDN_HEADS * GDN_HEAD_DIM
GDN_CONV = 4
GDN_CHUNK = 64
GDN_PREP_STEP = 4 * GDN_CHUNK
GDN_SCAN_STEP = 2 * GDN_CHUNK
NSA_HEADS = 8
NSA_KV_HEADS = 2
NSA_REP = NSA_HEADS // NSA_KV_HEADS
NSA_HEAD_DIM = 64
NSA_WIDTH = NSA_HEADS * NSA_HEAD_DIM
NSA_KV_WIDTH = NSA_KV_HEADS * NSA_HEAD_DIM
CMP_LEN = 32
CMP_STRIDE = 16
CMP_HIDDEN = 2 * NSA_HEAD_DIM
SEL_BLOCK = 64
N_SELECT = 16
WINDOW = 512
Q_BLOCK = 128
SEL_CHUNK = 1024
SEL_LANES = SEL_CHUNK // SEL_BLOCK


def sel_lane_base(g):
    return NSA_HEAD_DIM if g == 0 else 0
ROPE_THETA = 500000.0
ROT_DIM = NSA_HEAD_DIM // 4
N_ADA = 9
NEG_BIG = -1e30
VMEM_LIMIT = 56 * 1024 * 1024

HIGHEST = lax.Precision.HIGHEST


def _cparams(sem):
    return pltpu.CompilerParams(dimension_semantics=sem, vmem_limit_bytes=VMEM_LIMIT)


def _dot(a, b):
    return jnp.dot(a, b, preferred_element_type=F32)


def _dot_nt(a, b):
    return lax.dot_general(a, b, (((1,), (1,)), ((), ())), preferred_element_type=F32)


def _dot_hi(a, b):
    return jnp.dot(a, b, preferred_element_type=F32, precision=HIGHEST)


def _split3(x):
    hi = x.astype(BF16)
    r1 = x - hi.astype(F32)
    mid = r1.astype(BF16)
    lo = (r1 - mid.astype(F32)).astype(BF16)
    return hi, mid, lo


def _sigmoid(x):
    return 1.0 / (1.0 + jnp.exp(-x))


def _silu(x):
    return x * _sigmoid(x)


def _rms_rows(x, g):
    ms = jnp.mean(x * x, axis=-1, keepdims=True)
    return x * lax.rsqrt(ms + NORM_EPS) * g


def _ada_body(c_ref, w_ref, b_ref, o_ref):
    cond = _silu(c_ref[...])
    hi, mid, lo = [t.astype(F32) for t in _split3(cond)]
    rows = cond.shape[0]
    w = w_ref[0]
    w_hi = w.astype(BF16)
    w_lo = (w - w_hi.astype(F32)).astype(BF16)
    first = _dot(jnp.concatenate([hi, mid, lo, jnp.zeros_like(hi)], axis=0).astype(BF16), w_hi)
    second = _dot(jnp.concatenate([hi, mid], axis=0).astype(BF16), w_lo)
    o_ref[0] = ((first[:rows] + first[rows:2 * rows]) + (first[2 * rows:3 * rows] + second[:rows])
                + second[rows:]) + b_ref[0]


def _ada_mod(c_pad, ada_w, ada_b):
    depth, d, n = ada_w.shape
    tn = n // N_ADA
    rows = c_pad.shape[0]
    return pl.pallas_call(
        _ada_body,
        grid=(depth, n // tn),
        in_specs=[
            pl.BlockSpec((rows, d), lambda l, j: (0, 0)),
            pl.BlockSpec((1, d, tn), lambda l, j: (l, 0, j)),
            pl.BlockSpec((1, 1, tn), lambda l, j: (l, 0, j)),
        ],
        out_specs=pl.BlockSpec((1, rows, tn), lambda l, j: (l, 0, j)),
        out_shape=jax.ShapeDtypeStruct((depth, rows, n), F32),
        compiler_params=_cparams(("arbitrary", "arbitrary")),
        name="ada_mod",
    )(c_pad, ada_w, ada_b.reshape(depth, 1, n))


def _rope_body(pos_ref, invf_ref, cos_ref, sin_ref):
    ang = pos_ref[0] * invf_ref[...]
    cos_ref[0] = jnp.cos(ang)
    sin_ref[0] = jnp.sin(ang)


def _rope_tables(pos_f, invf_lane, tm):
    b, t, _ = pos_f.shape
    spec = pl.BlockSpec((1, tm, LANES), lambda i, j: (i, j, 0))
    return pl.pallas_call(
        _rope_body,
        grid=(b, t // tm),
        in_specs=[pl.BlockSpec((1, tm, 1), lambda i, j: (i, j, 0)),
                  pl.BlockSpec((1, LANES), lambda i, j: (0, 0))],
        out_specs=[spec, spec],
        out_shape=[jax.ShapeDtypeStruct((b, t, LANES), F32)] * 2,
        compiler_params=_cparams(("arbitrary", "arbitrary")),
        name="rope_tables",
    )(pos_f, invf_lane)


def _ffn_body(*refs, final_norm, with_mix, tf):
    if with_mix:
        (x_ref, mod_ref, g_ref, wi_ref, wo_ref, fg_ref,
         mmod_ref, yg_ref, yn_ref, wm_ref, o_ref, act_scr) = refs
        mix = _dot(yg_ref[0], wm_ref[0:GDN_WIDTH, :]) + _dot(yn_ref[0], wm_ref[GDN_WIDTH:, :])
        x = x_ref[0] + mmod_ref[0, 2:3, :] * mix
    else:
        x_ref, mod_ref, g_ref, wi_ref, wo_ref, fg_ref, o_ref, act_scr = refs
        x = x_ref[0]
    y = _rms_rows(x, g_ref[...])
    h = (y * (1.0 + mod_ref[0, 1:2, :]) + mod_ref[0, 0:1, :]).astype(BF16)
    ff = wo_ref.shape[0]
    for j in range(ff // tf):
        a = _dot(h, wi_ref[:, j * tf:(j + 1) * tf])
        u = _dot(h, wi_ref[:, ff + j * tf:ff + (j + 1) * tf])
        act_scr[:, j * tf:(j + 1) * tf] = (_silu(a) * u).astype(BF16)
    out = x + (0.5 * mod_ref[0, 2:3, :]) * _dot(act_scr[...], wo_ref[...])
    if final_norm:
        out = _rms_rows(out, fg_ref[...])
    o_ref[0] = out


def _mod_spec(d, layer, sub):
    return pl.BlockSpec((None, 1, None, 8, d), lambda i, j: (layer, i, sub, 0, 0))


def _gain_spec(d, layer, sub):
    return pl.BlockSpec((None, 1, d), lambda i, j: (layer * 3 + sub, 0, 0))


def _ffn(x, mod_all, gain_all, w_in_all, w_out_all, layer, idx, fg, *, final_norm, mix=None, tm=1024, tf=256):
    b, t, d = x.shape
    ff = w_out_all.shape[2]
    sub = 2 * idx
    row = lambda n: pl.BlockSpec((1, tm, n), lambda i, j: (i, j, 0))
    vec_spec = pl.BlockSpec((1, d), lambda i, j: (0, 0))
    stacked = lambda r, c, *lead: pl.BlockSpec((None,) * len(lead) + (r, c), lambda i, j: lead + (0, 0),
                                               pipeline_mode=pl.Buffered(1))
    in_specs = [row(d), _mod_spec(d, layer, sub), _gain_spec(d, layer, sub),
                stacked(d, 2 * ff, layer, idx), stacked(ff, d, layer, idx), vec_spec]
    args = [x, mod_all, gain_all, w_in_all, w_out_all, fg]
    if mix is not None:
        yg, yn, w_mix_all = mix
        in_specs += [_mod_spec(d, layer, 1), row(GDN_WIDTH), row(NSA_WIDTH), stacked(d, d, layer)]
        args += [mod_all, yg, yn, w_mix_all]
    return pl.pallas_call(
        functools.partial(_ffn_body, final_norm=final_norm, with_mix=mix is not None, tf=tf),
        grid=(b, t // tm),
        in_specs=in_specs,
        out_specs=row(d),
        out_shape=jax.ShapeDtypeStruct((b, t, d), F32),
        scratch_shapes=[pltpu.VMEM((tm, ff), BF16)],
        compiler_params=_cparams(("arbitrary", "arbitrary")),
        name=("ffn_mix" if mix is not None else "ffn") + ("_final" if final_norm else ""),
    )(*args)


_O_QKV, _O_Z, _O_Q = 0, 1536, 2048
_O_KC, _O_VC, _O_KS, _O_VS, _O_KW, _O_VW, _O_SM = 2560, 2688, 2816, 2944, 3072, 3200, 3328
_IN_COLS = 3456
_SMALL_ROWS = 32
_GATE_COPY = NSA_HEAD_DIM + 2 * GDN_HEADS


def _rope_tile(x, cosf, sint, first_half):
    fwd = pltpu.roll(x, 8, axis=1)
    bwd = pltpu.roll(x, LANES - 8, axis=1)
    return x * cosf + jnp.where(first_half, -bwd, fwd) * sint


def _inproj_body(x_ref, mod_ref, g_ref, w_ref, cos_ref, sin_ref,
                 qkv_ref, z_ref, ba_ref, bat_ref, qpad_ref,
                 kc_ref, vc_ref, ks_ref, vs_ref, kw_ref, vw_ref):
    y = _rms_rows(x_ref[0], g_ref[...])
    h = (y * (1.0 + mod_ref[0, 1:2, :]) + mod_ref[0, 0:1, :]).astype(BF16)
    cosf = cos_ref[0]
    sint = sin_ref[0]
    lane = lax.broadcasted_iota(jnp.int32, cosf.shape, 1)
    first_half = (lane % NSA_HEAD_DIM) < (ROT_DIM // 2)
    low = lane < NSA_HEAD_DIM

    for j in range(3):
        qkv_ref[0, :, j * 512:(j + 1) * 512] = _dot(h, w_ref[:, _O_QKV + j * 512:_O_QKV + (j + 1) * 512])
    z_ref[0] = _dot(h, w_ref[:, _O_Z:_O_Z + 512])
    ba = _dot(h, w_ref[:, _O_SM:_O_SM + LANES])
    ba_ref[0] = ba
    bat_ref[0] = ba.T[:_SMALL_ROWS, :]

    scale = NSA_HEAD_DIM ** -0.5 * math.log2(math.e)
    def pair(off):
        res = _dot(h, w_ref[:, off:off + 2 * LANES])
        return res[:, :LANES], res[:, LANES:]

    q_tiles = pair(_O_Q) + pair(_O_Q + 2 * LANES)
    for k in range(NSA_HEADS // 2):
        tile = _rope_tile(q_tiles[k], cosf, sint, first_half) * scale
        swapped = pltpu.roll(tile, NSA_HEAD_DIM, axis=1)
        grp = (2 * k) // NSA_REP
        if grp == 0:
            even = jnp.where(low, tile, 0.0)
            odd = jnp.where(low, swapped, 0.0)
        else:
            even = jnp.where(low, 0.0, swapped)
            odd = jnp.where(low, 0.0, tile)
        qpad_ref[0, 2 * k] = even.astype(BF16)
        qpad_ref[0, 2 * k + 1] = odd.astype(BF16)

    kc, vc = pair(_O_KC)
    kc_ref[0] = _rope_tile(kc, cosf, sint, first_half)
    vc_ref[0] = vc
    ks, vs = pair(_O_KS)
    ks = _rope_tile(ks, cosf, sint, first_half)
    tok = pl.program_id(1) * ks.shape[0] + lax.broadcasted_iota(jnp.int32, ks.shape, 0)
    blk = lax.shift_right_logical(tok, 6) & (SEL_LANES - 1)
    for grp in range(NSA_KV_HEADS):
        onehot = jnp.where(lane - sel_lane_base(grp) == blk, 1.0, 0.0)
        mixed = jnp.where(low, ks, onehot) if grp == 0 else jnp.where(low, onehot, ks)
        ks_ref[0, grp] = mixed.astype(BF16)
    vs_ref[0] = vs.astype(BF16)
    kw, vw = pair(_O_KW)
    kw_ref[0] = _rope_tile(kw, cosf, sint, first_half).astype(BF16)
    vw_ref[0] = vw.astype(BF16)


def _inproj(x, mod_all, gain_all, layer, w, cosf, sint, tm=512):
    b, t, d = x.shape
    row = lambda n: pl.BlockSpec((1, tm, n), lambda i, j: (i, j, 0))
    sds = lambda n, dt: jax.ShapeDtypeStruct((b, t, n), dt)
    return pl.pallas_call(
        _inproj_body,
        grid=(b, t // tm),
        in_specs=[
            row(d),
            _mod_spec(d, layer, 1),
            _gain_spec(d, layer, 1),
            pl.BlockSpec((d, _IN_COLS), lambda i, j: (0, 0)),
            row(LANES), row(LANES),
        ],
        out_specs=[
            row(3 * GDN_WIDTH), row(GDN_WIDTH), row(LANES),
            pl.BlockSpec((1, _SMALL_ROWS, tm), lambda i, j: (i, 0, j)),
            pl.BlockSpec((1, NSA_HEADS, tm, LANES), lambda i, j: (i, 0, j, 0)),
            row(LANES), row(LANES),
            pl.BlockSpec((1, NSA_KV_HEADS, tm, LANES), lambda i, j: (i, 0, j, 0)),
            row(LANES), row(LANES), row(LANES),
        ],
        out_shape=[
            sds(3 * GDN_WIDTH, F32), sds(GDN_WIDTH, F32), sds(LANES, F32),
            jax.ShapeDtypeStruct((b, _SMALL_ROWS, t), F32),
            jax.ShapeDtypeStruct((b, NSA_HEADS, t, LANES), BF16),
            sds(LANES, F32), sds(LANES, F32),
            jax.ShapeDtypeStruct((b, NSA_KV_HEADS, t, LANES), BF16),
            sds(LANES, BF16), sds(LANES, BF16), sds(LANES, BF16),
        ],
        compiler_params=_cparams(("arbitrary", "arbitrary")),
        name="mix_inproj",
    )(x, mod_all, gain_all, w, cosf, sint)


def _softplus(x):
    return jnp.maximum(x, 0.0) + jnp.log1p(jnp.exp(-jnp.abs(x)))


def _dot_inv(a, b):
    ah = a.astype(BF16)
    al = (a - ah.astype(F32)).astype(BF16)
    bh = b.astype(BF16)
    bl = (b - bh.astype(F32)).astype(BF16)
    m = a.shape[0]
    top = _dot(jnp.concatenate([ah, al], axis=0), bh)
    return top[:m] + top[m:] + _dot(ah, bl)


def _unit_lower_inverses(lows, order):
    n = lows[0].shape[0]
    r = lax.broadcasted_iota(jnp.int32, (n, n), 0)
    c = lax.broadcasted_iota(jnp.int32, (n, n), 1)
    eye = jnp.where(r == c, 1.0, 0.0)
    ps = [-low for low in lows]
    invs = [eye + p for p in ps]
    for _ in range(int(math.log2(order)) - 1):
        ps = [_dot_inv(p, p) for p in ps]
        invs = [inv + _dot_inv(inv, p) for inv, p in zip(invs, ps)]
    return invs


def _gdn_prep_body(qkv_ref, ba_ref, bat_ref, cw_ref, av_ref, avt_ref,
                   u_ref, wq_ref, ak_ref, egl_ref, buf):
    step = pl.program_id(1)
    c64 = GDN_CHUNK
    hd = GDN_HEAD_DIM
    nt = GDN_PREP_STEP

    @pl.when(step == 0)
    def _():
        buf[0:8, :] = jnp.zeros((8, 3 * GDN_WIDTH), F32)

    x = qkv_ref[0]
    buf[8:8 + nt, :] = x
    y = x * cw_ref[3:4, :]
    for j in range(GDN_CONV - 1):
        sh = GDN_CONV - 1 - j
        y = y + buf[8 - sh:8 - sh + nt, :] * cw_ref[j:j + 1, :]
    buf[0:8, :] = x[nt - 8:nt, :]
    y = _silu(y)

    ba = ba_ref[0]
    beta_all = _sigmoid(ba)
    g_col_all = av_ref[0:1, :] * _softplus(ba + av_ref[1:2, :])
    bat = bat_ref[0]
    g_row_all = avt_ref[:, 0:1] * _softplus(bat + avt_ref[:, 1:2])

    pw = 2 * c64
    rr = lax.broadcasted_iota(jnp.int32, (pw, pw), 0)
    kk = lax.broadcasted_iota(jnp.int32, (pw, pw), 1)
    same = lax.shift_right_logical(rr, 6) == lax.shift_right_logical(kk, 6)
    causal = same & (rr >= kk)
    strict = same & (rr > kk)
    tri = jnp.where(causal, 1.0, 0.0)
    tri_t = jnp.where(same & (rr <= kk), 1.0, 0.0)
    first = lax.broadcasted_iota(jnp.int32, (pw, 1), 0) < c64

    npair = nt // pw
    units = [(pc, h) for pc in range(npair) for h in range(GDN_HEADS)]
    rows_of = lambda pc: slice(pc * pw, (pc + 1) * pw)
    gc_cols = [_dot_hi(tri, g_col_all[rows_of(pc), :]) for pc in range(npair)]
    gc_rows = [_dot_hi(g_row_all[:, rows_of(pc)], tri_t) for pc in range(npair)]

    def l2n(v):
        return v * lax.rsqrt(jnp.sum(v * v, axis=-1, keepdims=True) + NORM_EPS)

    khs, kbs, decays, lows = [], [], [], []
    for pc, h in units:
        kh = l2n(y[rows_of(pc), GDN_WIDTH + h * hd:GDN_WIDTH + (h + 1) * hd])
        gcol = gc_cols[pc][:, GDN_HEADS + h:GDN_HEADS + h + 1]
        grow = gc_rows[pc][GDN_HEADS + h:GDN_HEADS + h + 1, :]
        decay = jnp.exp(jnp.where(causal, gcol - grow, -jnp.inf))
        kb = kh * beta_all[rows_of(pc), h:h + 1]
        low = jnp.where(strict, _dot_nt(kb.astype(BF16), kh.astype(BF16)) * decay, 0.0)
        khs.append(kh), kbs.append(kb), decays.append(decay), lows.append(low)

    invs = _unit_lower_inverses(lows, c64)

    egl_rows = [[] for _ in range(nt // c64)]
    for i, (pc, h) in enumerate(units):
        rows = rows_of(pc)
        kh, kb, decay = khs[i], kbs[i], decays[i]
        qh = l2n(y[rows, h * hd:(h + 1) * hd]) * (hd ** -0.5)
        vh = y[rows, 2 * GDN_WIDTH + h * hd:2 * GDN_WIDTH + (h + 1) * hd]
        gcol = gc_cols[pc][:, GDN_HEADS + h:GDN_HEADS + h + 1]
        grow = gc_rows[pc][GDN_HEADS + h:GDN_HEADS + h + 1, :]
        glasts = [grow[:, c64 - 1:c64], grow[:, pw - 1:pw]]
        glast = jnp.where(first, glasts[0], glasts[1])
        eg = jnp.exp(gcol)
        rhs = jnp.concatenate([vh * beta_all[rows, h:h + 1], kb * eg], axis=1)
        sol = _dot_inv(invs[i], rhs)
        attn = _dot_nt(qh.astype(BF16), kh.astype(BF16)) * decay
        kd_t = (kh * jnp.exp(glast - gcol)).T
        qe = (qh * eg).astype(BF16)
        u_ref[0, h, rows, :] = sol[:, :hd]
        for half in range(2):
            ch = 2 * pc + half
            part = slice(half * c64, (half + 1) * c64)
            wq_ref[0, h, ch, 0:c64, :] = sol[part, hd:].astype(BF16)
            wq_ref[0, h, ch, c64:2 * c64, :] = qe[part, :]
            ak_ref[0, h, ch, 0:c64, :] = attn[part, part].astype(BF16)
            ak_ref[0, h, ch, c64:c64 + hd, :] = kd_t[:, part].astype(BF16)
            egl_rows[ch].append(jnp.broadcast_to(jnp.exp(glasts[half]), (1, LANES)))
    for ch in range(nt // c64):
        egl_ref[0, ch] = jnp.concatenate(egl_rows[ch] + [jnp.zeros((8 - GDN_HEADS, LANES), F32)], axis=0)


def _gdn_prep(qkv, ba, bat, conv_w, avec, avect):
    b, t, _ = qkv.shape
    nt = GDN_PREP_STEP
    nch = nt // GDN_CHUNK
    hd = GDN_HEAD_DIM
    row = lambda n: pl.BlockSpec((1, nt, n), lambda i, j: (i, j, 0))
    return pl.pallas_call(
        _gdn_prep_body,
        grid=(b, t // nt),
        in_specs=[
            row(3 * GDN_WIDTH), row(LANES),
            pl.BlockSpec((1, _SMALL_ROWS, nt), lambda i, j: (i, 0, j)),
            pl.BlockSpec((GDN_CONV, 3 * GDN_WIDTH), lambda i, j: (0, 0)),
            pl.BlockSpec((8, LANES), lambda i, j: (0, 0)),
            pl.BlockSpec((_SMALL_ROWS, LANES), lambda i, j: (0, 0)),
        ],
        out_specs=[
            pl.BlockSpec((1, GDN_HEADS, nt, hd), lambda i, j: (i, 0, j, 0)),
            pl.BlockSpec((1, GDN_HEADS, nch, 2 * GDN_CHUNK, hd), lambda i, j: (i, 0, j, 0, 0)),
            pl.BlockSpec((1, GDN_HEADS, nch, GDN_CHUNK + hd, GDN_CHUNK), lambda i, j: (i, 0, j, 0, 0)),
            pl.BlockSpec((1, nch, 8, LANES), lambda i, j: (i, j, 0, 0)),
        ],
        out_shape=[
            jax.ShapeDtypeStruct((b, GDN_HEADS, t, hd), F32),
            jax.ShapeDtypeStruct((b, GDN_HEADS, t // GDN_CHUNK, 2 * GDN_CHUNK, hd), BF16),
            jax.ShapeDtypeStruct((b, GDN_HEADS, t // GDN_CHUNK, GDN_CHUNK + hd, GDN_CHUNK), BF16),
            jax.ShapeDtypeStruct((b, t // GDN_CHUNK, 8, LANES), F32),
        ],
        scratch_shapes=[pltpu.VMEM((8 + nt, 3 * GDN_WIDTH), F32)],
        compiler_params=_cparams(("arbitrary", "arbitrary")),
        name="gdn_prep",
    )(qkv, ba, bat, conv_w, avec, avect)


def _gdn_scan_body(u_ref, wq_ref, ak_ref, egl_ref, z_ref, ng_ref, o_ref, s_scr):
    c64 = GDN_CHUNK
    hd = GDN_HEAD_DIM
    nb = u_ref.shape[0]

    @pl.when(pl.program_id(0) == 0)
    def _():
        s_scr[...] = jnp.zeros_like(s_scr)

    chains = [(b, h) for b in range(nb) for h in range(GDN_HEADS)]
    states = [s_scr[b, h] for b, h in chains]
    for ch in range(GDN_SCAN_STEP // c64):
        rows = slice(ch * c64, (ch + 1) * c64)
        r1s = [_dot(wq_ref[b, h, ch], s.astype(BF16)) for (b, h), s in zip(chains, states)]
        vns = [(u_ref[b, h, rows, :] - r1[0:c64]).astype(BF16) for (b, h), r1 in zip(chains, r1s)]
        r2s = [_dot(ak_ref[b, h, ch], vn) for (b, h), vn in zip(chains, vns)]
        states = [s * egl_ref[b, ch, h:h + 1, :] + r2[c64:c64 + hd]
                  for (b, h), s, r2 in zip(chains, states, r2s)]
        for (b, h), r1, r2 in zip(chains, r1s, r2s):
            on = _rms_rows(r1[c64:2 * c64] + r2[0:c64], ng_ref[...])
            zh = z_ref[b, rows, h * hd:(h + 1) * hd]
            o_ref[b, rows, h * hd:(h + 1) * hd] = (on * _silu(zh)).astype(o_ref.dtype)
    for (b, h), s in zip(chains, states):
        s_scr[b, h] = s


def _gdn_scan(u, wq, ak, egl, z, norm_g):
    b, _, t, hd = u.shape
    nt = GDN_SCAN_STEP
    nch = nt // GDN_CHUNK
    return pl.pallas_call(
        _gdn_scan_body,
        grid=(t // nt,),
        in_specs=[
            pl.BlockSpec((b, GDN_HEADS, nt, hd), lambda j: (0, 0, j, 0)),
            pl.BlockSpec((b, GDN_HEADS, nch, 2 * GDN_CHUNK, hd), lambda j: (0, 0, j, 0, 0)),
            pl.BlockSpec((b, GDN_HEADS, nch, GDN_CHUNK + hd, GDN_CHUNK), lambda j: (0, 0, j, 0, 0)),
            pl.BlockSpec((b, nch, 8, LANES), lambda j: (0, j, 0, 0)),
            pl.BlockSpec((b, nt, GDN_WIDTH), lambda j: (0, j, 0)),
            pl.BlockSpec((1, hd), lambda j: (0, 0)),
        ],
        out_specs=pl.BlockSpec((b, nt, GDN_WIDTH), lambda j: (0, j, 0)),
        out_shape=jax.ShapeDtypeStruct((b, t, GDN_WIDTH), BF16),
        scratch_shapes=[pltpu.VMEM((b, GDN_HEADS, hd, hd), F32)],
        compiler_params=_cparams(("arbitrary",)),
        name="gdn_scan",
    )(u, wq, ak, egl, z, norm_g)


def _gdn(qkv, z, ba, bat, conv_w, avec, avect, norm_g):
    u, wq, ak, egl = _gdn_prep(qkv, ba, bat, conv_w, avec, avect)
    return _gdn_scan(u, wq, ak, egl, z, norm_g)


def _cmp_body(x_ref, pea_ref, peb_ref, wa_ref, wb_ref, w2_ref, o_ref):
    n_rows = x_ref.shape[1] // CMP_STRIDE
    x = jnp.concatenate([x_ref[0, pl.ds(l, n_rows, stride=CMP_STRIDE), :] for l in range(CMP_STRIDE)], axis=1)
    a = _dot((x + pea_ref[...]).astype(BF16), wa_ref[...])
    bm = _dot((x + peb_ref[...]).astype(BF16), wb_ref[...])
    n = a.shape[0]
    h1 = a + pltpu.roll(bm, n - 1, axis=0)
    o_ref[0] = _dot(_silu(h1).astype(BF16), w2_ref[...]).astype(o_ref.dtype)


def _compress(x, pea, peb, wa, wb, w2bd):
    b, t, wdt = x.shape
    n = t // CMP_STRIDE
    full = lambda s: pl.BlockSpec(s, lambda i: (0,) * len(s))
    return pl.pallas_call(
        _cmp_body,
        grid=(b,),
        in_specs=[pl.BlockSpec((1, t, wdt), lambda i: (i, 0, 0)),
                  full(pea.shape), full(peb.shape), full(wa.shape), full(wb.shape), full(w2bd.shape)],
        out_specs=pl.BlockSpec((1, n, LANES), lambda i: (i, 0, 0)),
        out_shape=jax.ShapeDtypeStruct((b, n, LANES), BF16),
        compiler_params=_cparams(("arbitrary",)),
        name="nsa_compress",
    )(x, pea, peb, wa, wb, w2bd)


def _nsa_body(q_ref, kcmp_ref, vcmp_ref, ks_ref, vs_ref, kw_ref, vw_ref, gate_ref,
              ovl_ref, ng_ref, o_ref, qbias_scr, *, seq):
    qb = pl.program_id(1)
    s0 = qb * Q_BLOCK
    n_cmp_rows = kcmp_ref.shape[1]
    n_blk = seq // SEL_BLOCK
    rq = NSA_REP * Q_BLOCK

    t_q = s0 + lax.broadcasted_iota(jnp.int32, (Q_BLOCK, 1), 0)
    lane = lax.broadcasted_iota(jnp.int32, (Q_BLOCK, LANES), 1)
    gates = _sigmoid(gate_ref[0])

    n_iota = lax.broadcasted_iota(jnp.int32, (Q_BLOCK, n_cmp_rows), 1)
    cmask = (n_iota * CMP_STRIDE + (CMP_LEN - 1)) <= t_q

    wstart = pl.multiple_of(jnp.maximum(s0 - WINDOW, 0), Q_BLOCK)
    kp = wstart + lax.broadcasted_iota(jnp.int32, (Q_BLOCK, WINDOW + Q_BLOCK), 1)
    wmask = (kp <= t_q) & (kp > t_q - WINDOW)

    cur = lax.shift_right_logical(t_q, 6)
    forced = (lane == 0) | (lane == cur) | (lane == cur - 1)
    valid = (lane * SEL_BLOCK <= t_q) & (lane < n_blk)
    jrow8 = lax.broadcasted_iota(jnp.int32, (8, Q_BLOCK), 0)

    groups_g = range(NSA_KV_HEADS)
    qs = [q_ref[0, g * NSA_REP:(g + 1) * NSA_REP].reshape(rq, LANES) for g in groups_g]
    lmasks = [(lane >= g * NSA_HEAD_DIM) & (lane < (g + 1) * NSA_HEAD_DIM) for g in groups_g]
    kv_lane = lax.broadcasted_iota(jnp.int32, (1, LANES), 1)
    own = [(kv_lane >= g * NSA_HEAD_DIM) & (kv_lane < (g + 1) * NSA_HEAD_DIM) for g in groups_g]

    def with_ones(v, g):
        return jnp.where(own[g], v, jnp.ones_like(v))

    def compressed(g, sc):
        sc = jnp.where(cmask[None], sc.reshape(NSA_REP, Q_BLOCK, n_cmp_rows), -jnp.inf)
        m = jnp.max(sc, axis=-1, keepdims=True)
        m = jnp.where(m == -jnp.inf, 0.0, m)
        p = jnp.exp2(sc - m)
        p = p * (1.0 / jnp.maximum(jnp.sum(p, axis=-1, keepdims=True), 1e-30))
        o_cmp = _dot(p.reshape(rq, n_cmp_rows).astype(BF16), vcmp_ref[0])
        psum = p[0] + p[1] + p[2] + p[3]
        hi, mid, lo = _split3(psum)
        ovl = ovl_ref[...]
        imp = _dot(hi, ovl) + _dot(mid, ovl) + _dot(lo, ovl)
        return o_cmp.reshape(NSA_REP, Q_BLOCK, LANES), imp

    def select_blocks(imp):
        key = jnp.where(forced, jnp.inf, jnp.where(valid, imp, -jnp.inf))
        key_t = key.T[:n_blk, :]
        groups = [key_t[8 * v:8 * v + 8, :] for v in range(n_blk // 8)]
        cnts = [jnp.zeros((8, Q_BLOCK), F32) for _ in groups]
        for i in range(n_blk):
            row = key_t[i:i + 1, :]
            for v, grp in enumerate(groups):
                if v < i // 8:
                    beats = jnp.where(row > grp, 1.0, 0.0)
                elif v > i // 8:
                    beats = jnp.where(row >= grp, 1.0, 0.0)
                else:
                    beats = jnp.where(jrow8 > i % 8, jnp.where(row >= grp, 1.0, 0.0),
                                      jnp.where(row > grp, 1.0, 0.0))
                cnts[v] = cnts[v] + beats
        sel_t = jnp.where(jnp.concatenate(cnts, axis=0) < float(N_SELECT), 1.0, 0.0)
        if n_blk < LANES:
            sel_t = jnp.concatenate([sel_t, jnp.zeros((LANES - n_blk, Q_BLOCK), F32)], axis=0)
        return sel_t.T

    def store_block_bias(g, sel):
        past = lane < 2 * qb
        bias = jnp.where(past, (sel - 1.0) * (-NEG_BIG), NEG_BIG)
        base = sel_lane_base(g)
        here = (lane >= base) & (lane < base + SEL_LANES)
        for c in range(seq // SEL_CHUNK):
            moved = pltpu.roll(bias, (base - c * SEL_LANES) % LANES, axis=1)
            qbias_scr[g, c] = jnp.where(here, moved, 0.0).astype(BF16)

    def sel_scores(c, g):
        k0 = pl.multiple_of(c * SEL_CHUNK, SEL_CHUNK)
        lhs = (q_ref[0, g * NSA_REP:(g + 1) * NSA_REP] + qbias_scr[g, c][None]).reshape(rq, LANES)
        return _dot_nt(lhs, ks_ref[0, g, pl.ds(k0, SEL_CHUNK), :])

    def sel_update(g, c, scores, state):
        k0 = pl.multiple_of(c * SEL_CHUNK, SEL_CHUNK)
        vch = vs_ref[0, pl.ds(k0, SEL_CHUNK), :]
        m_old, acc = state
        s = scores.reshape(NSA_REP, Q_BLOCK, SEL_CHUNK)
        m_new = jnp.maximum(m_old, jnp.max(s, axis=-1, keepdims=True))
        p = jnp.exp2(s - m_new)
        pv = _dot(p.reshape(rq, SEL_CHUNK).astype(BF16), with_ones(vch, g)).reshape(NSA_REP, Q_BLOCK, LANES)
        return m_new, jnp.exp2(m_old - m_new) * acc + pv

    kwin = kw_ref[0, pl.ds(wstart, WINDOW + Q_BLOCK), :]
    vwin = vw_ref[0, pl.ds(wstart, WINDOW + Q_BLOCK), :]
    cmp_scores = [_dot_nt(qs[g], kcmp_ref[0]) for g in groups_g]
    win_scores = [_dot_nt(qs[g], kwin) for g in groups_g]
    sdiag = pl.multiple_of(s0, Q_BLOCK)
    vdiag = vs_ref[0, pl.ds(sdiag, Q_BLOCK), :]
    diag_scores = [_dot_nt(qs[g], ks_ref[0, g, pl.ds(sdiag, Q_BLOCK), :]) for g in groups_g]

    cmp_out = [compressed(g, cmp_scores[g]) for g in groups_g]

    wbias = jnp.where(wmask, 0.0, NEG_BIG)
    sws = [win_scores[g].reshape(NSA_REP, Q_BLOCK, WINDOW + Q_BLOCK) + wbias[None] for g in groups_g]
    pws = [jnp.exp2(sw - jnp.max(sw, axis=-1, keepdims=True)) for sw in sws]
    wins = [_dot(pws[g].reshape(rq, WINDOW + Q_BLOCK).astype(BF16), with_ones(vwin, g)).reshape(NSA_REP, Q_BLOCK, LANES)
            for g in groups_g]

    for g in groups_g:
        store_block_bias(g, select_blocks(cmp_out[g][1]))

    dmask = (lax.broadcasted_iota(jnp.int32, (Q_BLOCK, Q_BLOCK), 1)
             <= lax.broadcasted_iota(jnp.int32, (Q_BLOCK, Q_BLOCK), 0))
    dbias = jnp.where(dmask, 0.0, NEG_BIG)
    states = []
    for g in groups_g:
        s = diag_scores[g].reshape(NSA_REP, Q_BLOCK, Q_BLOCK) + dbias[None]
        m0 = jnp.max(s, axis=-1, keepdims=True)
        p = jnp.exp2(s - m0).reshape(rq, Q_BLOCK).astype(BF16)
        states.append((m0, _dot(p, with_ones(vdiag, g)).reshape(NSA_REP, Q_BLOCK, LANES)))

    def sel_step(c, states):
        scores = [sel_scores(c, g) for g in groups_g]
        return tuple(sel_update(g, c, scores[g], states[g]) for g in groups_g)

    n_past = (qb + SEL_CHUNK // Q_BLOCK - 1) // (SEL_CHUNK // Q_BLOCK)
    sel_out = lax.fori_loop(0, n_past, sel_step, tuple(states))

    heads = [(g, r) for g in groups_g for r in range(NSA_REP)]
    col_of = lambda g, r: (_GATE_COPY if g == 0 else 2 * GDN_HEADS) + (g * NSA_REP + r) * 3
    c_sel = [gates[:, col_of(g, r) + 1:col_of(g, r) + 2]
             * (1.0 / sel_out[g][1][r][:, col_of(g, r) + 1:col_of(g, r) + 2]) for g, r in heads]
    c_win = [gates[:, col_of(g, r) + 2:col_of(g, r) + 3]
             * (1.0 / wins[g][r][:, col_of(g, r) + 2:col_of(g, r) + 3]) for g, r in heads]
    outs = [gates[:, col_of(g, r):col_of(g, r) + 1] * cmp_out[g][0][r] + c_sel[i] * sel_out[g][1][r]
            + c_win[i] * wins[g][r] for i, (g, r) in enumerate(heads)]
    outs = [jnp.where(lmasks[g], o, 0.0) for (g, r), o in zip(heads, outs)]
    mss = [jnp.sum(o * o, axis=-1, keepdims=True) * (1.0 / NSA_HEAD_DIM) for o in outs]
    ys = [o * lax.rsqrt(ms + NORM_EPS) * ng_ref[...] for o, ms in zip(outs, mss)]
    for g in groups_g:
        for pair in range(NSA_REP // 2):
            a, bb = ys[g * NSA_REP + 2 * pair], ys[g * NSA_REP + 2 * pair + 1]
            if g == 0:
                tile = a + pltpu.roll(bb, NSA_HEAD_DIM, axis=1)
            else:
                tile = pltpu.roll(a, NSA_HEAD_DIM, axis=1) + bb
            c0 = g * NSA_REP * NSA_HEAD_DIM + pair * LANES
            o_ref[0, :, c0:c0 + LANES] = tile.astype(o_ref.dtype)


def _nsa(qpad, kcmp, vcmp, ks, vs, kw, vw, ba, ovl, ng_lane):
    b, _, t, _ = qpad.shape
    n = kcmp.shape[1]
    whole = lambda rows: pl.BlockSpec((1, rows, LANES), lambda i, j: (i, 0, 0))
    return pl.pallas_call(
        functools.partial(_nsa_body, seq=t),
        grid=(b, t // Q_BLOCK),
        in_specs=[
            pl.BlockSpec((1, NSA_HEADS, Q_BLOCK, LANES), lambda i, j: (i, 0, j, 0)),
            whole(n), whole(n),
            pl.BlockSpec((1, NSA_KV_HEADS, t, LANES), lambda i, j: (i, 0, 0, 0)),
            whole(t), whole(t), whole(t),
            pl.BlockSpec((1, Q_BLOCK, LANES), lambda i, j: (i, j, 0)),
            pl.BlockSpec((n, LANES), lambda i, j: (0, 0)),
            pl.BlockSpec((1, LANES), lambda i, j: (0, 0)),
        ],
        out_specs=pl.BlockSpec((1, Q_BLOCK, NSA_WIDTH), lambda i, j: (i, j, 0)),
        out_shape=jax.ShapeDtypeStruct((b, t, NSA_WIDTH), BF16),
        scratch_shapes=[pltpu.VMEM((NSA_KV_HEADS, t // SEL_CHUNK, Q_BLOCK, LANES), BF16)],
        compiler_params=_cparams(("arbitrary", "arbitrary")),
        name="nsa_attn",
    )(qpad, kcmp, vcmp, ks, vs, kw, vw, ba, ovl, ng_lane)


_IN_SIZES = (3 * GDN_WIDTH, GDN_WIDTH, GDN_HEADS, GDN_HEADS, NSA_WIDTH) + (NSA_KV_WIDTH,) * 6 + (3 * NSA_HEADS,)


def _prep_inproj_weight(w):
    offs = np.concatenate([[0], np.cumsum(_IN_SIZES)])
    seg = lambda i: w[:, int(offs[i]):int(offs[i + 1])]
    small = jnp.concatenate([seg(2), seg(3), seg(11)], axis=1)
    pad = lambda n: jnp.zeros((w.shape[0], n), w.dtype)
    big = jnp.concatenate([seg(0), seg(1)] + [seg(i) for i in range(4, 11)]
                          + [small, pad(_GATE_COPY - _SMALL_ROWS), seg(11),
                             pad(LANES - _GATE_COPY - 3 * NSA_HEADS)], axis=1)
    return big.astype(BF16)


def _prep_compress(pe, w1, w2):
    hid = CMP_HIDDEN
    g, dh = NSA_KV_HEADS, NSA_HEAD_DIM
    half = CMP_LEN // 2
    w1r = w1.reshape(CMP_LEN, dh, hid)
    eye = jnp.eye(g, dtype=w1.dtype)
    expand = lambda part: jnp.einsum("ldh,gk->lgdkh", part, eye).reshape(half * g * dh, g * hid)
    wa, wb = expand(w1r[:half]), expand(w1r[half:])
    pe_row = lambda part: jnp.broadcast_to(part[:, None, :], (half, g, dh)).reshape(1, half * g * dh)
    pea, peb = pe_row(pe[:half]), pe_row(pe[half:])
    w2bd = jnp.einsum("hd,gk->ghkd", w2, eye).reshape(g * hid, g * dh)
    return pea, peb, wa.astype(BF16), wb.astype(BF16), w2bd.astype(BF16)


def _overlap_matrix(n_rows, seq):
    n_cmp = (seq - CMP_LEN) // CMP_STRIDE + 1
    n_blk = seq // SEL_BLOCK
    n = np.arange(n_rows)[:, None]
    j = np.arange(LANES)[None, :]
    start, end = n * CMP_STRIDE, n * CMP_STRIDE + CMP_LEN - 1
    ovl = (start < j * SEL_BLOCK + SEL_BLOCK) & (end >= j * SEL_BLOCK) & (n < n_cmp) & (j < n_blk)
    return jnp.asarray(ovl.astype(np.float32), dtype=BF16)


def kernel(x, c, positions, ada_w, ada_b, norm_g, ffn_w_in, ffn_w_out, mix_w_in, gdn_conv_w, gdn_a_log, gdn_dt_bias, gdn_norm_g, cmp_pe_k, cmp_w1_k, cmp_w2_k, cmp_pe_v, cmp_w1_v, cmp_w2_v, nsa_norm_g, mix_w_out, final_norm_g):
    b, t, d = x.shape
    depth = ada_w.shape[0]
    assert t % SEL_CHUNK == 0 and t >= WINDOW + Q_BLOCK and b <= 8

    c_pad = jnp.zeros((8, d), F32).at[:b].set(c.astype(F32))
    mod = _ada_mod(c_pad, ada_w.astype(F32), ada_b.astype(F32))
    mod = mod[:, :b].reshape(depth, b, 3, 3, d)
    mod8 = jnp.concatenate([mod, jnp.zeros((depth, b, 3, 5, d), F32)], axis=3)

    half = ROT_DIM // 2
    inv_freq = jnp.power(ROPE_THETA, -jnp.arange(half, dtype=F32) * (2.0 / ROT_DIM))
    dim = np.arange(LANES) % NSA_HEAD_DIM
    invf_lane = jnp.where(jnp.asarray(dim < ROT_DIM), inv_freq[dim % half], 0.0).reshape(1, LANES)
    cosf, sint = _rope_tables(positions.astype(F32).reshape(b, t, 1), invf_lane, tm=1024)

    n_rows = t // CMP_STRIDE
    ovl = _overlap_matrix(n_rows, t)
    fg = final_norm_g.reshape(1, d).astype(F32)

    w_in_all = ffn_w_in.astype(BF16)
    w_out_all = ffn_w_out.astype(BF16)
    w_mix_all = mix_w_out.astype(BF16)

    gain_all = norm_g.astype(F32).reshape(depth * 3, 1, d)

    for l in range(depth):
        x = _ffn(x, mod8, gain_all, w_in_all, w_out_all, l, 0, fg, final_norm=False)

        (qkv, z, ba, bat, qpad, kc, vc, ks, vs, kw, vw) = _inproj(
            x, mod8, gain_all, l, _prep_inproj_weight(mix_w_in[l]), cosf, sint)

        neg_a = -jnp.exp(gdn_a_log[l].astype(F32))
        dtb = gdn_dt_bias[l].astype(F32)
        avec = jnp.zeros((8, LANES), F32).at[0, GDN_HEADS:2 * GDN_HEADS].set(neg_a)
        avec = avec.at[1, GDN_HEADS:2 * GDN_HEADS].set(dtb)
        avect = jnp.zeros((_SMALL_ROWS, LANES), F32).at[GDN_HEADS:2 * GDN_HEADS, 0].set(neg_a)
        avect = avect.at[GDN_HEADS:2 * GDN_HEADS, 1].set(dtb)
        y_gdn = _gdn(qkv, z, ba, bat, gdn_conv_w[l].astype(F32), avec, avect,
                     gdn_norm_g[l].reshape(1, GDN_HEAD_DIM).astype(F32))

        kcmp = _compress(kc, *_prep_compress(cmp_pe_k[l], cmp_w1_k[l], cmp_w2_k[l]))
        vcmp = _compress(vc, *_prep_compress(cmp_pe_v[l], cmp_w1_v[l], cmp_w2_v[l]))
        ng_lane = jnp.tile(nsa_norm_g[l].astype(F32), NSA_KV_HEADS).reshape(1, LANES)
        y_nsa = _nsa(qpad, kcmp, vcmp, ks, vs, kw, vw, ba, ovl, ng_lane)

        x = _ffn(x, mod8, gain_all, w_in_all, w_out_all, l, 1, fg, final_norm=(l == depth - 1),
                 mix=(y_gdn, y_nsa, w_mix_all))
    return x
```

```python
import functools
import math

import jax
import jax.numpy as jnp
import numpy as np
from jax import lax
from jax.experimental import pallas as pl
from jax.experimental.pallas import tpu as pltpu

F32 = jnp.float32
BF16 = jnp.bfloat16

NORM_EPS = 1e-6
LANES = 128
GDN_HEADS = 4
GDN_HEAD_DIM = 128
GDN_WIDTH = GDN_HEADS * GDN_HEAD_DIM
GDN_CONV = 4
GDN_CHUNK = 64
GDN_PREP_STEP = 4 * GDN_CHUNK
GDN_SCAN_STEP = 2 * GDN_CHUNK
NSA_HEADS = 8
NSA_KV_HEADS = 2
NSA_REP = NSA_HEADS // NSA_KV_HEADS
NSA_HEAD_DIM = 64
NSA_WIDTH = NSA_HEADS * NSA_HEAD_DIM
NSA_KV_WIDTH = NSA_KV_HEADS * NSA_HEAD_DIM
CMP_LEN = 32
CMP_STRIDE = 16
CMP_HIDDEN = 2 * NSA_HEAD_DIM
SEL_BLOCK = 64
N_SELECT = 16
WINDOW = 512
Q_BLOCK = 128
SEL_CHUNK = 1024
SEL_LANES = SEL_CHUNK // SEL_BLOCK


def sel_lane_base(g):
    return NSA_HEAD_DIM if g == 0 else 0
ROPE_THETA = 500000.0
ROT_DIM = NSA_HEAD_DIM // 4
N_ADA = 9
NEG_BIG = -1e30
VMEM_LIMIT = 56 * 1024 * 1024


def _cparams(sem):
    return pltpu.CompilerParams(dimension_semantics=sem, vmem_limit_bytes=VMEM_LIMIT)


def _dot(a, b):
    return jnp.dot(a, b, preferred_element_type=F32)


def _dot_nt(a, b):
    return lax.dot_general(a, b, (((1,), (1,)), ((), ())), preferred_element_type=F32)


def _split3(x):
    hi = x.astype(BF16)
    r1 = x - hi.astype(F32)
    mid = r1.astype(BF16)
    lo = (r1 - mid.astype(F32)).astype(BF16)
    return hi, mid, lo


def _sigmoid(x):
    return 1.0 / (1.0 + jnp.exp(-x))


def _silu(x):
    return x * _sigmoid(x)


def _rms_rows(x, g):
    ms = jnp.mean(x * x, axis=-1, keepdims=True)
    return x * lax.rsqrt(ms + NORM_EPS) * g


def _ada_body(c_ref, w_ref, b_ref, o_ref):
    cond = _silu(c_ref[...])
    hi, mid, lo = [t.astype(F32) for t in _split3(cond)]
    rows = cond.shape[0]
    w = w_ref[0]
    w_hi = w.astype(BF16)
    w_lo = (w - w_hi.astype(F32)).astype(BF16)
    first = _dot(jnp.concatenate([hi, mid, lo, jnp.zeros_like(hi)], axis=0).astype(BF16), w_hi)
    second = _dot(jnp.concatenate([hi, mid], axis=0).astype(BF16), w_lo)
    o_ref[0] = ((first[:rows] + first[rows:2 * rows]) + (first[2 * rows:3 * rows] + second[:rows])
                + second[rows:]) + b_ref[0]


def _ada_mod(c_pad, ada_w, ada_b):
    depth, d, n = ada_w.shape
    tn = n // N_ADA
    rows = c_pad.shape[0]
    return pl.pallas_call(
        _ada_body,
        grid=(depth, n // tn),
        in_specs=[
            pl.BlockSpec((rows, d), lambda l, j: (0, 0)),
            pl.BlockSpec((1, d, tn), lambda l, j: (l, 0, j)),
            pl.BlockSpec((1, 1, tn), lambda l, j: (l, 0, j)),
        ],
        out_specs=pl.BlockSpec((1, rows, tn), lambda l, j: (l, 0, j)),
        out_shape=jax.ShapeDtypeStruct((depth, rows, n), F32),
        compiler_params=_cparams(("arbitrary", "arbitrary")),
        name="ada_mod",
    )(c_pad, ada_w, ada_b.reshape(depth, 1, n))


def _rope_body(pos_ref, invf_ref, cos_ref, sin_ref):
    ang = pos_ref[0] * invf_ref[...]
    cos_ref[0] = jnp.cos(ang)
    sin_ref[0] = jnp.sin(ang)


def _rope_tables(pos_f, invf_lane, tm):
    b, t, _ = pos_f.shape
    spec = pl.BlockSpec((1, tm, LANES), lambda i, j: (i, j, 0))
    return pl.pallas_call(
        _rope_body,
        grid=(b, t // tm),
        in_specs=[pl.BlockSpec((1, tm, 1), lambda i, j: (i, j, 0)),
                  pl.BlockSpec((1, LANES), lambda i, j: (0, 0))],
        out_specs=[spec, spec],
        out_shape=[jax.ShapeDtypeStruct((b, t, LANES), F32)] * 2,
        compiler_params=_cparams(("arbitrary", "arbitrary")),
        name="rope_tables",
    )(pos_f, invf_lane)


def _ffn_body(*refs, final_norm, with_mix, tf):
    if with_mix:
        (x_ref, mod_ref, g_ref, wi_ref, wo_ref, fg_ref,
         mmod_ref, yg_ref, yn_ref, wm_ref, o_ref, act_scr) = refs
        mix = _dot(yg_ref[0], wm_ref[0:GDN_WIDTH, :]) + _dot(yn_ref[0], wm_ref[GDN_WIDTH:, :])
        x = x_ref[0] + mmod_ref[0, 2:3, :] * mix
    else:
        x_ref, mod_ref, g_ref, wi_ref, wo_ref, fg_ref, o_ref, act_scr = refs
        x = x_ref[0]
    y = _rms_rows(x, g_ref[...])
    h = (y * (1.0 + mod_ref[0, 1:2, :]) + mod_ref[0, 0:1, :]).astype(BF16)
    ff = wo_ref.shape[0]
    for j in range(ff // tf):
        a = _dot(h, wi_ref[:, j * tf:(j + 1) * tf])
        u = _dot(h, wi_ref[:, ff + j * tf:ff + (j + 1) * tf])
        act_scr[:, j * tf:(j + 1) * tf] = (_silu(a) * u).astype(BF16)
    out = x + (0.5 * mod_ref[0, 2:3, :]) * _dot(act_scr[...], wo_ref[...])
    if final_norm:
        out = _rms_rows(out, fg_ref[...])
    o_ref[0] = out


def _mod_spec(d, layer, sub):
    return pl.BlockSpec((None, 1, None, 8, d), lambda i, j: (layer, i, sub, 0, 0))


def _gain_spec(d, layer, sub):
    return pl.BlockSpec((None, 1, d), lambda i, j: (layer * 3 + sub, 0, 0))


def _ffn(x, mod_all, gain_all, w_in_all, w_out_all, layer, idx, fg, *, final_norm, mix=None, tm=1024, tf=256):
    b, t, d = x.shape
    ff = w_out_all.shape[2]
    sub = 2 * idx
    row = lambda n: pl.BlockSpec((1, tm, n), lambda i, j: (i, j, 0))
    vec_spec = pl.BlockSpec((1, d), lambda i, j: (0, 0))
    stacked = lambda r, c, *lead: pl.BlockSpec((None,) * len(lead) + (r, c), lambda i, j: lead + (0, 0),
                                               pipeline_mode=pl.Buffered(1))
    in_specs = [row(d), _mod_spec(d, layer, sub), _gain_spec(d, layer, sub),
                stacked(d, 2 * ff, layer, idx), stacked(ff, d, layer, idx), vec_spec]
    args = [x, mod_all, gain_all, w_in_all, w_out_all, fg]
    if mix is not None:
        yg, yn, w_mix_all = mix
        in_specs += [_mod_spec(d, layer, 1), row(GDN_WIDTH), row(NSA_WIDTH), stacked(d, d, layer)]
        args += [mod_all, yg, yn, w_mix_all]
    return pl.pallas_call(
        functools.partial(_ffn_body, final_norm=final_norm, with_mix=mix is not None, tf=tf),
        grid=(b, t // tm),
        in_specs=in_specs,
        out_specs=row(d),
        out_shape=jax.ShapeDtypeStruct((b, t, d), F32),
        scratch_shapes=[pltpu.VMEM((tm, ff), BF16)],
        compiler_params=_cparams(("arbitrary", "arbitrary")),
        name=("ffn_mix" if mix is not None else "ffn") + ("_final" if final_norm else ""),
    )(*args)


_O_QKV, _O_Z, _O_Q = 0, 1536, 2048
_O_KC, _O_VC, _O_KS, _O_VS, _O_KW, _O_VW, _O_SM = 2560, 2688, 2816, 2944, 3072, 3200, 3328
_IN_COLS = 3456
_SMALL_ROWS = 32
_GATE_COPY = NSA_HEAD_DIM + 2 * GDN_HEADS


def _rope_tile(x, cosf, sint, first_half):
    fwd = pltpu.roll(x, 8, axis=1)
    bwd = pltpu.roll(x, LANES - 8, axis=1)
    return x * cosf + jnp.where(first_half, -bwd, fwd) * sint


def _inproj_body(x_ref, mod_ref, g_ref, w_ref, cos_ref, sin_ref,
                 qkv_ref, z_ref, ba_ref, bat_ref, qpad_ref,
                 kc_ref, vc_ref, ks_ref, vs_ref, kw_ref, vw_ref):
    y = _rms_rows(x_ref[0], g_ref[...])
    h = (y * (1.0 + mod_ref[0, 1:2, :]) + mod_ref[0, 0:1, :]).astype(BF16)
    cosf = cos_ref[0]
    sint = sin_ref[0]
    lane = lax.broadcasted_iota(jnp.int32, cosf.shape, 1)
    first_half = (lane % NSA_HEAD_DIM) < (ROT_DIM // 2)
    low = lane < NSA_HEAD_DIM

    for j in range(3):
        qkv_ref[0, :, j * 512:(j + 1) * 512] = _dot(h, w_ref[:, _O_QKV + j * 512:_O_QKV + (j + 1) * 512])
    z_ref[0] = _dot(h, w_ref[:, _O_Z:_O_Z + 512])
    ba = _dot(h, w_ref[:, _O_SM:_O_SM + LANES])
    ba_ref[0] = ba
    bat_ref[0] = ba.T[:_SMALL_ROWS, :]

    scale = NSA_HEAD_DIM ** -0.5 * math.log2(math.e)
    def pair(off):
        res = _dot(h, w_ref[:, off:off + 2 * LANES])
        return res[:, :LANES], res[:, LANES:]

    q_tiles = pair(_O_Q) + pair(_O_Q + 2 * LANES)
    for k in range(NSA_HEADS // 2):
        tile = _rope_tile(q_tiles[k], cosf, sint, first_half) * scale
        swapped = pltpu.roll(tile, NSA_HEAD_DIM, axis=1)
        grp = (2 * k) // NSA_REP
        if grp == 0:
            even = jnp.where(low, tile, 0.0)
            odd = jnp.where(low, swapped, 0.0)
        else:
            even = jnp.where(low, 0.0, swapped)
            odd = jnp.where(low, 0.0, tile)
        qpad_ref[0, 2 * k] = even.astype(BF16)
        qpad_ref[0, 2 * k + 1] = odd.astype(BF16)

    kc, vc = pair(_O_KC)
    kc_ref[0] = _rope_tile(kc, cosf, sint, first_half)
    vc_ref[0] = vc
    ks, vs = pair(_O_KS)
    ks = _rope_tile(ks, cosf, sint, first_half)
    tok = pl.program_id(1) * ks.shape[0] + lax.broadcasted_iota(jnp.int32, ks.shape, 0)
    blk = lax.shift_right_logical(tok, 6) & (SEL_LANES - 1)
    for grp in range(NSA_KV_HEADS):
        onehot = jnp.where(lane - sel_lane_base(grp) == blk, 1.0, 0.0)
        mixed = jnp.where(low, ks, onehot) if grp == 0 else jnp.where(low, onehot, ks)
        ks_ref[0, grp] = mixed.astype(BF16)
    vs_ref[0] = vs.astype(BF16)
    kw, vw = pair(_O_KW)
    kw_ref[0] = _rope_tile(kw, cosf, sint, first_half).astype(BF16)
    vw_ref[0] = vw.astype(BF16)


def _inproj(x, mod_all, gain_all, layer, w, cosf, sint, tm=512):
    b, t, d = x.shape
    row = lambda n: pl.BlockSpec((1, tm, n), lambda i, j: (i, j, 0))
    sds = lambda n, dt: jax.ShapeDtypeStruct((b, t, n), dt)
    return pl.pallas_call(
        _inproj_body,
        grid=(b, t // tm),
        in_specs=[
            row(d),
            _mod_spec(d, layer, 1),
            _gain_spec(d, layer, 1),
            pl.BlockSpec((d, _IN_COLS), lambda i, j: (0, 0)),
            row(LANES), row(LANES),
        ],
        out_specs=[
            row(3 * GDN_WIDTH), row(GDN_WIDTH), row(LANES),
            pl.BlockSpec((1, _SMALL_ROWS, tm), lambda i, j: (i, 0, j)),
            pl.BlockSpec((1, NSA_HEADS, tm, LANES), lambda i, j: (i, 0, j, 0)),
            row(LANES), row(LANES),
            pl.BlockSpec((1, NSA_KV_HEADS, tm, LANES), lambda i, j: (i, 0, j, 0)),
            row(LANES), row(LANES), row(LANES),
        ],
        out_shape=[
            sds(3 * GDN_WIDTH, F32), sds(GDN_WIDTH, F32), sds(LANES, F32),
            jax.ShapeDtypeStruct((b, _SMALL_ROWS, t), F32),
            jax.ShapeDtypeStruct((b, NSA_HEADS, t, LANES), BF16),
            sds(LANES, F32), sds(LANES, F32),
            jax.ShapeDtypeStruct((b, NSA_KV_HEADS, t, LANES), BF16),
            sds(LANES, BF16), sds(LANES, BF16), sds(LANES, BF16),
        ],
        compiler_params=_cparams(("arbitrary", "arbitrary")),
        name="mix_inproj",
    )(x, mod_all, gain_all, w, cosf, sint)


def _softplus(x):
    return jnp.maximum(x, 0.0) + jnp.log1p(jnp.exp(-jnp.abs(x)))


def _dot_inv(a, b):
    ah = a.astype(BF16)
    al = (a - ah.astype(F32)).astype(BF16)
    bh = b.astype(BF16)
    bl = (b - bh.astype(F32)).astype(BF16)
    m = a.shape[0]
    top = _dot(jnp.concatenate([ah, al], axis=0), bh)
    return top[:m] + top[m:] + _dot(ah, bl)


def _unit_lower_inverses(lows, order):
    n = lows[0].shape[0]
    r = lax.broadcasted_iota(jnp.int32, (n, n), 0)
    c = lax.broadcasted_iota(jnp.int32, (n, n), 1)
    eye = jnp.where(r == c, 1.0, 0.0)
    ps = [-low for low in lows]
    invs = [eye + p for p in ps]
    for _ in range(int(math.log2(order)) - 1):
        ps = [_dot_inv(p, p) for p in ps]
        invs = [inv + _dot_inv(inv, p) for inv, p in zip(invs, ps)]
    return invs


def _gdn_prep_body(qkv_ref, ba_ref, bat_ref, cw_ref, av_ref, avt_ref,
                   u_ref, wq_ref, ak_ref, egl_ref, buf):
    step = pl.program_id(1)
    c64 = GDN_CHUNK
    hd = GDN_HEAD_DIM
    nt = GDN_PREP_STEP

    @pl.when(step == 0)
    def _():
        buf[0:8, :] = jnp.zeros((8, 3 * GDN_WIDTH), F32)

    x = qkv_ref[0]
    buf[8:8 + nt, :] = x
    y = x * cw_ref[3:4, :]
    for j in range(GDN_CONV - 1):
        sh = GDN_CONV - 1 - j
        y = y + buf[8 - sh:8 - sh + nt, :] * cw_ref[j:j + 1, :]
    buf[0:8, :] = x[nt - 8:nt, :]
    y = _silu(y)

    ba = ba_ref[0]
    beta_all = _sigmoid(ba)
    g_col_all = av_ref[0:1, :] * _softplus(ba + av_ref[1:2, :])
    bat = bat_ref[0]
    g_row_all = avt_ref[:, 0:1] * _softplus(bat + avt_ref[:, 1:2])

    pw = 2 * c64
    rr = lax.broadcasted_iota(jnp.int32, (pw, pw), 0)
    kk = lax.broadcasted_iota(jnp.int32, (pw, pw), 1)
    same = lax.shift_right_logical(rr, 6) == lax.shift_right_logical(kk, 6)
    causal = same & (rr >= kk)
    strict = same & (rr > kk)
    tri = jnp.where(causal, 1.0, 0.0)
    tri_t = jnp.where(same & (rr <= kk), 1.0, 0.0)
    first = lax.broadcasted_iota(jnp.int32, (pw, 1), 0) < c64

    npair = nt // pw
    units = [(pc, h) for pc in range(npair) for h in range(GDN_HEADS)]
    rows_of = lambda pc: slice(pc * pw, (pc + 1) * pw)
    tri_b, tri_tb = tri.astype(BF16), tri_t.astype(BF16)
    gc_cols = [sum(_dot(tri_b, part) for part in _split3(g_col_all[rows_of(pc), :]))
               for pc in range(npair)]
    gc_rows = [sum(_dot(part, tri_tb) for part in _split3(g_row_all[:, rows_of(pc)]))
               for pc in range(npair)]

    def l2n(v):
        return v * lax.rsqrt(jnp.sum(v * v, axis=-1, keepdims=True) + NORM_EPS)

    khs, kbs, decays, lows = [], [], [], []
    for pc, h in units:
        kh = l2n(y[rows_of(pc), GDN_WIDTH + h * hd:GDN_WIDTH + (h + 1) * hd])
        gcol = gc_cols[pc][:, GDN_HEADS + h:GDN_HEADS + h + 1]
        grow = gc_rows[pc][GDN_HEADS + h:GDN_HEADS + h + 1, :]
        decay = jnp.exp(jnp.where(causal, gcol - grow, -jnp.inf))
        kb = kh * beta_all[rows_of(pc), h:h + 1]
        low = jnp.where(strict, _dot_nt(kb.astype(BF16), kh.astype(BF16)) * decay, 0.0)
        khs.append(kh), kbs.append(kb), decays.append(decay), lows.append(low)

    invs = _unit_lower_inverses(lows, c64)

    egl_rows = [[] for _ in range(nt // c64)]
    for i, (pc, h) in enumerate(units):
        rows = rows_of(pc)
        kh, kb, decay = khs[i], kbs[i], decays[i]
        qh = l2n(y[rows, h * hd:(h + 1) * hd]) * (hd ** -0.5)
        vh = y[rows, 2 * GDN_WIDTH + h * hd:2 * GDN_WIDTH + (h + 1) * hd]
        gcol = gc_cols[pc][:, GDN_HEADS + h:GDN_HEADS + h + 1]
        grow = gc_rows[pc][GDN_HEADS + h:GDN_HEADS + h + 1, :]
        glasts = [grow[:, c64 - 1:c64], grow[:, pw - 1:pw]]
        glast = jnp.where(first, glasts[0], glasts[1])
        eg = jnp.exp(gcol)
        rhs = jnp.concatenate([vh * beta_all[rows, h:h + 1], kb * eg], axis=1)
        sol = _dot_inv(invs[i], rhs)
        attn = _dot_nt(qh.astype(BF16), kh.astype(BF16)) * decay
        kd_t = (kh * jnp.exp(glast - gcol)).T
        qe = (qh * eg).astype(BF16)
        u_ref[0, h, rows, :] = sol[:, :hd]
        for half in range(2):
            ch = 2 * pc + half
            part = slice(half * c64, (half + 1) * c64)
            wq_ref[0, h, ch, 0:c64, :] = sol[part, hd:].astype(BF16)
            wq_ref[0, h, ch, c64:2 * c64, :] = qe[part, :]
            ak_ref[0, h, ch, 0:c64, :] = attn[part, part].astype(BF16)
            ak_ref[0, h, ch, c64:c64 + hd, :] = kd_t[:, part].astype(BF16)
            egl_rows[ch].append(jnp.broadcast_to(jnp.exp(glasts[half]), (1, LANES)))
    for ch in range(nt // c64):
        egl_ref[0, ch] = jnp.concatenate(egl_rows[ch] + [jnp.zeros((8 - GDN_HEADS, LANES), F32)], axis=0)


def _gdn_prep(qkv, ba, bat, conv_w, avec, avect):
    b, t, _ = qkv.shape
    nt = GDN_PREP_STEP
    nch = nt // GDN_CHUNK
    hd = GDN_HEAD_DIM
    row = lambda n: pl.BlockSpec((1, nt, n), lambda i, j: (i, j, 0))
    return pl.pallas_call(
        _gdn_prep_body,
        grid=(b, t // nt),
        in_specs=[
            row(3 * GDN_WIDTH), row(LANES),
            pl.BlockSpec((1, _SMALL_ROWS, nt), lambda i, j: (i, 0, j)),
            pl.BlockSpec((GDN_CONV, 3 * GDN_WIDTH), lambda i, j: (0, 0)),
            pl.BlockSpec((8, LANES), lambda i, j: (0, 0)),
            pl.BlockSpec((_SMALL_ROWS, LANES), lambda i, j: (0, 0)),
        ],
        out_specs=[
            pl.BlockSpec((1, GDN_HEADS, nt, hd), lambda i, j: (i, 0, j, 0)),
            pl.BlockSpec((1, GDN_HEADS, nch, 2 * GDN_CHUNK, hd), lambda i, j: (i, 0, j, 0, 0)),
            pl.BlockSpec((1, GDN_HEADS, nch, GDN_CHUNK + hd, GDN_CHUNK), lambda i, j: (i, 0, j, 0, 0)),
            pl.BlockSpec((1, nch, 8, LANES), lambda i, j: (i, j, 0, 0)),
        ],
        out_shape=[
            jax.ShapeDtypeStruct((b, GDN_HEADS, t, hd), F32),
            jax.ShapeDtypeStruct((b, GDN_HEADS, t // GDN_CHUNK, 2 * GDN_CHUNK, hd), BF16),
            jax.ShapeDtypeStruct((b, GDN_HEADS, t // GDN_CHUNK, GDN_CHUNK + hd, GDN_CHUNK), BF16),
            jax.ShapeDtypeStruct((b, t // GDN_CHUNK, 8, LANES), F32),
        ],
        scratch_shapes=[pltpu.VMEM((8 + nt, 3 * GDN_WIDTH), F32)],
        compiler_params=_cparams(("arbitrary", "arbitrary")),
        name="gdn_prep",
    )(qkv, ba, bat, conv_w, avec, avect)


def _gdn_scan_body(u_ref, wq_ref, ak_ref, egl_ref, z_ref, ng_ref, o_ref, s_scr):
    c64 = GDN_CHUNK
    hd = GDN_HEAD_DIM
    nb = u_ref.shape[0]

    @pl.when(pl.program_id(0) == 0)
    def _():
        s_scr[...] = jnp.zeros_like(s_scr)

    chains = [(b, h) for b in range(nb) for h in range(GDN_HEADS)]
    states = [s_scr[b, h] for b, h in chains]
    for ch in range(GDN_SCAN_STEP // c64):
        rows = slice(ch * c64, (ch + 1) * c64)
        r1s = [_dot(wq_ref[b, h, ch], s.astype(BF16)) for (b, h), s in zip(chains, states)]
        vns = [(u_ref[b, h, rows, :] - r1[0:c64]).astype(BF16) for (b, h), r1 in zip(chains, r1s)]
        r2s = [_dot(ak_ref[b, h, ch], vn) for (b, h), vn in zip(chains, vns)]
        states = [s * egl_ref[b, ch, h:h + 1, :] + r2[c64:c64 + hd]
                  for (b, h), s, r2 in zip(chains, states, r2s)]
        for (b, h), r1, r2 in zip(chains, r1s, r2s):
            on = _rms_rows(r1[c64:2 * c64] + r2[0:c64], ng_ref[...])
            zh = z_ref[b, rows, h * hd:(h + 1) * hd]
            o_ref[b, rows, h * hd:(h + 1) * hd] = (on * _silu(zh)).astype(o_ref.dtype)
    for (b, h), s in zip(chains, states):
        s_scr[b, h] = s


def _gdn_scan(u, wq, ak, egl, z, norm_g):
    b, _, t, hd = u.shape
    nt = GDN_SCAN_STEP
    nch = nt // GDN_CHUNK
    return pl.pallas_call(
        _gdn_scan_body,
        grid=(t // nt,),
        in_specs=[
            pl.BlockSpec((b, GDN_HEADS, nt, hd), lambda j: (0, 0, j, 0)),
            pl.BlockSpec((b, GDN_HEADS, nch, 2 * GDN_CHUNK, hd), lambda j: (0, 0, j, 0, 0)),
            pl.BlockSpec((b, GDN_HEADS, nch, GDN_CHUNK + hd, GDN_CHUNK), lambda j: (0, 0, j, 0, 0)),
            pl.BlockSpec((b, nch, 8, LANES), lambda j: (0, j, 0, 0)),
            pl.BlockSpec((b, nt, GDN_WIDTH), lambda j: (0, j, 0)),
            pl.BlockSpec((1, hd), lambda j: (0, 0)),
        ],
        out_specs=pl.BlockSpec((b, nt, GDN_WIDTH), lambda j: (0, j, 0)),
        out_shape=jax.ShapeDtypeStruct((b, t, GDN_WIDTH), BF16),
        scratch_shapes=[pltpu.VMEM((b, GDN_HEADS, hd, hd), F32)],
        compiler_params=_cparams(("arbitrary",)),
        name="gdn_scan",
    )(u, wq, ak, egl, z, norm_g)


def _gdn(qkv, z, ba, bat, conv_w, avec, avect, norm_g):
    u, wq, ak, egl = _gdn_prep(qkv, ba, bat, conv_w, avec, avect)
    return _gdn_scan(u, wq, ak, egl, z, norm_g)


def _cmp_body(x_ref, pea_ref, peb_ref, wa_ref, wb_ref, w2_ref, o_ref):
    n_rows = x_ref.shape[1] // CMP_STRIDE
    x = jnp.concatenate([x_ref[0, pl.ds(l, n_rows, stride=CMP_STRIDE), :] for l in range(CMP_STRIDE)], axis=1)
    a = _dot((x + pea_ref[...]).astype(BF16), wa_ref[...])
    bm = _dot((x + peb_ref[...]).astype(BF16), wb_ref[...])
    n = a.shape[0]
    h1 = a + pltpu.roll(bm, n - 1, axis=0)
    o_ref[0] = _dot(_silu(h1).astype(BF16), w2_ref[...]).astype(o_ref.dtype)


def _compress(x, pea, peb, wa, wb, w2bd):
    b, t, wdt = x.shape
    n = t // CMP_STRIDE
    full = lambda s: pl.BlockSpec(s, lambda i: (0,) * len(s))
    return pl.pallas_call(
        _cmp_body,
        grid=(b,),
        in_specs=[pl.BlockSpec((1, t, wdt), lambda i: (i, 0, 0)),
                  full(pea.shape), full(peb.shape), full(wa.shape), full(wb.shape), full(w2bd.shape)],
        out_specs=pl.BlockSpec((1, n, LANES), lambda i: (i, 0, 0)),
        out_shape=jax.ShapeDtypeStruct((b, n, LANES), BF16),
        compiler_params=_cparams(("arbitrary",)),
        name="nsa_compress",
    )(x, pea, peb, wa, wb, w2bd)


def _nsa_body(q_ref, kcmp_ref, vcmp_ref, ks_ref, vs_ref, kw_ref, vw_ref, gate_ref,
              ovl_ref, ng_ref, o_ref, qbias_scr, *, seq):
    qb = pl.program_id(1)
    s0 = qb * Q_BLOCK
    n_cmp_rows = kcmp_ref.shape[1]
    n_blk = seq // SEL_BLOCK
    rq = NSA_REP * Q_BLOCK

    t_q = s0 + lax.broadcasted_iota(jnp.int32, (Q_BLOCK, 1), 0)
    lane = lax.broadcasted_iota(jnp.int32, (Q_BLOCK, LANES), 1)
    gates = _sigmoid(gate_ref[0])

    n_iota = lax.broadcasted_iota(jnp.int32, (Q_BLOCK, n_cmp_rows), 1)
    cmask = (n_iota * CMP_STRIDE + (CMP_LEN - 1)) <= t_q

    wstart = pl.multiple_of(jnp.maximum(s0 - WINDOW, 0), Q_BLOCK)
    kp = wstart + lax.broadcasted_iota(jnp.int32, (Q_BLOCK, WINDOW + Q_BLOCK), 1)
    wmask = (kp <= t_q) & (kp > t_q - WINDOW)

    cur = lax.shift_right_logical(t_q, 6)
    forced = (lane == 0) | (lane == cur) | (lane == cur - 1)
    valid = (lane * SEL_BLOCK <= t_q) & (lane < n_blk)
    jrow8 = lax.broadcasted_iota(jnp.int32, (8, Q_BLOCK), 0)

    groups_g = range(NSA_KV_HEADS)
    qs = [q_ref[0, g * NSA_REP:(g + 1) * NSA_REP].reshape(rq, LANES) for g in groups_g]
    lmasks = [(lane >= g * NSA_HEAD_DIM) & (lane < (g + 1) * NSA_HEAD_DIM) for g in groups_g]
    kv_lane = lax.broadcasted_iota(jnp.int32, (1, LANES), 1)
    own = [(kv_lane >= g * NSA_HEAD_DIM) & (kv_lane < (g + 1) * NSA_HEAD_DIM) for g in groups_g]

    def with_ones(v, g):
        return jnp.where(own[g], v, jnp.ones_like(v))

    col_of = lambda g, r: (_GATE_COPY if g == 0 else 2 * GDN_HEADS) + (g * NSA_REP + r) * 3

    def compressed(g, sc):
        sc = jnp.where(cmask[None], sc.reshape(NSA_REP, Q_BLOCK, n_cmp_rows), -jnp.inf)
        m = jnp.max(sc, axis=-1, keepdims=True)
        m = jnp.where(m == -jnp.inf, 0.0, m)
        p = jnp.exp2(sc - m)
        inv = 1.0 / jnp.maximum(jnp.sum(p, axis=-1, keepdims=True), 1e-30)
        gate = jnp.stack([gates[:, col_of(g, r):col_of(g, r) + 1] for r in range(NSA_REP)], axis=0)
        o_cmp = _dot((p * (inv * gate)).reshape(rq, n_cmp_rows).astype(BF16), vcmp_ref[0])
        p = p * inv
        psum = p[0] + p[1] + p[2] + p[3]
        hi, mid, lo = _split3(psum)
        ovl = ovl_ref[...]
        imp = _dot(hi, ovl) + _dot(mid, ovl) + _dot(lo, ovl)
        return o_cmp.reshape(NSA_REP, Q_BLOCK, LANES), imp

    def select_blocks(imp):
        key = jnp.where(forced, jnp.inf, jnp.where(valid, imp, -jnp.inf))
        key_t = key.T[:n_blk, :]
        groups = [key_t[8 * v:8 * v + 8, :] for v in range(n_blk // 8)]
        cnts = [jnp.zeros((8, Q_BLOCK), F32) for _ in groups]
        for i in range(n_blk):
            row = key_t[i:i + 1, :]
            for v, grp in enumerate(groups):
                if v < i // 8:
                    beats = jnp.where(row > grp, 1.0, 0.0)
                elif v > i // 8:
                    beats = jnp.where(row >= grp, 1.0, 0.0)
                else:
                    beats = jnp.where(jrow8 > i % 8, jnp.where(row >= grp, 1.0, 0.0),
                                      jnp.where(row > grp, 1.0, 0.0))
                cnts[v] = cnts[v] + beats
        sel_t = jnp.where(jnp.concatenate(cnts, axis=0) < float(N_SELECT), 1.0, 0.0)
        if n_blk < LANES:
            sel_t = jnp.concatenate([sel_t, jnp.zeros((LANES - n_blk, Q_BLOCK), F32)], axis=0)
        return sel_t.T

    def store_block_bias(g, sel):
        past = lane < 2 * qb
        bias = jnp.where(past, (sel - 1.0) * (-NEG_BIG), NEG_BIG)
        base = sel_lane_base(g)
        here = (lane >= base) & (lane < base + SEL_LANES)
        for c in range(seq // SEL_CHUNK):
            moved = pltpu.roll(bias, (base - c * SEL_LANES) % LANES, axis=1)
            qbias_scr[g, c] = jnp.where(here, moved, 0.0).astype(BF16)

    def sel_scores(c, g):
        k0 = pl.multiple_of(c * SEL_CHUNK, SEL_CHUNK)
        lhs = (q_ref[0, g * NSA_REP:(g + 1) * NSA_REP] + qbias_scr[g, c][None]).reshape(rq, LANES)
        return _dot_nt(lhs, ks_ref[0, g, pl.ds(k0, SEL_CHUNK), :])

    def sel_update(g, c, scores, state):
        k0 = pl.multiple_of(c * SEL_CHUNK, SEL_CHUNK)
        vch = vs_ref[0, pl.ds(k0, SEL_CHUNK), :]
        m_old, acc = state
        s = scores.reshape(NSA_REP, Q_BLOCK, SEL_CHUNK)
        m_new = jnp.maximum(m_old, jnp.max(s, axis=-1, keepdims=True))
        p = jnp.exp2(s - m_new)
        pv = _dot(p.reshape(rq, SEL_CHUNK).astype(BF16), with_ones(vch, g)).reshape(NSA_REP, Q_BLOCK, LANES)
        return m_new, jnp.exp2(m_old - m_new) * acc + pv

    kwin = kw_ref[0, pl.ds(wstart, WINDOW + Q_BLOCK), :]
    vwin = vw_ref[0, pl.ds(wstart, WINDOW + Q_BLOCK), :]
    cmp_scores = [_dot_nt(qs[g], kcmp_ref[0]) for g in groups_g]
    win_scores = [_dot_nt(qs[g], kwin) for g in groups_g]
    sdiag = pl.multiple_of(s0, Q_BLOCK)
    vdiag = vs_ref[0, pl.ds(sdiag, Q_BLOCK), :]
    diag_scores = [_dot_nt(qs[g], ks_ref[0, g, pl.ds(sdiag, Q_BLOCK), :]) for g in groups_g]

    cmp_out = [compressed(g, cmp_scores[g]) for g in groups_g]

    wbias = jnp.where(wmask, 0.0, NEG_BIG)
    sws = [win_scores[g].reshape(NSA_REP, Q_BLOCK, WINDOW + Q_BLOCK) + wbias[None] for g in groups_g]
    pws = [jnp.exp2(sw - jnp.max(sw, axis=-1, keepdims=True)) for sw in sws]
    wins = [_dot(pws[g].reshape(rq, WINDOW + Q_BLOCK).astype(BF16), with_ones(vwin, g)).reshape(NSA_REP, Q_BLOCK, LANES)
            for g in groups_g]

    for g in groups_g:
        store_block_bias(g, select_blocks(cmp_out[g][1]))

    dmask = (lax.broadcasted_iota(jnp.int32, (Q_BLOCK, Q_BLOCK), 1)
             <= lax.broadcasted_iota(jnp.int32, (Q_BLOCK, Q_BLOCK), 0))
    dbias = jnp.where(dmask, 0.0, NEG_BIG)
    states = []
    for g in groups_g:
        s = diag_scores[g].reshape(NSA_REP, Q_BLOCK, Q_BLOCK) + dbias[None]
        m0 = jnp.max(s, axis=-1, keepdims=True)
        p = jnp.exp2(s - m0).reshape(rq, Q_BLOCK).astype(BF16)
        states.append((m0, _dot(p, with_ones(vdiag, g)).reshape(NSA_REP, Q_BLOCK, LANES)))

    def sel_step(c, states):
        scores = [sel_scores(c, g) for g in groups_g]
        return tuple(sel_update(g, c, scores[g], states[g]) for g in groups_g)

    n_past = (qb + SEL_CHUNK // Q_BLOCK - 1) // (SEL_CHUNK // Q_BLOCK)
    sel_out = lax.fori_loop(0, n_past, sel_step, tuple(states))

    heads = [(g, r) for g in groups_g for r in range(NSA_REP)]
    c_sel = [gates[:, col_of(g, r) + 1:col_of(g, r) + 2]
             * (1.0 / sel_out[g][1][r][:, col_of(g, r) + 1:col_of(g, r) + 2]) for g, r in heads]
    c_win = [gates[:, col_of(g, r) + 2:col_of(g, r) + 3]
             * (1.0 / wins[g][r][:, col_of(g, r) + 2:col_of(g, r) + 3]) for g, r in heads]
    outs = [cmp_out[g][0][r] + c_sel[i] * sel_out[g][1][r] + c_win[i] * wins[g][r]
            for i, (g, r) in enumerate(heads)]
    outs = [jnp.where(lmasks[g], o, 0.0) for (g, r), o in zip(heads, outs)]
    ones_mat = jnp.ones((LANES, LANES), BF16)

    def row_sums(sq):
        hi = sq.astype(BF16)
        lo = (sq - hi.astype(F32)).astype(BF16)
        return _dot(hi, ones_mat) + _dot(lo, ones_mat)

    mss = [row_sums(o * o) * (1.0 / NSA_HEAD_DIM) for o in outs]
    ys = [o * lax.rsqrt(ms + NORM_EPS) * ng_ref[...] for o, ms in zip(outs, mss)]
    for g in groups_g:
        for pair in range(NSA_REP // 2):
            a, bb = ys[g * NSA_REP + 2 * pair], ys[g * NSA_REP + 2 * pair + 1]
            if g == 0:
                tile = a + pltpu.roll(bb, NSA_HEAD_DIM, axis=1)
            else:
                tile = pltpu.roll(a, NSA_HEAD_DIM, axis=1) + bb
            c0 = g * NSA_REP * NSA_HEAD_DIM + pair * LANES
            o_ref[0, :, c0:c0 + LANES] = tile.astype(o_ref.dtype)


def _nsa(qpad, kcmp, vcmp, ks, vs, kw, vw, ba, ovl, ng_lane):
    b, _, t, _ = qpad.shape
    n = kcmp.shape[1]
    whole = lambda rows: pl.BlockSpec((1, rows, LANES), lambda i, j: (i, 0, 0))
    return pl.pallas_call(
        functools.partial(_nsa_body, seq=t),
        grid=(b, t // Q_BLOCK),
        in_specs=[
            pl.BlockSpec((1, NSA_HEADS, Q_BLOCK, LANES), lambda i, j: (i, 0, j, 0)),
            whole(n), whole(n),
            pl.BlockSpec((1, NSA_KV_HEADS, t, LANES), lambda i, j: (i, 0, 0, 0)),
            whole(t), whole(t), whole(t),
            pl.BlockSpec((1, Q_BLOCK, LANES), lambda i, j: (i, j, 0)),
            pl.BlockSpec((n, LANES), lambda i, j: (0, 0)),
            pl.BlockSpec((1, LANES), lambda i, j: (0, 0)),
        ],
        out_specs=pl.BlockSpec((1, Q_BLOCK, NSA_WIDTH), lambda i, j: (i, j, 0)),
        out_shape=jax.ShapeDtypeStruct((b, t, NSA_WIDTH), BF16),
        scratch_shapes=[pltpu.VMEM((NSA_KV_HEADS, t // SEL_CHUNK, Q_BLOCK, LANES), BF16)],
        compiler_params=_cparams(("arbitrary", "arbitrary")),
        name="nsa_attn",
    )(qpad, kcmp, vcmp, ks, vs, kw, vw, ba, ovl, ng_lane)


_IN_SIZES = (3 * GDN_WIDTH, GDN_WIDTH, GDN_HEADS, GDN_HEADS, NSA_WIDTH) + (NSA_KV_WIDTH,) * 6 + (3 * NSA_HEADS,)


def _prep_inproj_weight(w):
    offs = np.concatenate([[0], np.cumsum(_IN_SIZES)])
    seg = lambda i: w[:, int(offs[i]):int(offs[i + 1])]
    small = jnp.concatenate([seg(2), seg(3), seg(11)], axis=1)
    pad = lambda n: jnp.zeros((w.shape[0], n), w.dtype)
    big = jnp.concatenate([seg(0), seg(1)] + [seg(i) for i in range(4, 11)]
                          + [small, pad(_GATE_COPY - _SMALL_ROWS), seg(11),
                             pad(LANES - _GATE_COPY - 3 * NSA_HEADS)], axis=1)
    return big.astype(BF16)


def _prep_compress(pe, w1, w2):
    hid = CMP_HIDDEN
    g, dh = NSA_KV_HEADS, NSA_HEAD_DIM
    half = CMP_LEN // 2
    w1r = w1.reshape(CMP_LEN, dh, hid)
    eye = jnp.eye(g, dtype=w1.dtype)
    expand = lambda part: jnp.einsum("ldh,gk->lgdkh", part, eye).reshape(half * g * dh, g * hid)
    wa, wb = expand(w1r[:half]), expand(w1r[half:])
    pe_row = lambda part: jnp.broadcast_to(part[:, None, :], (half, g, dh)).reshape(1, half * g * dh)
    pea, peb = pe_row(pe[:half]), pe_row(pe[half:])
    w2bd = jnp.einsum("hd,gk->ghkd", w2, eye).reshape(g * hid, g * dh)
    return pea, peb, wa.astype(BF16), wb.astype(BF16), w2bd.astype(BF16)


def _overlap_matrix(n_rows, seq):
    n_cmp = (seq - CMP_LEN) // CMP_STRIDE + 1
    n_blk = seq // SEL_BLOCK
    n = np.arange(n_rows)[:, None]
    j = np.arange(LANES)[None, :]
    start, end = n * CMP_STRIDE, n * CMP_STRIDE + CMP_LEN - 1
    ovl = (start < j * SEL_BLOCK + SEL_BLOCK) & (end >= j * SEL_BLOCK) & (n < n_cmp) & (j < n_blk)
    return jnp.asarray(ovl.astype(np.float32), dtype=BF16)


def kernel(x, c, positions, ada_w, ada_b, norm_g, ffn_w_in, ffn_w_out, mix_w_in, gdn_conv_w, gdn_a_log, gdn_dt_bias, gdn_norm_g, cmp_pe_k, cmp_w1_k, cmp_w2_k, cmp_pe_v, cmp_w1_v, cmp_w2_v, nsa_norm_g, mix_w_out, final_norm_g):
    b, t, d = x.shape
    depth = ada_w.shape[0]
    assert t % SEL_CHUNK == 0 and t >= WINDOW + Q_BLOCK and b <= 8

    c_pad = jnp.zeros((8, d), F32).at[:b].set(c.astype(F32))
    mod = _ada_mod(c_pad, ada_w.astype(F32), ada_b.astype(F32))
    mod = mod[:, :b].reshape(depth, b, 3, 3, d)
    mod8 = jnp.concatenate([mod, jnp.zeros((depth, b, 3, 5, d), F32)], axis=3)

    half = ROT_DIM // 2
    inv_freq = jnp.power(ROPE_THETA, -jnp.arange(half, dtype=F32) * (2.0 / ROT_DIM))
    dim = np.arange(LANES) % NSA_HEAD_DIM
    invf_lane = jnp.where(jnp.asarray(dim < ROT_DIM), inv_freq[dim % half], 0.0).reshape(1, LANES)
    cosf, sint = _rope_tables(positions.astype(F32).reshape(b, t, 1), invf_lane, tm=1024)

    n_rows = t // CMP_STRIDE
    ovl = _overlap_matrix(n_rows, t)
    fg = final_norm_g.reshape(1, d).astype(F32)

    w_in_all = ffn_w_in.astype(BF16)
    w_out_all = ffn_w_out.astype(BF16)
    w_mix_all = mix_w_out.astype(BF16)

    gain_all = norm_g.astype(F32).reshape(depth * 3, 1, d)

    for l in range(depth):
        x = _ffn(x, mod8, gain_all, w_in_all, w_out_all, l, 0, fg, final_norm=False)

        (qkv, z, ba, bat, qpad, kc, vc, ks, vs, kw, vw) = _inproj(
            x, mod8, gain_all, l, _prep_inproj_weight(mix_w_in[l]), cosf, sint)

        neg_a = -jnp.exp(gdn_a_log[l].astype(F32))
        dtb = gdn_dt_bias[l].astype(F32)
        avec = jnp.zeros((8, LANES), F32).at[0, GDN_HEADS:2 * GDN_HEADS].set(neg_a)
        avec = avec.at[1, GDN_HEADS:2 * GDN_HEADS].set(dtb)
        avect = jnp.zeros((_SMALL_ROWS, LANES), F32).at[GDN_HEADS:2 * GDN_HEADS, 0].set(neg_a)
        avect = avect.at[GDN_HEADS:2 * GDN_HEADS, 1].set(dtb)
        y_gdn = _gdn(qkv, z, ba, bat, gdn_conv_w[l].astype(F32), avec, avect,
                     gdn_norm_g[l].reshape(1, GDN_HEAD_DIM).astype(F32))

        kcmp = _compress(kc, *_prep_compress(cmp_pe_k[l], cmp_w1_k[l], cmp_w2_k[l]))
        vcmp = _compress(vc, *_prep_compress(cmp_pe_v[l], cmp_w1_v[l], cmp_w2_v[l]))
        ng_lane = jnp.tile(nsa_norm_g[l].astype(F32), NSA_KV_HEADS).reshape(1, LANES)
        y_nsa = _nsa(qpad, kcmp, vcmp, ks, vs, kw, vw, ba, ovl, ng_lane)

        x = _ffn(x, mod8, gain_all, w_in_all, w_out_all, l, 1, fg, final_norm=(l == depth - 1),
                 mix=(y_gdn, y_nsa, w_mix_all))
    return x
```

```python
import functools
import math

import jax
import jax.numpy as jnp
import numpy as np
from jax import lax
from jax.experimental import pallas as pl
from jax.experimental.pallas import tpu as pltpu

F32 = jnp.float32
BF16 = jnp.bfloat16

NORM_EPS = 1e-6
LANES = 128
GDN_HEADS = 4
GDN_HEAD_DIM = 128
GDN_WIDTH = GDN_HEADS * GDN_HEAD_DIM
GDN_CONV = 4
GDN_CHUNK = 64
GDN_PREP_STEP = 8 * GDN_CHUNK
GDN_SCAN_STEP = 2 * GDN_CHUNK
NSA_HEADS = 8
NSA_KV_HEADS = 2
NSA_REP = NSA_HEADS // NSA_KV_HEADS
NSA_HEAD_DIM = 64
NSA_WIDTH = NSA_HEADS * NSA_HEAD_DIM
NSA_KV_WIDTH = NSA_KV_HEADS * NSA_HEAD_DIM
CMP_LEN = 32
CMP_STRIDE = 16
CMP_HIDDEN = 2 * NSA_HEAD_DIM
SEL_BLOCK = 64
N_SELECT = 16
WINDOW = 512
Q_BLOCK = 128
SEL_CHUNK = 1024
SEL_LANES = SEL_CHUNK // SEL_BLOCK


def sel_lane_base(g):
    return NSA_HEAD_DIM if g == 0 else 0
ROPE_THETA = 500000.0
ROT_DIM = NSA_HEAD_DIM // 4
N_ADA = 9
NEG_BIG = -1e30
VMEM_LIMIT = 56 * 1024 * 1024


def _cparams(sem):
    return pltpu.CompilerParams(dimension_semantics=sem, vmem_limit_bytes=VMEM_LIMIT)


def _dot(a, b):
    return jnp.dot(a, b, preferred_element_type=F32)


def _dot_nt(a, b):
    return lax.dot_general(a, b, (((1,), (1,)), ((), ())), preferred_element_type=F32)


def _split3(x):
    hi = x.astype(BF16)
    r1 = x - hi.astype(F32)
    mid = r1.astype(BF16)
    lo = (r1 - mid.astype(F32)).astype(BF16)
    return hi, mid, lo


def _sigmoid(x):
    return 1.0 / (1.0 + jnp.exp(-x))


def _silu(x):
    return x * _sigmoid(x)


def _rms_rows(x, g):
    ms = jnp.mean(x * x, axis=-1, keepdims=True)
    return x * lax.rsqrt(ms + NORM_EPS) * g


def _ada_body(c_ref, w_ref, b_ref, o_ref):
    cond = _silu(c_ref[...])
    hi, mid, lo = [t.astype(F32) for t in _split3(cond)]
    rows = cond.shape[0]
    w = w_ref[0]
    w_hi = w.astype(BF16)
    w_lo = (w - w_hi.astype(F32)).astype(BF16)
    first = _dot(jnp.concatenate([hi, mid, lo, jnp.zeros_like(hi)], axis=0).astype(BF16), w_hi)
    second = _dot(jnp.concatenate([hi, mid], axis=0).astype(BF16), w_lo)
    o_ref[0] = ((first[:rows] + first[rows:2 * rows]) + (first[2 * rows:3 * rows] + second[:rows])
                + second[rows:]) + b_ref[0]


def _ada_mod(c_pad, ada_w, ada_b):
    depth, d, n = ada_w.shape
    tn = n // N_ADA
    rows = c_pad.shape[0]
    return pl.pallas_call(
        _ada_body,
        grid=(depth, n // tn),
        in_specs=[
            pl.BlockSpec((rows, d), lambda l, j: (0, 0)),
            pl.BlockSpec((1, d, tn), lambda l, j: (l, 0, j)),
            pl.BlockSpec((1, 1, tn), lambda l, j: (l, 0, j)),
        ],
        out_specs=pl.BlockSpec((1, rows, tn), lambda l, j: (l, 0, j)),
        out_shape=jax.ShapeDtypeStruct((depth, rows, n), F32),
        compiler_params=_cparams(("arbitrary", "arbitrary")),
        name="ada_mod",
    )(c_pad, ada_w, ada_b.reshape(depth, 1, n))


def _rope_body(pos_ref, invf_ref, cos_ref, sin_ref):
    ang = pos_ref[0] * invf_ref[...]
    cos_ref[0] = jnp.cos(ang)
    sin_ref[0] = jnp.sin(ang)


def _rope_tables(pos_f, invf_lane, tm):
    b, t, _ = pos_f.shape
    spec = pl.BlockSpec((1, tm, LANES), lambda i, j: (i, j, 0))
    return pl.pallas_call(
        _rope_body,
        grid=(b, t // tm),
        in_specs=[pl.BlockSpec((1, tm, 1), lambda i, j: (i, j, 0)),
                  pl.BlockSpec((1, LANES), lambda i, j: (0, 0))],
        out_specs=[spec, spec],
        out_shape=[jax.ShapeDtypeStruct((b, t, LANES), F32)] * 2,
        compiler_params=_cparams(("arbitrary", "arbitrary")),
        name="rope_tables",
    )(pos_f, invf_lane)


def _ffn_body(*refs, final_norm, with_mix, tf):
    if with_mix:
        (x_ref, mod_ref, g_ref, wi_ref, wo_ref, fg_ref,
         mmod_ref, yg_ref, yn_ref, wm_ref, o_ref, act_scr) = refs
        mix = _dot(yg_ref[0], wm_ref[0:GDN_WIDTH, :]) + _dot(yn_ref[0], wm_ref[GDN_WIDTH:, :])
        x = x_ref[0] + mmod_ref[0, 2:3, :] * mix
    else:
        x_ref, mod_ref, g_ref, wi_ref, wo_ref, fg_ref, o_ref, act_scr = refs
        x = x_ref[0]
    y = _rms_rows(x, g_ref[...])
    h = (y * (1.0 + mod_ref[0, 1:2, :]) + mod_ref[0, 0:1, :]).astype(BF16)
    ff = wo_ref.shape[0]
    for j in range(ff // tf):
        a = _dot(h, wi_ref[:, j * tf:(j + 1) * tf])
        u = _dot(h, wi_ref[:, ff + j * tf:ff + (j + 1) * tf])
        act_scr[:, j * tf:(j + 1) * tf] = (_silu(a) * u).astype(BF16)
    out = x + (0.5 * mod_ref[0, 2:3, :]) * _dot(act_scr[...], wo_ref[...])
    if final_norm:
        out = _rms_rows(out, fg_ref[...])
    o_ref[0] = out


def _mod_spec(d, layer, sub):
    return pl.BlockSpec((None, 1, None, 8, d), lambda i, j: (layer, i, sub, 0, 0))


def _gain_spec(d, layer, sub):
    return pl.BlockSpec((None, 1, d), lambda i, j: (layer * 3 + sub, 0, 0))


def _ffn(x, mod_all, gain_all, w_in_all, w_out_all, layer, idx, fg, *, final_norm, mix=None, tm=1024, tf=256):
    b, t, d = x.shape
    ff = w_out_all.shape[2]
    sub = 2 * idx
    row = lambda n: pl.BlockSpec((1, tm, n), lambda i, j: (i, j, 0))
    vec_spec = pl.BlockSpec((1, d), lambda i, j: (0, 0))
    stacked = lambda r, c, *lead: pl.BlockSpec((None,) * len(lead) + (r, c), lambda i, j: lead + (0, 0),
                                               pipeline_mode=pl.Buffered(1))
    in_specs = [row(d), _mod_spec(d, layer, sub), _gain_spec(d, layer, sub),
                stacked(d, 2 * ff, layer, idx), stacked(ff, d, layer, idx), vec_spec]
    args = [x, mod_all, gain_all, w_in_all, w_out_all, fg]
    if mix is not None:
        yg, yn, w_mix_all = mix
        in_specs += [_mod_spec(d, layer, 1), row(GDN_WIDTH), row(NSA_WIDTH), stacked(d, d, layer)]
        args += [mod_all, yg, yn, w_mix_all]
    return pl.pallas_call(
        functools.partial(_ffn_body, final_norm=final_norm, with_mix=mix is not None, tf=tf),
        grid=(b, t // tm),
        in_specs=in_specs,
        out_specs=row(d),
        out_shape=jax.ShapeDtypeStruct((b, t, d), F32),
        scratch_shapes=[pltpu.VMEM((tm, ff), BF16)],
        compiler_params=_cparams(("arbitrary", "arbitrary")),
        name=("ffn_mix" if mix is not None else "ffn") + ("_final" if final_norm else ""),
    )(*args)


_O_QKV, _O_Z, _O_Q = 0, 1536, 2048
_O_KC, _O_VC, _O_KS, _O_VS, _O_KW, _O_VW, _O_SM = 2560, 2688, 2816, 2944, 3072, 3200, 3328
_IN_COLS = 3456
_SMALL_ROWS = 32
_GATE_COPY = NSA_HEAD_DIM + 2 * GDN_HEADS


def _rope_tile(x, cosf, sint, first_half):
    fwd = pltpu.roll(x, 8, axis=1)
    bwd = pltpu.roll(x, LANES - 8, axis=1)
    return x * cosf + jnp.where(first_half, -bwd, fwd) * sint


def _inproj_body(x_ref, mod_ref, g_ref, w_ref, cos_ref, sin_ref,
                 qkv_ref, z_ref, ba_ref, bat_ref, qpad_ref,
                 kc_ref, vc_ref, ks_ref, vs_ref, kw_ref, vw_ref):
    y = _rms_rows(x_ref[0], g_ref[...])
    h = (y * (1.0 + mod_ref[0, 1:2, :]) + mod_ref[0, 0:1, :]).astype(BF16)
    cosf = cos_ref[0]
    sint = sin_ref[0]
    lane = lax.broadcasted_iota(jnp.int32, cosf.shape, 1)
    first_half = (lane % NSA_HEAD_DIM) < (ROT_DIM // 2)
    low = lane < NSA_HEAD_DIM

    for j in range(3):
        qkv_ref[0, :, j * 512:(j + 1) * 512] = _dot(h, w_ref[:, _O_QKV + j * 512:_O_QKV + (j + 1) * 512])
    z_ref[0] = _dot(h, w_ref[:, _O_Z:_O_Z + 512])
    ba = _dot(h, w_ref[:, _O_SM:_O_SM + LANES])
    ba_ref[0] = ba
    bat_ref[0] = ba.T[:_SMALL_ROWS, :]

    scale = NSA_HEAD_DIM ** -0.5 * math.log2(math.e)
    def pair(off):
        res = _dot(h, w_ref[:, off:off + 2 * LANES])
        return res[:, :LANES], res[:, LANES:]

    q_tiles = pair(_O_Q) + pair(_O_Q + 2 * LANES)
    for k in range(NSA_HEADS // 2):
        tile = _rope_tile(q_tiles[k], cosf, sint, first_half) * scale
        swapped = pltpu.roll(tile, NSA_HEAD_DIM, axis=1)
        grp = (2 * k) // NSA_REP
        if grp == 0:
            even = jnp.where(low, tile, 0.0)
            odd = jnp.where(low, swapped, 0.0)
        else:
            even = jnp.where(low, 0.0, swapped)
            odd = jnp.where(low, 0.0, tile)
        qpad_ref[0, 2 * k] = even.astype(BF16)
        qpad_ref[0, 2 * k + 1] = odd.astype(BF16)

    kc, vc = pair(_O_KC)
    kc_ref[0] = _rope_tile(kc, cosf, sint, first_half)
    vc_ref[0] = vc
    ks, vs = pair(_O_KS)
    ks = _rope_tile(ks, cosf, sint, first_half)
    tok = pl.program_id(1) * ks.shape[0] + lax.broadcasted_iota(jnp.int32, ks.shape, 0)
    blk = lax.shift_right_logical(tok, 6) & (SEL_LANES - 1)
    for grp in range(NSA_KV_HEADS):
        onehot = jnp.where(lane - sel_lane_base(grp) == blk, 1.0, 0.0)
        mixed = jnp.where(low, ks, onehot) if grp == 0 else jnp.where(low, onehot, ks)
        ks_ref[0, grp] = mixed.astype(BF16)
    vs_ref[0] = vs.astype(BF16)
    kw, vw = pair(_O_KW)
    kw_ref[0] = _rope_tile(kw, cosf, sint, first_half).astype(BF16)
    vw_ref[0] = vw.astype(BF16)


def _inproj(x, mod_all, gain_all, layer, w, cosf, sint, tm=512):
    b, t, d = x.shape
    row = lambda n: pl.BlockSpec((1, tm, n), lambda i, j: (i, j, 0))
    sds = lambda n, dt: jax.ShapeDtypeStruct((b, t, n), dt)
    return pl.pallas_call(
        _inproj_body,
        grid=(b, t // tm),
        in_specs=[
            row(d),
            _mod_spec(d, layer, 1),
            _gain_spec(d, layer, 1),
            pl.BlockSpec((d, _IN_COLS), lambda i, j: (0, 0)),
            row(LANES), row(LANES),
        ],
        out_specs=[
            row(3 * GDN_WIDTH), row(GDN_WIDTH), row(LANES),
            pl.BlockSpec((1, _SMALL_ROWS, tm), lambda i, j: (i, 0, j)),
            pl.BlockSpec((1, NSA_HEADS, tm, LANES), lambda i, j: (i, 0, j, 0)),
            row(LANES), row(LANES),
            pl.BlockSpec((1, NSA_KV_HEADS, tm, LANES), lambda i, j: (i, 0, j, 0)),
            row(LANES), row(LANES), row(LANES),
        ],
        out_shape=[
            sds(3 * GDN_WIDTH, F32), sds(GDN_WIDTH, F32), sds(LANES, F32),
            jax.ShapeDtypeStruct((b, _SMALL_ROWS, t), F32),
            jax.ShapeDtypeStruct((b, NSA_HEADS, t, LANES), BF16),
            sds(LANES, F32), sds(LANES, F32),
            jax.ShapeDtypeStruct((b, NSA_KV_HEADS, t, LANES), BF16),
            sds(LANES, BF16), sds(LANES, BF16), sds(LANES, BF16),
        ],
        compiler_params=_cparams(("arbitrary", "arbitrary")),
        name="mix_inproj",
    )(x, mod_all, gain_all, w, cosf, sint)


def _softplus(x):
    return jnp.maximum(x, 0.0) + jnp.log1p(jnp.exp(-jnp.abs(x)))


def _dot_inv(a, b):
    ah = a.astype(BF16)
    al = (a - ah.astype(F32)).astype(BF16)
    bh = b.astype(BF16)
    bl = (b - bh.astype(F32)).astype(BF16)
    m = a.shape[0]
    top = _dot(jnp.concatenate([ah, al], axis=0), bh)
    return top[:m] + top[m:] + _dot(ah, bl)


def _gdn_prep_body(qkv_ref, ba_ref, bat_ref, cw_ref, av_ref, avt_ref,
                   u_ref, wq_ref, ak_ref, egl_ref, buf):
    step = pl.program_id(1)
    c64 = GDN_CHUNK
    hd = GDN_HEAD_DIM
    nt = GDN_PREP_STEP

    @pl.when(step == 0)
    def _():
        buf[0:8, :] = jnp.zeros((8, 3 * GDN_WIDTH), F32)

    x = qkv_ref[0]
    buf[8:8 + nt, :] = x
    y = x * cw_ref[3:4, :]
    for j in range(GDN_CONV - 1):
        sh = GDN_CONV - 1 - j
        y = y + buf[8 - sh:8 - sh + nt, :] * cw_ref[j:j + 1, :]
    buf[0:8, :] = x[nt - 8:nt, :]
    y = _silu(y)

    ba = ba_ref[0]
    beta_all = _sigmoid(ba)
    g_col_all = av_ref[0:1, :] * _softplus(ba + av_ref[1:2, :])
    bat = bat_ref[0]
    g_row_all = avt_ref[:, 0:1] * _softplus(bat + avt_ref[:, 1:2])

    pw = 2 * c64
    rr = lax.broadcasted_iota(jnp.int32, (pw, pw), 0)
    kk = lax.broadcasted_iota(jnp.int32, (pw, pw), 1)
    same = lax.shift_right_logical(rr, 6) == lax.shift_right_logical(kk, 6)
    causal = same & (rr >= kk)
    strict = same & (rr > kk)
    tri = jnp.where(causal, 1.0, 0.0)
    tri_t = jnp.where(same & (rr <= kk), 1.0, 0.0)
    first = lax.broadcasted_iota(jnp.int32, (pw, 1), 0) < c64

    npair = nt // pw
    units = [(pc, h) for pc in range(npair) for h in range(GDN_HEADS)]
    rows_of = lambda pc: slice(pc * pw, (pc + 1) * pw)
    tri_b, tri_tb = tri.astype(BF16), tri_t.astype(BF16)
    gc_cols = [sum(_dot(tri_b, part) for part in _split3(g_col_all[rows_of(pc), :]))
               for pc in range(npair)]
    gc_rows = [sum(_dot(part, tri_tb) for part in _split3(g_row_all[:, rows_of(pc)]))
               for pc in range(npair)]

    def l2n(v):
        return v * lax.rsqrt(jnp.sum(v * v, axis=-1, keepdims=True) + NORM_EPS)

    def front(pc, h):
        kh = l2n(y[rows_of(pc), GDN_WIDTH + h * hd:GDN_WIDTH + (h + 1) * hd])
        gcol = gc_cols[pc][:, GDN_HEADS + h:GDN_HEADS + h + 1]
        grow = gc_rows[pc][GDN_HEADS + h:GDN_HEADS + h + 1, :]
        decay = jnp.exp(jnp.where(causal, gcol - grow, -jnp.inf))
        kb = kh * beta_all[rows_of(pc), h:h + 1]
        low = jnp.where(strict, _dot_nt(kb.astype(BF16), kh.astype(BF16)) * decay, 0.0)
        return kh, kb, decay, low

    egl_rows = [[] for _ in range(nt // c64)]

    def back(pc, h, kh, kb, decay, inv):
        rows = rows_of(pc)
        qh = l2n(y[rows, h * hd:(h + 1) * hd]) * (hd ** -0.5)
        vh = y[rows, 2 * GDN_WIDTH + h * hd:2 * GDN_WIDTH + (h + 1) * hd]
        gcol = gc_cols[pc][:, GDN_HEADS + h:GDN_HEADS + h + 1]
        grow = gc_rows[pc][GDN_HEADS + h:GDN_HEADS + h + 1, :]
        glasts = [grow[:, c64 - 1:c64], grow[:, pw - 1:pw]]
        glast = jnp.where(first, glasts[0], glasts[1])
        eg = jnp.exp(gcol)
        rhs = jnp.concatenate([vh * beta_all[rows, h:h + 1], kb * eg], axis=1)
        sol = _dot_inv(inv, rhs)
        attn = _dot_nt(qh.astype(BF16), kh.astype(BF16)) * decay
        kd_t = (kh * jnp.exp(glast - gcol)).T
        qe = (qh * eg).astype(BF16)
        u_ref[0, h, rows, :] = sol[:, :hd]
        for half in range(2):
            ch = 2 * pc + half
            part = slice(half * c64, (half + 1) * c64)
            wq_ref[0, h, ch, 0:c64, :] = sol[part, hd:].astype(BF16)
            wq_ref[0, h, ch, c64:2 * c64, :] = qe[part, :]
            ak_ref[0, h, ch, 0:c64, :] = attn[part, part].astype(BF16)
            ak_ref[0, h, ch, c64:c64 + hd, :] = kd_t[:, part].astype(BF16)
            egl_rows[ch].append(jnp.broadcast_to(jnp.exp(glasts[half]), (1, LANES)))

    eye = jnp.where(rr == kk, 1.0, 0.0)
    levels = int(math.log2(c64)) - 1
    waves = [units[:len(units) // 2], units[len(units) // 2:]]

    def spread(items, n):
        q, r = divmod(len(items), n)
        bounds = [i * q + min(i, r) for i in range(n + 1)]
        return [items[bounds[i]:bounds[i + 1]] for i in range(n)]

    def inverse_level(state):
        state = [(_dot_inv(p, p), inv) for p, inv in state]
        return [(p, inv + _dot_inv(inv, p)) for p, inv in state]

    fronts = [[front(pc, h) for pc, h in waves[0]], []]
    state = [(-f[3], eye - f[3]) for f in fronts[0]]
    for group in spread(waves[1], levels):
        state = inverse_level(state)
        fronts[1] += [front(pc, h) for pc, h in group]
    inv_first = [inv for _, inv in state]
    state = [(-f[3], eye - f[3]) for f in fronts[1]]
    for group in spread(list(range(len(waves[0]))), levels):
        state = inverse_level(state)
        for i in group:
            back(*waves[0][i], *fronts[0][i][:3], inv_first[i])
    for i, unit in enumerate(waves[1]):
        back(*unit, *fronts[1][i][:3], state[i][1])
    for ch in range(nt // c64):
        egl_ref[0, ch] = jnp.concatenate(egl_rows[ch] + [jnp.zeros((8 - GDN_HEADS, LANES), F32)], axis=0)


def _gdn_prep(qkv, ba, bat, conv_w, avec, avect):
    b, t, _ = qkv.shape
    nt = GDN_PREP_STEP
    nch = nt // GDN_CHUNK
    hd = GDN_HEAD_DIM
    row = lambda n: pl.BlockSpec((1, nt, n), lambda i, j: (i, j, 0))
    return pl.pallas_call(
        _gdn_prep_body,
        grid=(b, t // nt),
        in_specs=[
            row(3 * GDN_WIDTH), row(LANES),
            pl.BlockSpec((1, _SMALL_ROWS, nt), lambda i, j: (i, 0, j)),
            pl.BlockSpec((GDN_CONV, 3 * GDN_WIDTH), lambda i, j: (0, 0)),
            pl.BlockSpec((8, LANES), lambda i, j: (0, 0)),
            pl.BlockSpec((_SMALL_ROWS, LANES), lambda i, j: (0, 0)),
        ],
        out_specs=[
            pl.BlockSpec((1, GDN_HEADS, nt, hd), lambda i, j: (i, 0, j, 0)),
            pl.BlockSpec((1, GDN_HEADS, nch, 2 * GDN_CHUNK, hd), lambda i, j: (i, 0, j, 0, 0)),
            pl.BlockSpec((1, GDN_HEADS, nch, GDN_CHUNK + hd, GDN_CHUNK), lambda i, j: (i, 0, j, 0, 0)),
            pl.BlockSpec((1, nch, 8, LANES), lambda i, j: (i, j, 0, 0)),
        ],
        out_shape=[
            jax.ShapeDtypeStruct((b, GDN_HEADS, t, hd), F32),
            jax.ShapeDtypeStruct((b, GDN_HEADS, t // GDN_CHUNK, 2 * GDN_CHUNK, hd), BF16),
            jax.ShapeDtypeStruct((b, GDN_HEADS, t // GDN_CHUNK, GDN_CHUNK + hd, GDN_CHUNK), BF16),
            jax.ShapeDtypeStruct((b, t // GDN_CHUNK, 8, LANES), F32),
        ],
        scratch_shapes=[pltpu.VMEM((8 + nt, 3 * GDN_WIDTH), F32)],
        compiler_params=_cparams(("arbitrary", "arbitrary")),
        name="gdn_prep",
    )(qkv, ba, bat, conv_w, avec, avect)


def _gdn_scan_body(u_ref, wq_ref, ak_ref, egl_ref, z_ref, ng_ref, o_ref, s_scr):
    c64 = GDN_CHUNK
    hd = GDN_HEAD_DIM
    nb = u_ref.shape[0]

    @pl.when(pl.program_id(0) == 0)
    def _():
        s_scr[...] = jnp.zeros_like(s_scr)

    chains = [(b, h) for b in range(nb) for h in range(GDN_HEADS)]
    states = [s_scr[b, h] for b, h in chains]
    for ch in range(GDN_SCAN_STEP // c64):
        rows = slice(ch * c64, (ch + 1) * c64)
        r1s = [_dot(wq_ref[b, h, ch], s.astype(BF16)) for (b, h), s in zip(chains, states)]
        vns = [(u_ref[b, h, rows, :] - r1[0:c64]).astype(BF16) for (b, h), r1 in zip(chains, r1s)]
        r2s = [_dot(ak_ref[b, h, ch], vn) for (b, h), vn in zip(chains, vns)]
        states = [s * egl_ref[b, ch, h:h + 1, :] + r2[c64:c64 + hd]
                  for (b, h), s, r2 in zip(chains, states, r2s)]
        for (b, h), r1, r2 in zip(chains, r1s, r2s):
            on = _rms_rows(r1[c64:2 * c64] + r2[0:c64], ng_ref[...])
            zh = z_ref[b, rows, h * hd:(h + 1) * hd]
            o_ref[b, rows, h * hd:(h + 1) * hd] = (on * _silu(zh)).astype(o_ref.dtype)
    for (b, h), s in zip(chains, states):
        s_scr[b, h] = s


def _gdn_scan(u, wq, ak, egl, z, norm_g):
    b, _, t, hd = u.shape
    nt = GDN_SCAN_STEP
    nch = nt // GDN_CHUNK
    return pl.pallas_call(
        _gdn_scan_body,
        grid=(t // nt,),
        in_specs=[
            pl.BlockSpec((b, GDN_HEADS, nt, hd), lambda j: (0, 0, j, 0)),
            pl.BlockSpec((b, GDN_HEADS, nch, 2 * GDN_CHUNK, hd), lambda j: (0, 0, j, 0, 0)),
            pl.BlockSpec((b, GDN_HEADS, nch, GDN_CHUNK + hd, GDN_CHUNK), lambda j: (0, 0, j, 0, 0)),
            pl.BlockSpec((b, nch, 8, LANES), lambda j: (0, j, 0, 0)),
            pl.BlockSpec((b, nt, GDN_WIDTH), lambda j: (0, j, 0)),
            pl.BlockSpec((1, hd), lambda j: (0, 0)),
        ],
        out_specs=pl.BlockSpec((b, nt, GDN_WIDTH), lambda j: (0, j, 0)),
        out_shape=jax.ShapeDtypeStruct((b, t, GDN_WIDTH), BF16),
        scratch_shapes=[pltpu.VMEM((b, GDN_HEADS, hd, hd), F32)],
        compiler_params=_cparams(("arbitrary",)),
        name="gdn_scan",
    )(u, wq, ak, egl, z, norm_g)


def _gdn(qkv, z, ba, bat, conv_w, avec, avect, norm_g):
    u, wq, ak, egl = _gdn_prep(qkv, ba, bat, conv_w, avec, avect)
    return _gdn_scan(u, wq, ak, egl, z, norm_g)


def _cmp_body(x_ref, pea_ref, peb_ref, wa_ref, wb_ref, w2_ref, o_ref):
    n_rows = x_ref.shape[1] // CMP_STRIDE
    x = jnp.concatenate([x_ref[0, pl.ds(l, n_rows, stride=CMP_STRIDE), :] for l in range(CMP_STRIDE)], axis=1)
    a = _dot((x + pea_ref[...]).astype(BF16), wa_ref[...])
    bm = _dot((x + peb_ref[...]).astype(BF16), wb_ref[...])
    n = a.shape[0]
    h1 = a + pltpu.roll(bm, n - 1, axis=0)
    o_ref[0] = _dot(_silu(h1).astype(BF16), w2_ref[...]).astype(o_ref.dtype)


def _compress(x, pea, peb, wa, wb, w2bd):
    b, t, wdt = x.shape
    n = t // CMP_STRIDE
    full = lambda s: pl.BlockSpec(s, lambda i: (0,) * len(s))
    return pl.pallas_call(
        _cmp_body,
        grid=(b,),
        in_specs=[pl.BlockSpec((1, t, wdt), lambda i: (i, 0, 0)),
                  full(pea.shape), full(peb.shape), full(wa.shape), full(wb.shape), full(w2bd.shape)],
        out_specs=pl.BlockSpec((1, n, LANES), lambda i: (i, 0, 0)),
        out_shape=jax.ShapeDtypeStruct((b, n, LANES), BF16),
        compiler_params=_cparams(("arbitrary",)),
        name="nsa_compress",
    )(x, pea, peb, wa, wb, w2bd)


def _nsa_body(q_ref, kcmp_ref, vcmp_ref, ks_ref, vs_ref, kw_ref, vw_ref, gate_ref,
              ovl_ref, ng_ref, o_ref, qbias_scr, *, seq):
    qb = pl.program_id(1)
    s0 = qb * Q_BLOCK
    n_cmp_rows = kcmp_ref.shape[1]
    n_blk = seq // SEL_BLOCK
    rq = NSA_REP * Q_BLOCK

    t_q = s0 + lax.broadcasted_iota(jnp.int32, (Q_BLOCK, 1), 0)
    lane = lax.broadcasted_iota(jnp.int32, (Q_BLOCK, LANES), 1)
    gates = _sigmoid(gate_ref[0])

    n_iota = lax.broadcasted_iota(jnp.int32, (Q_BLOCK, n_cmp_rows), 1)
    cmask = (n_iota * CMP_STRIDE + (CMP_LEN - 1)) <= t_q

    wstart = pl.multiple_of(jnp.maximum(s0 - WINDOW, 0), Q_BLOCK)
    kp = wstart + lax.broadcasted_iota(jnp.int32, (Q_BLOCK, WINDOW + Q_BLOCK), 1)
    wmask = (kp <= t_q) & (kp > t_q - WINDOW)

    cur = lax.shift_right_logical(t_q, 6)
    forced = (lane == 0) | (lane == cur) | (lane == cur - 1)
    valid = (lane * SEL_BLOCK <= t_q) & (lane < n_blk)
    jrow8 = lax.broadcasted_iota(jnp.int32, (8, Q_BLOCK), 0)

    groups_g = range(NSA_KV_HEADS)
    qs = [q_ref[0, g * NSA_REP:(g + 1) * NSA_REP].reshape(rq, LANES) for g in groups_g]
    lmasks = [(lane >= g * NSA_HEAD_DIM) & (lane < (g + 1) * NSA_HEAD_DIM) for g in groups_g]
    kv_lane = lax.broadcasted_iota(jnp.int32, (1, LANES), 1)
    own = [(kv_lane >= g * NSA_HEAD_DIM) & (kv_lane < (g + 1) * NSA_HEAD_DIM) for g in groups_g]

    def with_ones(v, g):
        return jnp.where(own[g], v, jnp.ones_like(v))

    col_of = lambda g, r: (_GATE_COPY if g == 0 else 2 * GDN_HEADS) + (g * NSA_REP + r) * 3

    def compressed(g, sc):
        sc = jnp.where(cmask[None], sc.reshape(NSA_REP, Q_BLOCK, n_cmp_rows), -jnp.inf)
        m = jnp.max(sc, axis=-1, keepdims=True)
        m = jnp.where(m == -jnp.inf, 0.0, m)
        p = jnp.exp2(sc - m)
        inv = 1.0 / jnp.maximum(jnp.sum(p, axis=-1, keepdims=True), 1e-30)
        gate = jnp.stack([gates[:, col_of(g, r):col_of(g, r) + 1] for r in range(NSA_REP)], axis=0)
        o_cmp = _dot((p * (inv * gate)).reshape(rq, n_cmp_rows).astype(BF16), vcmp_ref[0])
        p = p * inv
        psum = p[0] + p[1] + p[2] + p[3]
        hi, mid, lo = _split3(psum)
        ovl = ovl_ref[...]
        imp = _dot(hi, ovl) + _dot(mid, ovl) + _dot(lo, ovl)
        return o_cmp.reshape(NSA_REP, Q_BLOCK, LANES), imp

    def select_blocks(imp):
        key = jnp.where(forced, jnp.inf, jnp.where(valid, imp, -jnp.inf))
        key_t = key.T[:n_blk, :]
        groups = [key_t[8 * v:8 * v + 8, :] for v in range(n_blk // 8)]
        cnts = [jnp.zeros((8, Q_BLOCK), F32) for _ in groups]
        for i in range(n_blk):
            row = key_t[i:i + 1, :]
            for v, grp in enumerate(groups):
                if v < i // 8:
                    beats = jnp.where(row > grp, 1.0, 0.0)
                elif v > i // 8:
                    beats = jnp.where(row >= grp, 1.0, 0.0)
                else:
                    beats = jnp.where(jrow8 > i % 8, jnp.where(row >= grp, 1.0, 0.0),
                                      jnp.where(row > grp, 1.0, 0.0))
                cnts[v] = cnts[v] + beats
        sel_t = jnp.where(jnp.concatenate(cnts, axis=0) < float(N_SELECT), 1.0, 0.0)
        if n_blk < LANES:
            sel_t = jnp.concatenate([sel_t, jnp.zeros((LANES - n_blk, Q_BLOCK), F32)], axis=0)
        return sel_t.T

    def store_block_bias(g, sel):
        past = lane < 2 * qb
        bias = jnp.where(past, (sel - 1.0) * (-NEG_BIG), NEG_BIG)
        base = sel_lane_base(g)
        here = (lane >= base) & (lane < base + SEL_LANES)
        for c in range(seq // SEL_CHUNK):
            moved = pltpu.roll(bias, (base - c * SEL_LANES) % LANES, axis=1)
            qbias_scr[g, c] = jnp.where(here, moved, 0.0).astype(BF16)

    def sel_scores(c, g):
        k0 = pl.multiple_of(c * SEL_CHUNK, SEL_CHUNK)
        lhs = (q_ref[0, g * NSA_REP:(g + 1) * NSA_REP] + qbias_scr[g, c][None]).reshape(rq, LANES)
        return _dot_nt(lhs, ks_ref[0, g, pl.ds(k0, SEL_CHUNK), :])

    def sel_update(g, c, scores, state):
        k0 = pl.multiple_of(c * SEL_CHUNK, SEL_CHUNK)
        vch = vs_ref[0, pl.ds(k0, SEL_CHUNK), :]
        m_old, acc = state
        s = scores.reshape(NSA_REP, Q_BLOCK, SEL_CHUNK)
        m_new = jnp.maximum(m_old, jnp.max(s, axis=-1, keepdims=True))
        p = jnp.exp2(s - m_new)
        pv = _dot(p.reshape(rq, SEL_CHUNK).astype(BF16), with_ones(vch, g)).reshape(NSA_REP, Q_BLOCK, LANES)
        return m_new, jnp.exp2(m_old - m_new) * acc + pv

    kwin = kw_ref[0, pl.ds(wstart, WINDOW + Q_BLOCK), :]
    vwin = vw_ref[0, pl.ds(wstart, WINDOW + Q_BLOCK), :]
    cmp_scores = [_dot_nt(qs[g], kcmp_ref[0]) for g in groups_g]
    win_scores = [_dot_nt(qs[g], kwin) for g in groups_g]
    sdiag = pl.multiple_of(s0, Q_BLOCK)
    vdiag = vs_ref[0, pl.ds(sdiag, Q_BLOCK), :]
    diag_scores = [_dot_nt(qs[g], ks_ref[0, g, pl.ds(sdiag, Q_BLOCK), :]) for g in groups_g]

    cmp_out = [compressed(g, cmp_scores[g]) for g in groups_g]

    wbias = jnp.where(wmask, 0.0, NEG_BIG)
    sws = [win_scores[g].reshape(NSA_REP, Q_BLOCK, WINDOW + Q_BLOCK) + wbias[None] for g in groups_g]
    pws = [jnp.exp2(sw - jnp.max(sw, axis=-1, keepdims=True)) for sw in sws]
    wins = [_dot(pws[g].reshape(rq, WINDOW + Q_BLOCK).astype(BF16), with_ones(vwin, g)).reshape(NSA_REP, Q_BLOCK, LANES)
            for g in groups_g]

    for g in groups_g:
        store_block_bias(g, select_blocks(cmp_out[g][1]))

    dmask = (lax.broadcasted_iota(jnp.int32, (Q_BLOCK, Q_BLOCK), 1)
             <= lax.broadcasted_iota(jnp.int32, (Q_BLOCK, Q_BLOCK), 0))
    dbias = jnp.where(dmask, 0.0, NEG_BIG)
    states = []
    for g in groups_g:
        s = diag_scores[g].reshape(NSA_REP, Q_BLOCK, Q_BLOCK) + dbias[None]
        m0 = jnp.max(s, axis=-1, keepdims=True)
        p = jnp.exp2(s - m0).reshape(rq, Q_BLOCK).astype(BF16)
        states.append((m0, _dot(p, with_ones(vdiag, g)).reshape(NSA_REP, Q_BLOCK, LANES)))

    def sel_step(c, states):
        scores = [sel_scores(c, g) for g in groups_g]
        return tuple(sel_update(g, c, scores[g], states[g]) for g in groups_g)

    n_past = (qb + SEL_CHUNK // Q_BLOCK - 1) // (SEL_CHUNK // Q_BLOCK)
    sel_out = lax.fori_loop(0, n_past, sel_step, tuple(states))

    heads = [(g, r) for g in groups_g for r in range(NSA_REP)]
    c_sel = [gates[:, col_of(g, r) + 1:col_of(g, r) + 2]
             * (1.0 / sel_out[g][1][r][:, col_of(g, r) + 1:col_of(g, r) + 2]) for g, r in heads]
    c_win = [gates[:, col_of(g, r) + 2:col_of(g, r) + 3]
             * (1.0 / wins[g][r][:, col_of(g, r) + 2:col_of(g, r) + 3]) for g, r in heads]
    outs = [cmp_out[g][0][r] + c_sel[i] * sel_out[g][1][r] + c_win[i] * wins[g][r]
            for i, (g, r) in enumerate(heads)]
    outs = [jnp.where(lmasks[g], o, 0.0) for (g, r), o in zip(heads, outs)]
    ones_mat = jnp.ones((LANES, LANES), BF16)

    def row_sums(sq):
        hi = sq.astype(BF16)
        lo = (sq - hi.astype(F32)).astype(BF16)
        return _dot(hi, ones_mat) + _dot(lo, ones_mat)

    mss = [row_sums(o * o) * (1.0 / NSA_HEAD_DIM) for o in outs]
    ys = [o * lax.rsqrt(ms + NORM_EPS) * ng_ref[...] for o, ms in zip(outs, mss)]
    for g in groups_g:
        for pair in range(NSA_REP // 2):
            a, bb = ys[g * NSA_REP + 2 * pair], ys[g * NSA_REP + 2 * pair + 1]
            if g == 0:
                tile = a + pltpu.roll(bb, NSA_HEAD_DIM, axis=1)
            else:
                tile = pltpu.roll(a, NSA_HEAD_DIM, axis=1) + bb
            c0 = g * NSA_REP * NSA_HEAD_DIM + pair * LANES
            o_ref[0, :, c0:c0 + LANES] = tile.astype(o_ref.dtype)


def _nsa(qpad, kcmp, vcmp, ks, vs, kw, vw, ba, ovl, ng_lane):
    b, _, t, _ = qpad.shape
    n = kcmp.shape[1]
    whole = lambda rows: pl.BlockSpec((1, rows, LANES), lambda i, j: (i, 0, 0))
    return pl.pallas_call(
        functools.partial(_nsa_body, seq=t),
        grid=(b, t // Q_BLOCK),
        in_specs=[
            pl.BlockSpec((1, NSA_HEADS, Q_BLOCK, LANES), lambda i, j: (i, 0, j, 0)),
            whole(n), whole(n),
            pl.BlockSpec((1, NSA_KV_HEADS, t, LANES), lambda i, j: (i, 0, 0, 0)),
            whole(t), whole(t), whole(t),
            pl.BlockSpec((1, Q_BLOCK, LANES), lambda i, j: (i, j, 0)),
            pl.BlockSpec((n, LANES), lambda i, j: (0, 0)),
            pl.BlockSpec((1, LANES), lambda i, j: (0, 0)),
        ],
        out_specs=pl.BlockSpec((1, Q_BLOCK, NSA_WIDTH), lambda i, j: (i, j, 0)),
        out_shape=jax.ShapeDtypeStruct((b, t, NSA_WIDTH), BF16),
        scratch_shapes=[pltpu.VMEM((NSA_KV_HEADS, t // SEL_CHUNK, Q_BLOCK, LANES), BF16)],
        compiler_params=_cparams(("arbitrary", "arbitrary")),
        name="nsa_attn",
    )(qpad, kcmp, vcmp, ks, vs, kw, vw, ba, ovl, ng_lane)


_IN_SIZES = (3 * GDN_WIDTH, GDN_WIDTH, GDN_HEADS, GDN_HEADS, NSA_WIDTH) + (NSA_KV_WIDTH,) * 6 + (3 * NSA_HEADS,)


def _prep_inproj_weight(w):
    offs = np.concatenate([[0], np.cumsum(_IN_SIZES)])
    seg = lambda i: w[:, int(offs[i]):int(offs[i + 1])]
    small = jnp.concatenate([seg(2), seg(3), seg(11)], axis=1)
    pad = lambda n: jnp.zeros((w.shape[0], n), w.dtype)
    big = jnp.concatenate([seg(0), seg(1)] + [seg(i) for i in range(4, 11)]
                          + [small, pad(_GATE_COPY - _SMALL_ROWS), seg(11),
                             pad(LANES - _GATE_COPY - 3 * NSA_HEADS)], axis=1)
    return big.astype(BF16)


def _prep_compress(pe, w1, w2):
    hid = CMP_HIDDEN
    g, dh = NSA_KV_HEADS, NSA_HEAD_DIM
    half = CMP_LEN // 2
    w1r = w1.reshape(CMP_LEN, dh, hid)
    eye = jnp.eye(g, dtype=w1.dtype)
    expand = lambda part: jnp.einsum("ldh,gk->lgdkh", part, eye).reshape(half * g * dh, g * hid)
    wa, wb = expand(w1r[:half]), expand(w1r[half:])
    pe_row = lambda part: jnp.broadcast_to(part[:, None, :], (half, g, dh)).reshape(1, half * g * dh)
    pea, peb = pe_row(pe[:half]), pe_row(pe[half:])
    w2bd = jnp.einsum("hd,gk->ghkd", w2, eye).reshape(g * hid, g * dh)
    return pea, peb, wa.astype(BF16), wb.astype(BF16), w2bd.astype(BF16)


def _overlap_matrix(n_rows, seq):
    n_cmp = (seq - CMP_LEN) // CMP_STRIDE + 1
    n_blk = seq // SEL_BLOCK
    n = np.arange(n_rows)[:, None]
    j = np.arange(LANES)[None, :]
    start, end = n * CMP_STRIDE, n * CMP_STRIDE + CMP_LEN - 1
    ovl = (start < j * SEL_BLOCK + SEL_BLOCK) & (end >= j * SEL_BLOCK) & (n < n_cmp) & (j < n_blk)
    return jnp.asarray(ovl.astype(np.float32), dtype=BF16)


def kernel(x, c, positions, ada_w, ada_b, norm_g, ffn_w_in, ffn_w_out, mix_w_in, gdn_conv_w, gdn_a_log, gdn_dt_bias, gdn_norm_g, cmp_pe_k, cmp_w1_k, cmp_w2_k, cmp_pe_v, cmp_w1_v, cmp_w2_v, nsa_norm_g, mix_w_out, final_norm_g):
    b, t, d = x.shape
    depth = ada_w.shape[0]
    assert t % SEL_CHUNK == 0 and t >= WINDOW + Q_BLOCK and b <= 8

    c_pad = jnp.zeros((8, d), F32).at[:b].set(c.astype(F32))
    mod = _ada_mod(c_pad, ada_w.astype(F32), ada_b.astype(F32))
    mod = mod[:, :b].reshape(depth, b, 3, 3, d)
    mod8 = jnp.concatenate([mod, jnp.zeros((depth, b, 3, 5, d), F32)], axis=3)

    half = ROT_DIM // 2
    inv_freq = jnp.power(ROPE_THETA, -jnp.arange(half, dtype=F32) * (2.0 / ROT_DIM))
    dim = np.arange(LANES) % NSA_HEAD_DIM
    invf_lane = jnp.where(jnp.asarray(dim < ROT_DIM), inv_freq[dim % half], 0.0).reshape(1, LANES)
    cosf, sint = _rope_tables(positions.astype(F32).reshape(b, t, 1), invf_lane, tm=1024)

    n_rows = t // CMP_STRIDE
    ovl = _overlap_matrix(n_rows, t)
    fg = final_norm_g.reshape(1, d).astype(F32)

    w_in_all = ffn_w_in.astype(BF16)
    w_out_all = ffn_w_out.astype(BF16)
    w_mix_all = mix_w_out.astype(BF16)

    gain_all = norm_g.astype(F32).reshape(depth * 3, 1, d)

    for l in range(depth):
        x = _ffn(x, mod8, gain_all, w_in_all, w_out_all, l, 0, fg, final_norm=False)

        (qkv, z, ba, bat, qpad, kc, vc, ks, vs, kw, vw) = _inproj(
            x, mod8, gain_all, l, _prep_inproj_weight(mix_w_in[l]), cosf, sint)

        neg_a = -jnp.exp(gdn_a_log[l].astype(F32))
        dtb = gdn_dt_bias[l].astype(F32)
        avec = jnp.zeros((8, LANES), F32).at[0, GDN_HEADS:2 * GDN_HEADS].set(neg_a)
        avec = avec.at[1, GDN_HEADS:2 * GDN_HEADS].set(dtb)
        avect = jnp.zeros((_SMALL_ROWS, LANES), F32).at[GDN_HEADS:2 * GDN_HEADS, 0].set(neg_a)
        avect = avect.at[GDN_HEADS:2 * GDN_HEADS, 1].set(dtb)
        y_gdn = _gdn(qkv, z, ba, bat, gdn_conv_w[l].astype(F32), avec, avect,
                     gdn_norm_g[l].reshape(1, GDN_HEAD_DIM).astype(F32))

        kcmp = _compress(kc, *_prep_compress(cmp_pe_k[l], cmp_w1_k[l], cmp_w2_k[l]))
        vcmp = _compress(vc, *_prep_compress(cmp_pe_v[l], cmp_w1_v[l], cmp_w2_v[l]))
        ng_lane = jnp.tile(nsa_norm_g[l].astype(F32), NSA_KV_HEADS).reshape(1, LANES)
        y_nsa = _nsa(qpad, kcmp, vcmp, ks, vs, kw, vw, ba, ovl, ng_lane)

        x = _ffn(x, mod8, gain_all, w_in_all, w_out_all, l, 1, fg, final_norm=(l == depth - 1),
                 mix=(y_gdn, y_nsa, w_mix_all))
    return x
```

```python
import functools
import math

import jax
import jax.numpy as jnp
import numpy as np
from jax import lax
from jax.experimental import pallas as pl
from jax.experimental.pallas import tpu as pltpu

F32 = jnp.float32
BF16 = jnp.bfloat16

NORM_EPS = 1e-6
LANES = 128
GDN_HEADS = 4
GDN_HEAD_DIM = 128
GDN_WIDTH = GDN_HEADS * GDN_HEAD_DIM
GDN_CONV = 4
GDN_CHUNK = 64
GDN_PREP_STEP = 8 * GDN_CHUNK
GDN_SCAN_STEP = 4 * GDN_CHUNK
NSA_HEADS = 8
NSA_KV_HEADS = 2
NSA_REP = NSA_HEADS // NSA_KV_HEADS
NSA_HEAD_DIM = 64
NSA_WIDTH = NSA_HEADS * NSA_HEAD_DIM
NSA_KV_WIDTH = NSA_KV_HEADS * NSA_HEAD_DIM
CMP_LEN = 32
CMP_STRIDE = 16
CMP_HIDDEN = 2 * NSA_HEAD_DIM
SEL_BLOCK = 64
N_SELECT = 16
WINDOW = 512
Q_BLOCK = 128
SEL_CHUNK = 1024
SEL_LANES = SEL_CHUNK // SEL_BLOCK


def sel_lane_base(g):
    return NSA_HEAD_DIM if g == 0 else 0
ROPE_THETA = 500000.0
ROT_DIM = NSA_HEAD_DIM // 4
N_ADA = 9
NEG_BIG = -1e30
VMEM_LIMIT = 56 * 1024 * 1024


def _cparams(sem):
    return pltpu.CompilerParams(dimension_semantics=sem, vmem_limit_bytes=VMEM_LIMIT)


def _dot(a, b):
    return jnp.dot(a, b, preferred_element_type=F32)


def _dot_nt(a, b):
    return lax.dot_general(a, b, (((1,), (1,)), ((), ())), preferred_element_type=F32)


def _split3(x):
    hi = x.astype(BF16)
    r1 = x - hi.astype(F32)
    mid = r1.astype(BF16)
    lo = (r1 - mid.astype(F32)).astype(BF16)
    return hi, mid, lo


def _sigmoid(x):
    return 1.0 / (1.0 + jnp.exp(-x))


def _silu(x):
    return x * _sigmoid(x)


def _rms_rows(x, g):
    ms = jnp.mean(x * x, axis=-1, keepdims=True)
    return x * lax.rsqrt(ms + NORM_EPS) * g


def _ada_body(c_ref, w_ref, b_ref, o_ref):
    cond = _silu(c_ref[...])
    hi, mid, lo = [t.astype(F32) for t in _split3(cond)]
    rows = cond.shape[0]
    w = w_ref[0]
    w_hi = w.astype(BF16)
    w_lo = (w - w_hi.astype(F32)).astype(BF16)
    first = _dot(jnp.concatenate([hi, mid, lo, jnp.zeros_like(hi)], axis=0).astype(BF16), w_hi)
    second = _dot(jnp.concatenate([hi, mid], axis=0).astype(BF16), w_lo)
    o_ref[0] = ((first[:rows] + first[rows:2 * rows]) + (first[2 * rows:3 * rows] + second[:rows])
                + second[rows:]) + b_ref[0]


def _ada_mod(c_pad, ada_w, ada_b):
    depth, d, n = ada_w.shape
    tn = n // N_ADA
    rows = c_pad.shape[0]
    return pl.pallas_call(
        _ada_body,
        grid=(depth, n // tn),
        in_specs=[
            pl.BlockSpec((rows, d), lambda l, j: (0, 0)),
            pl.BlockSpec((1, d, tn), lambda l, j: (l, 0, j)),
            pl.BlockSpec((1, 1, tn), lambda l, j: (l, 0, j)),
        ],
        out_specs=pl.BlockSpec((1, rows, tn), lambda l, j: (l, 0, j)),
        out_shape=jax.ShapeDtypeStruct((depth, rows, n), F32),
        compiler_params=_cparams(("arbitrary", "arbitrary")),
        name="ada_mod",
    )(c_pad, ada_w, ada_b.reshape(depth, 1, n))


def _rope_body(pos_ref, invf_ref, cos_ref, sin_ref):
    ang = pos_ref[0] * invf_ref[...]
    cos_ref[0] = jnp.cos(ang)
    sin_ref[0] = jnp.sin(ang)


def _rope_tables(pos_f, invf_lane, tm):
    b, t, _ = pos_f.shape
    spec = pl.BlockSpec((1, tm, LANES), lambda i, j: (i, j, 0))
    return pl.pallas_call(
        _rope_body,
        grid=(b, t // tm),
        in_specs=[pl.BlockSpec((1, tm, 1), lambda i, j: (i, j, 0)),
                  pl.BlockSpec((1, LANES), lambda i, j: (0, 0))],
        out_specs=[spec, spec],
        out_shape=[jax.ShapeDtypeStruct((b, t, LANES), F32)] * 2,
        compiler_params=_cparams(("arbitrary", "arbitrary")),
        name="rope_tables",
    )(pos_f, invf_lane)


def _ffn_body(*refs, final_norm, with_mix, tf):
    if with_mix:
        (x_ref, mod_ref, g_ref, wi_ref, wo_ref, fg_ref,
         mmod_ref, yg_ref, yn_ref, wm_ref, o_ref, act_scr) = refs
        mix = _dot(yg_ref[0], wm_ref[0:GDN_WIDTH, :]) + _dot(yn_ref[0], wm_ref[GDN_WIDTH:, :])
        x = x_ref[0] + mmod_ref[0, 2:3, :] * mix
    else:
        x_ref, mod_ref, g_ref, wi_ref, wo_ref, fg_ref, o_ref, act_scr = refs
        x = x_ref[0]
    y = _rms_rows(x, g_ref[...])
    h = (y * (1.0 + mod_ref[0, 1:2, :]) + mod_ref[0, 0:1, :]).astype(BF16)
    ff = wo_ref.shape[0]
    for j in range(ff // tf):
        a = _dot(h, wi_ref[:, j * tf:(j + 1) * tf])
        u = _dot(h, wi_ref[:, ff + j * tf:ff + (j + 1) * tf])
        act_scr[:, j * tf:(j + 1) * tf] = (_silu(a) * u).astype(BF16)
    out = x + (0.5 * mod_ref[0, 2:3, :]) * _dot(act_scr[...], wo_ref[...])
    if final_norm:
        out = _rms_rows(out, fg_ref[...])
    o_ref[0] = out


def _mod_spec(d, layer, sub):
    return pl.BlockSpec((None, 1, None, 8, d), lambda i, j: (layer, i, sub, 0, 0))


def _gain_spec(d, layer, sub):
    return pl.BlockSpec((None, 1, d), lambda i, j: (layer * 3 + sub, 0, 0))


def _ffn(x, mod_all, gain_all, w_in_all, w_out_all, layer, idx, fg, *, final_norm, mix=None, tm=1024, tf=256):
    b, t, d = x.shape
    ff = w_out_all.shape[2]
    sub = 2 * idx
    row = lambda n: pl.BlockSpec((1, tm, n), lambda i, j: (i, j, 0))
    vec_spec = pl.BlockSpec((1, d), lambda i, j: (0, 0))
    stacked = lambda r, c, *lead: pl.BlockSpec((None,) * len(lead) + (r, c), lambda i, j: lead + (0, 0),
                                               pipeline_mode=pl.Buffered(1))
    in_specs = [row(d), _mod_spec(d, layer, sub), _gain_spec(d, layer, sub),
                stacked(d, 2 * ff, layer, idx), stacked(ff, d, layer, idx), vec_spec]
    args = [x, mod_all, gain_all, w_in_all, w_out_all, fg]
    if mix is not None:
        yg, yn, w_mix_all = mix
        in_specs += [_mod_spec(d, layer, 1), row(GDN_WIDTH), row(NSA_WIDTH), stacked(d, d, layer)]
        args += [mod_all, yg, yn, w_mix_all]
    return pl.pallas_call(
        functools.partial(_ffn_body, final_norm=final_norm, with_mix=mix is not None, tf=tf),
        grid=(b, t // tm),
        in_specs=in_specs,
        out_specs=row(d),
        out_shape=jax.ShapeDtypeStruct((b, t, d), F32),
        scratch_shapes=[pltpu.VMEM((tm, ff), BF16)],
        compiler_params=_cparams(("arbitrary", "arbitrary")),
        name=("ffn_mix" if mix is not None else "ffn") + ("_final" if final_norm else ""),
    )(*args)


_O_QKV, _O_Z, _O_Q = 0, 1536, 2048
_O_KC, _O_VC, _O_KS, _O_VS, _O_KW, _O_VW, _O_SM = 2560, 2688, 2816, 2944, 3072, 3200, 3328
_IN_COLS = 3456
_SMALL_ROWS = 32
_GATE_COPY = NSA_HEAD_DIM + 2 * GDN_HEADS


def _rope_tile(x, cosf, sint, first_half):
    fwd = pltpu.roll(x, 8, axis=1)
    bwd = pltpu.roll(x, LANES - 8, axis=1)
    return x * cosf + jnp.where(first_half, -bwd, fwd) * sint


def _inproj_body(x_ref, mod_ref, g_ref, w_ref, cos_ref, sin_ref,
                 qkv_ref, z_ref, ba_ref, bat_ref, qpad_ref,
                 kc_ref, vc_ref, ks_ref, vs_ref, kw_ref, vw_ref):
    y = _rms_rows(x_ref[0], g_ref[...])
    h = (y * (1.0 + mod_ref[0, 1:2, :]) + mod_ref[0, 0:1, :]).astype(BF16)
    cosf = cos_ref[0]
    sint = sin_ref[0]
    lane = lax.broadcasted_iota(jnp.int32, cosf.shape, 1)
    first_half = (lane % NSA_HEAD_DIM) < (ROT_DIM // 2)
    low = lane < NSA_HEAD_DIM

    for j in range(3):
        qkv_ref[0, :, j * 512:(j + 1) * 512] = _dot(h, w_ref[:, _O_QKV + j * 512:_O_QKV + (j + 1) * 512])
    z_ref[0] = _dot(h, w_ref[:, _O_Z:_O_Z + 512])
    ba = _dot(h, w_ref[:, _O_SM:_O_SM + LANES])
    ba_ref[0] = ba
    bat_ref[0] = ba.T[:_SMALL_ROWS, :]

    scale = NSA_HEAD_DIM ** -0.5 * math.log2(math.e)
    def pair(off):
        res = _dot(h, w_ref[:, off:off + 2 * LANES])
        return res[:, :LANES], res[:, LANES:]

    q_tiles = pair(_O_Q) + pair(_O_Q + 2 * LANES)
    for k in range(NSA_HEADS // 2):
        tile = _rope_tile(q_tiles[k], cosf, sint, first_half) * scale
        swapped = pltpu.roll(tile, NSA_HEAD_DIM, axis=1)
        grp = (2 * k) // NSA_REP
        if grp == 0:
            even = jnp.where(low, tile, 0.0)
            odd = jnp.where(low, swapped, 0.0)
        else:
            even = jnp.where(low, 0.0, swapped)
            odd = jnp.where(low, 0.0, tile)
        qpad_ref[0, 2 * k] = even.astype(BF16)
        qpad_ref[0, 2 * k + 1] = odd.astype(BF16)

    kc, vc = pair(_O_KC)
    kc_ref[0] = _rope_tile(kc, cosf, sint, first_half)
    vc_ref[0] = vc
    ks, vs = pair(_O_KS)
    ks = _rope_tile(ks, cosf, sint, first_half)
    tok = pl.program_id(1) * ks.shape[0] + lax.broadcasted_iota(jnp.int32, ks.shape, 0)
    blk = lax.shift_right_logical(tok, 6) & (SEL_LANES - 1)
    for grp in range(NSA_KV_HEADS):
        onehot = jnp.where(lane - sel_lane_base(grp) == blk, 1.0, 0.0)
        mixed = jnp.where(low, ks, onehot) if grp == 0 else jnp.where(low, onehot, ks)
        ks_ref[0, grp] = mixed.astype(BF16)
    vs_ref[0] = vs.astype(BF16)
    kw, vw = pair(_O_KW)
    kw_ref[0] = _rope_tile(kw, cosf, sint, first_half).astype(BF16)
    vw_ref[0] = vw.astype(BF16)


def _inproj(x, mod_all, gain_all, layer, w, cosf, sint, tm=512):
    b, t, d = x.shape
    row = lambda n: pl.BlockSpec((1, tm, n), lambda i, j: (i, j, 0))
    sds = lambda n, dt: jax.ShapeDtypeStruct((b, t, n), dt)
    return pl.pallas_call(
        _inproj_body,
        grid=(b, t // tm),
        in_specs=[
            row(d),
            _mod_spec(d, layer, 1),
            _gain_spec(d, layer, 1),
            pl.BlockSpec((d, _IN_COLS), lambda i, j: (0, 0)),
            row(LANES), row(LANES),
        ],
        out_specs=[
            row(3 * GDN_WIDTH), row(GDN_WIDTH), row(LANES),
            pl.BlockSpec((1, _SMALL_ROWS, tm), lambda i, j: (i, 0, j)),
            pl.BlockSpec((1, NSA_HEADS, tm, LANES), lambda i, j: (i, 0, j, 0)),
            row(LANES), row(LANES),
            pl.BlockSpec((1, NSA_KV_HEADS, tm, LANES), lambda i, j: (i, 0, j, 0)),
            row(LANES), row(LANES), row(LANES),
        ],
        out_shape=[
            sds(3 * GDN_WIDTH, F32), sds(GDN_WIDTH, F32), sds(LANES, F32),
            jax.ShapeDtypeStruct((b, _SMALL_ROWS, t), F32),
            jax.ShapeDtypeStruct((b, NSA_HEADS, t, LANES), BF16),
            sds(LANES, F32), sds(LANES, F32),
            jax.ShapeDtypeStruct((b, NSA_KV_HEADS, t, LANES), BF16),
            sds(LANES, BF16), sds(LANES, BF16), sds(LANES, BF16),
        ],
        compiler_params=_cparams(("arbitrary", "arbitrary")),
        name="mix_inproj",
    )(x, mod_all, gain_all, w, cosf, sint)


def _softplus(x):
    return jnp.maximum(x, 0.0) + jnp.log1p(jnp.exp(-jnp.abs(x)))


def _dot_inv(a, b):
    ah = a.astype(BF16)
    al = (a - ah.astype(F32)).astype(BF16)
    bh = b.astype(BF16)
    bl = (b - bh.astype(F32)).astype(BF16)
    m = a.shape[0]
    top = _dot(jnp.concatenate([ah, al], axis=0), bh)
    return top[:m] + top[m:] + _dot(ah, bl)


def _gdn_prep_body(qkv_ref, ba_ref, bat_ref, cw_ref, av_ref, avt_ref,
                   u_ref, wq_ref, ak_ref, egl_ref, buf):
    step = pl.program_id(1)
    c64 = GDN_CHUNK
    hd = GDN_HEAD_DIM
    nt = GDN_PREP_STEP

    @pl.when(step == 0)
    def _():
        buf[0:8, :] = jnp.zeros((8, 3 * GDN_WIDTH), F32)

    x = qkv_ref[0]
    buf[8:8 + nt, :] = x
    y = x * cw_ref[3:4, :]
    for j in range(GDN_CONV - 1):
        sh = GDN_CONV - 1 - j
        y = y + buf[8 - sh:8 - sh + nt, :] * cw_ref[j:j + 1, :]
    buf[0:8, :] = x[nt - 8:nt, :]
    y = _silu(y)

    ba = ba_ref[0]
    beta_all = _sigmoid(ba)
    g_col_all = av_ref[0:1, :] * _softplus(ba + av_ref[1:2, :])
    bat = bat_ref[0]
    g_row_all = avt_ref[:, 0:1] * _softplus(bat + avt_ref[:, 1:2])

    pw = 2 * c64
    rr = lax.broadcasted_iota(jnp.int32, (pw, pw), 0)
    kk = lax.broadcasted_iota(jnp.int32, (pw, pw), 1)
    same = lax.shift_right_logical(rr, 6) == lax.shift_right_logical(kk, 6)
    causal = same & (rr >= kk)
    strict = same & (rr > kk)
    tri = jnp.where(causal, 1.0, 0.0)
    tri_t = jnp.where(same & (rr <= kk), 1.0, 0.0)
    first = lax.broadcasted_iota(jnp.int32, (pw, 1), 0) < c64

    npair = nt // pw
    units = [(pc, h) for pc in range(npair) for h in range(GDN_HEADS)]
    rows_of = lambda pc: slice(pc * pw, (pc + 1) * pw)
    tri_b, tri_tb = tri.astype(BF16), tri_t.astype(BF16)
    gc_cols = [sum(_dot(tri_b, part) for part in _split3(g_col_all[rows_of(pc), :]))
               for pc in range(npair)]
    gc_rows = [sum(_dot(part, tri_tb) for part in _split3(g_row_all[:, rows_of(pc)]))
               for pc in range(npair)]

    def l2n(v):
        return v * lax.rsqrt(jnp.sum(v * v, axis=-1, keepdims=True) + NORM_EPS)

    def front(pc, h):
        kh = l2n(y[rows_of(pc), GDN_WIDTH + h * hd:GDN_WIDTH + (h + 1) * hd])
        gcol = gc_cols[pc][:, GDN_HEADS + h:GDN_HEADS + h + 1]
        grow = gc_rows[pc][GDN_HEADS + h:GDN_HEADS + h + 1, :]
        decay = jnp.exp(jnp.where(causal, gcol - grow, -jnp.inf))
        kb = kh * beta_all[rows_of(pc), h:h + 1]
        low = jnp.where(strict, _dot_nt(kb.astype(BF16), kh.astype(BF16)) * decay, 0.0)
        return kh, kb, decay, low

    egl_rows = [[] for _ in range(nt // c64)]

    def back(pc, h, kh, kb, decay, inv):
        rows = rows_of(pc)
        qh = l2n(y[rows, h * hd:(h + 1) * hd]) * (hd ** -0.5)
        vh = y[rows, 2 * GDN_WIDTH + h * hd:2 * GDN_WIDTH + (h + 1) * hd]
        gcol = gc_cols[pc][:, GDN_HEADS + h:GDN_HEADS + h + 1]
        grow = gc_rows[pc][GDN_HEADS + h:GDN_HEADS + h + 1, :]
        glasts = [grow[:, c64 - 1:c64], grow[:, pw - 1:pw]]
        glast = jnp.where(first, glasts[0], glasts[1])
        eg = jnp.exp(gcol)
        rhs = jnp.concatenate([vh * beta_all[rows, h:h + 1], kb * eg], axis=1)
        sol = _dot_inv(inv, rhs)
        attn = _dot_nt(qh.astype(BF16), kh.astype(BF16)) * decay
        kd_t = (kh * jnp.exp(glast - gcol)).T
        qe = (qh * eg).astype(BF16)
        u_ref[0, h, rows, :] = sol[:, :hd]
        for half in range(2):
            ch = 2 * pc + half
            part = slice(half * c64, (half + 1) * c64)
            wq_ref[0, h, ch, 0:c64, :] = sol[part, hd:].astype(BF16)
            wq_ref[0, h, ch, c64:2 * c64, :] = qe[part, :]
            ak_ref[0, h, ch, 0:c64, :] = attn[part, part].astype(BF16)
            ak_ref[0, h, ch, c64:c64 + hd, :] = kd_t[:, part].astype(BF16)
            egl_rows[ch].append(jnp.broadcast_to(jnp.exp(glasts[half]), (1, LANES)))

    eye = jnp.where(rr == kk, 1.0, 0.0)
    levels = int(math.log2(c64)) - 1
    waves = [units[:len(units) // 2], units[len(units) // 2:]]

    def spread(items, n):
        q, r = divmod(len(items), n)
        bounds = [i * q + min(i, r) for i in range(n + 1)]
        return [items[bounds[i]:bounds[i + 1]] for i in range(n)]

    def inverse_level(state):
        state = [(_dot_inv(p, p), inv) for p, inv in state]
        return [(p, inv + _dot_inv(inv, p)) for p, inv in state]

    fronts = [[front(pc, h) for pc, h in waves[0]], []]
    state = [(-f[3], eye - f[3]) for f in fronts[0]]
    for group in spread(waves[1], levels):
        state = inverse_level(state)
        fronts[1] += [front(pc, h) for pc, h in group]
    inv_first = [inv for _, inv in state]
    state = [(-f[3], eye - f[3]) for f in fronts[1]]
    for group in spread(list(range(len(waves[0]))), levels):
        state = inverse_level(state)
        for i in group:
            back(*waves[0][i], *fronts[0][i][:3], inv_first[i])
    for i, unit in enumerate(waves[1]):
        back(*unit, *fronts[1][i][:3], state[i][1])
    for ch in range(nt // c64):
        egl_ref[0, ch] = jnp.concatenate(egl_rows[ch] + [jnp.zeros((8 - GDN_HEADS, LANES), F32)], axis=0)


def _gdn_prep(qkv, ba, bat, conv_w, avec, avect):
    b, t, _ = qkv.shape
    nt = GDN_PREP_STEP
    nch = nt // GDN_CHUNK
    hd = GDN_HEAD_DIM
    row = lambda n: pl.BlockSpec((1, nt, n), lambda i, j: (i, j, 0))
    return pl.pallas_call(
        _gdn_prep_body,
        grid=(b, t // nt),
        in_specs=[
            row(3 * GDN_WIDTH), row(LANES),
            pl.BlockSpec((1, _SMALL_ROWS, nt), lambda i, j: (i, 0, j)),
            pl.BlockSpec((GDN_CONV, 3 * GDN_WIDTH), lambda i, j: (0, 0)),
            pl.BlockSpec((8, LANES), lambda i, j: (0, 0)),
            pl.BlockSpec((_SMALL_ROWS, LANES), lambda i, j: (0, 0)),
        ],
        out_specs=[
            pl.BlockSpec((1, GDN_HEADS, nt, hd), lambda i, j: (i, 0, j, 0)),
            pl.BlockSpec((1, GDN_HEADS, nch, 2 * GDN_CHUNK, hd), lambda i, j: (i, 0, j, 0, 0)),
            pl.BlockSpec((1, GDN_HEADS, nch, GDN_CHUNK + hd, GDN_CHUNK), lambda i, j: (i, 0, j, 0, 0)),
            pl.BlockSpec((1, nch, 8, LANES), lambda i, j: (i, j, 0, 0)),
        ],
        out_shape=[
            jax.ShapeDtypeStruct((b, GDN_HEADS, t, hd), F32),
            jax.ShapeDtypeStruct((b, GDN_HEADS, t // GDN_CHUNK, 2 * GDN_CHUNK, hd), BF16),
            jax.ShapeDtypeStruct((b, GDN_HEADS, t // GDN_CHUNK, GDN_CHUNK + hd, GDN_CHUNK), BF16),
            jax.ShapeDtypeStruct((b, t // GDN_CHUNK, 8, LANES), F32),
        ],
        scratch_shapes=[pltpu.VMEM((8 + nt, 3 * GDN_WIDTH), F32)],
        compiler_params=_cparams(("arbitrary", "arbitrary")),
        name="gdn_prep",
    )(qkv, ba, bat, conv_w, avec, avect)


def _gdn_scan_body(u_ref, wq_ref, ak_ref, egl_ref, z_ref, ng_ref, o_ref, s_scr):
    c64 = GDN_CHUNK
    hd = GDN_HEAD_DIM
    nb = u_ref.shape[0]

    @pl.when(pl.program_id(0) == 0)
    def _():
        s_scr[...] = jnp.zeros_like(s_scr)

    chains = [(b, h) for b in range(nb) for h in range(GDN_HEADS)]
    states = [s_scr[b, h] for b, h in chains]
    for ch in range(GDN_SCAN_STEP // c64):
        rows = slice(ch * c64, (ch + 1) * c64)
        r1s = [_dot(wq_ref[b, h, ch], s.astype(BF16)) for (b, h), s in zip(chains, states)]
        vns = [(u_ref[b, h, rows, :] - r1[0:c64]).astype(BF16) for (b, h), r1 in zip(chains, r1s)]
        r2s = [_dot(ak_ref[b, h, ch], vn) for (b, h), vn in zip(chains, vns)]
        states = [s * egl_ref[b, ch, h:h + 1, :] + r2[c64:c64 + hd]
                  for (b, h), s, r2 in zip(chains, states, r2s)]
        for (b, h), r1, r2 in zip(chains, r1s, r2s):
            on = _rms_rows(r1[c64:2 * c64] + r2[0:c64], ng_ref[...])
            zh = z_ref[b, rows, h * hd:(h + 1) * hd]
            o_ref[b, rows, h * hd:(h + 1) * hd] = (on * _silu(zh)).astype(o_ref.dtype)
    for (b, h), s in zip(chains, states):
        s_scr[b, h] = s


def _gdn_scan(u, wq, ak, egl, z, norm_g):
    b, _, t, hd = u.shape
    nt = GDN_SCAN_STEP
    nch = nt // GDN_CHUNK
    return pl.pallas_call(
        _gdn_scan_body,
        grid=(t // nt,),
        in_specs=[
            pl.BlockSpec((b, GDN_HEADS, nt, hd), lambda j: (0, 0, j, 0)),
            pl.BlockSpec((b, GDN_HEADS, nch, 2 * GDN_CHUNK, hd), lambda j: (0, 0, j, 0, 0)),
            pl.BlockSpec((b, GDN_HEADS, nch, GDN_CHUNK + hd, GDN_CHUNK), lambda j: (0, 0, j, 0, 0)),
            pl.BlockSpec((b, nch, 8, LANES), lambda j: (0, j, 0, 0)),
            pl.BlockSpec((b, nt, GDN_WIDTH), lambda j: (0, j, 0)),
            pl.BlockSpec((1, hd), lambda j: (0, 0)),
        ],
        out_specs=pl.BlockSpec((b, nt, GDN_WIDTH), lambda j: (0, j, 0)),
        out_shape=jax.ShapeDtypeStruct((b, t, GDN_WIDTH), BF16),
        scratch_shapes=[pltpu.VMEM((b, GDN_HEADS, hd, hd), F32)],
        compiler_params=_cparams(("arbitrary",)),
        name="gdn_scan",
    )(u, wq, ak, egl, z, norm_g)


def _gdn(qkv, z, ba, bat, conv_w, avec, avect, norm_g):
    u, wq, ak, egl = _gdn_prep(qkv, ba, bat, conv_w, avec, avect)
    return _gdn_scan(u, wq, ak, egl, z, norm_g)


def _cmp_body(x_ref, pea_ref, peb_ref, wa_ref, wb_ref, w2_ref, o_ref):
    n_rows = x_ref.shape[1] // CMP_STRIDE
    x = jnp.concatenate([x_ref[0, pl.ds(l, n_rows, stride=CMP_STRIDE), :] for l in range(CMP_STRIDE)], axis=1)
    a = _dot((x + pea_ref[...]).astype(BF16), wa_ref[...])
    bm = _dot((x + peb_ref[...]).astype(BF16), wb_ref[...])
    n = a.shape[0]
    h1 = a + pltpu.roll(bm, n - 1, axis=0)
    o_ref[0] = _dot(_silu(h1).astype(BF16), w2_ref[...]).astype(o_ref.dtype)


def _compress(x, pea, peb, wa, wb, w2bd):
    b, t, wdt = x.shape
    n = t // CMP_STRIDE
    full = lambda s: pl.BlockSpec(s, lambda i: (0,) * len(s))
    return pl.pallas_call(
        _cmp_body,
        grid=(b,),
        in_specs=[pl.BlockSpec((1, t, wdt), lambda i: (i, 0, 0)),
                  full(pea.shape), full(peb.shape), full(wa.shape), full(wb.shape), full(w2bd.shape)],
        out_specs=pl.BlockSpec((1, n, LANES), lambda i: (i, 0, 0)),
        out_shape=jax.ShapeDtypeStruct((b, n, LANES), BF16),
        compiler_params=_cparams(("arbitrary",)),
        name="nsa_compress",
    )(x, pea, peb, wa, wb, w2bd)


def _nsa_body(q_ref, kcmp_ref, vcmp_ref, ks_ref, vs_ref, kw_ref, vw_ref, gate_ref,
              ovl_ref, ng_ref, o_ref, qbias_scr, *, seq):
    qb = pl.program_id(1)
    s0 = qb * Q_BLOCK
    n_cmp_rows = kcmp_ref.shape[1]
    n_blk = seq // SEL_BLOCK
    rq = NSA_REP * Q_BLOCK

    t_q = s0 + lax.broadcasted_iota(jnp.int32, (Q_BLOCK, 1), 0)
    lane = lax.broadcasted_iota(jnp.int32, (Q_BLOCK, LANES), 1)
    gates = _sigmoid(gate_ref[0])

    n_iota = lax.broadcasted_iota(jnp.int32, (Q_BLOCK, n_cmp_rows), 1)
    cmask = (n_iota * CMP_STRIDE + (CMP_LEN - 1)) <= t_q

    wstart = pl.multiple_of(jnp.maximum(s0 - WINDOW, 0), Q_BLOCK)
    kp = wstart + lax.broadcasted_iota(jnp.int32, (Q_BLOCK, WINDOW + Q_BLOCK), 1)
    wmask = (kp <= t_q) & (kp > t_q - WINDOW)

    cur = lax.shift_right_logical(t_q, 6)
    forced = (lane == 0) | (lane == cur) | (lane == cur - 1)
    valid = (lane * SEL_BLOCK <= t_q) & (lane < n_blk)
    jrow8 = lax.broadcasted_iota(jnp.int32, (8, Q_BLOCK), 0)

    groups_g = range(NSA_KV_HEADS)
    qs = [q_ref[0, g * NSA_REP:(g + 1) * NSA_REP].reshape(rq, LANES) for g in groups_g]
    lmasks = [(lane >= g * NSA_HEAD_DIM) & (lane < (g + 1) * NSA_HEAD_DIM) for g in groups_g]
    kv_lane = lax.broadcasted_iota(jnp.int32, (1, LANES), 1)
    own = [(kv_lane >= g * NSA_HEAD_DIM) & (kv_lane < (g + 1) * NSA_HEAD_DIM) for g in groups_g]

    def with_ones(v, g):
        return jnp.where(own[g], v, jnp.ones_like(v))

    col_of = lambda g, r: (_GATE_COPY if g == 0 else 2 * GDN_HEADS) + (g * NSA_REP + r) * 3

    def compressed(g, sc):
        sc = jnp.where(cmask[None], sc.reshape(NSA_REP, Q_BLOCK, n_cmp_rows), -jnp.inf)
        m = jnp.max(sc, axis=-1, keepdims=True)
        m = jnp.where(m == -jnp.inf, 0.0, m)
        p = jnp.exp2(sc - m)
        inv = 1.0 / jnp.maximum(jnp.sum(p, axis=-1, keepdims=True), 1e-30)
        gate = jnp.stack([gates[:, col_of(g, r):col_of(g, r) + 1] for r in range(NSA_REP)], axis=0)
        o_cmp = _dot((p * (inv * gate)).reshape(rq, n_cmp_rows).astype(BF16), vcmp_ref[0])
        p = p * inv
        psum = p[0] + p[1] + p[2] + p[3]
        hi, mid, lo = _split3(psum)
        ovl = ovl_ref[...]
        imp = _dot(hi, ovl) + _dot(mid, ovl) + _dot(lo, ovl)
        return o_cmp.reshape(NSA_REP, Q_BLOCK, LANES), imp

    def select_blocks(imp):
        key = jnp.where(forced, jnp.inf, jnp.where(valid, imp, -jnp.inf))
        key_t = key.T[:n_blk, :]
        groups = [key_t[8 * v:8 * v + 8, :] for v in range(n_blk // 8)]
        cnts = [jnp.zeros((8, Q_BLOCK), F32) for _ in groups]
        for i in range(n_blk):
            row = key_t[i:i + 1, :]
            for v, grp in enumerate(groups):
                if v < i // 8:
                    beats = jnp.where(row > grp, 1.0, 0.0)
                elif v > i // 8:
                    beats = jnp.where(row >= grp, 1.0, 0.0)
                else:
                    beats = jnp.where(jrow8 > i % 8, jnp.where(row >= grp, 1.0, 0.0),
                                      jnp.where(row > grp, 1.0, 0.0))
                cnts[v] = cnts[v] + beats
        sel_t = jnp.where(jnp.concatenate(cnts, axis=0) < float(N_SELECT), 1.0, 0.0)
        if n_blk < LANES:
            sel_t = jnp.concatenate([sel_t, jnp.zeros((LANES - n_blk, Q_BLOCK), F32)], axis=0)
        return sel_t.T

    def store_block_bias(g, sel):
        past = lane < 2 * qb
        bias = jnp.where(past, (sel - 1.0) * (-NEG_BIG), NEG_BIG)
        base = sel_lane_base(g)
        here = (lane >= base) & (lane < base + SEL_LANES)
        for c in range(seq // SEL_CHUNK):
            moved = pltpu.roll(bias, (base - c * SEL_LANES) % LANES, axis=1)
            qbias_scr[g, c] = jnp.where(here, moved, 0.0).astype(BF16)

    def sel_scores(c, g):
        k0 = pl.multiple_of(c * SEL_CHUNK, SEL_CHUNK)
        lhs = (q_ref[0, g * NSA_REP:(g + 1) * NSA_REP] + qbias_scr[g, c][None]).reshape(rq, LANES)
        return _dot_nt(lhs, ks_ref[0, g, pl.ds(k0, SEL_CHUNK), :])

    def sel_update(g, c, scores, state):
        k0 = pl.multiple_of(c * SEL_CHUNK, SEL_CHUNK)
        vch = vs_ref[0, pl.ds(k0, SEL_CHUNK), :]
        m_old, acc = state
        s = scores.reshape(NSA_REP, Q_BLOCK, SEL_CHUNK)
        m_new = jnp.maximum(m_old, jnp.max(s, axis=-1, keepdims=True))
        p = jnp.exp2(s - m_new)
        pv = _dot(p.reshape(rq, SEL_CHUNK).astype(BF16), with_ones(vch, g)).reshape(NSA_REP, Q_BLOCK, LANES)
        return m_new, jnp.exp2(m_old - m_new) * acc + pv

    kwin = kw_ref[0, pl.ds(wstart, WINDOW + Q_BLOCK), :]
    vwin = vw_ref[0, pl.ds(wstart, WINDOW + Q_BLOCK), :]
    cmp_scores = [_dot_nt(qs[g], kcmp_ref[0]) for g in groups_g]
    win_scores = [_dot_nt(qs[g], kwin) for g in groups_g]
    sdiag = pl.multiple_of(s0, Q_BLOCK)
    vdiag = vs_ref[0, pl.ds(sdiag, Q_BLOCK), :]
    diag_scores = [_dot_nt(qs[g], ks_ref[0, g, pl.ds(sdiag, Q_BLOCK), :]) for g in groups_g]

    cmp_out = [compressed(g, cmp_scores[g]) for g in groups_g]

    wbias = jnp.where(wmask, 0.0, NEG_BIG)
    sws = [win_scores[g].reshape(NSA_REP, Q_BLOCK, WINDOW + Q_BLOCK) + wbias[None] for g in groups_g]
    pws = [jnp.exp2(sw - jnp.max(sw, axis=-1, keepdims=True)) for sw in sws]
    wins = [_dot(pws[g].reshape(rq, WINDOW + Q_BLOCK).astype(BF16), with_ones(vwin, g)).reshape(NSA_REP, Q_BLOCK, LANES)
            for g in groups_g]

    for g in groups_g:
        store_block_bias(g, select_blocks(cmp_out[g][1]))

    dmask = (lax.broadcasted_iota(jnp.int32, (Q_BLOCK, Q_BLOCK), 1)
             <= lax.broadcasted_iota(jnp.int32, (Q_BLOCK, Q_BLOCK), 0))
    dbias = jnp.where(dmask, 0.0, NEG_BIG)
    states = []
    for g in groups_g:
        s = diag_scores[g].reshape(NSA_REP, Q_BLOCK, Q_BLOCK) + dbias[None]
        m0 = jnp.max(s, axis=-1, keepdims=True)
        p = jnp.exp2(s - m0).reshape(rq, Q_BLOCK).astype(BF16)
        states.append((m0, _dot(p, with_ones(vdiag, g)).reshape(NSA_REP, Q_BLOCK, LANES)))

    def sel_step(c, states):
        scores = [sel_scores(c, g) for g in groups_g]
        return tuple(sel_update(g, c, scores[g], states[g]) for g in groups_g)

    n_past = (qb + SEL_CHUNK // Q_BLOCK - 1) // (SEL_CHUNK // Q_BLOCK)
    sel_out = lax.fori_loop(0, n_past, sel_step, tuple(states))

    heads = [(g, r) for g in groups_g for r in range(NSA_REP)]
    c_sel = [gates[:, col_of(g, r) + 1:col_of(g, r) + 2]
             * (1.0 / sel_out[g][1][r][:, col_of(g, r) + 1:col_of(g, r) + 2]) for g, r in heads]
    c_win = [gates[:, col_of(g, r) + 2:col_of(g, r) + 3]
             * (1.0 / wins[g][r][:, col_of(g, r) + 2:col_of(g, r) + 3]) for g, r in heads]
    outs = [cmp_out[g][0][r] + c_sel[i] * sel_out[g][1][r] + c_win[i] * wins[g][r]
            for i, (g, r) in enumerate(heads)]
    outs = [jnp.where(lmasks[g], o, 0.0) for (g, r), o in zip(heads, outs)]
    ones_mat = jnp.ones((LANES, LANES), BF16)

    def row_sums(sq):
        hi = sq.astype(BF16)
        lo = (sq - hi.astype(F32)).astype(BF16)
        return _dot(hi, ones_mat) + _dot(lo, ones_mat)

    mss = [row_sums(o * o) * (1.0 / NSA_HEAD_DIM) for o in outs]
    ys = [o * lax.rsqrt(ms + NORM_EPS) * ng_ref[...] for o, ms in zip(outs, mss)]
    for g in groups_g:
        for pair in range(NSA_REP // 2):
            a, bb = ys[g * NSA_REP + 2 * pair], ys[g * NSA_REP + 2 * pair + 1]
            if g == 0:
                tile = a + pltpu.roll(bb, NSA_HEAD_DIM, axis=1)
            else:
                tile = pltpu.roll(a, NSA_HEAD_DIM, axis=1) + bb
            c0 = g * NSA_REP * NSA_HEAD_DIM + pair * LANES
            o_ref[0, :, c0:c0 + LANES] = tile.astype(o_ref.dtype)


def _nsa(qpad, kcmp, vcmp, ks, vs, kw, vw, ba, ovl, ng_lane):
    b, _, t, _ = qpad.shape
    n = kcmp.shape[1]
    whole = lambda rows: pl.BlockSpec((1, rows, LANES), lambda i, j: (i, 0, 0))
    return pl.pallas_call(
        functools.partial(_nsa_body, seq=t),
        grid=(b, t // Q_BLOCK),
        in_specs=[
            pl.BlockSpec((1, NSA_HEADS, Q_BLOCK, LANES), lambda i, j: (i, 0, j, 0)),
            whole(n), whole(n),
            pl.BlockSpec((1, NSA_KV_HEADS, t, LANES), lambda i, j: (i, 0, 0, 0)),
            whole(t), whole(t), whole(t),
            pl.BlockSpec((1, Q_BLOCK, LANES), lambda i, j: (i, j, 0)),
            pl.BlockSpec((n, LANES), lambda i, j: (0, 0)),
            pl.BlockSpec((1, LANES), lambda i, j: (0, 0)),
        ],
        out_specs=pl.BlockSpec((1, Q_BLOCK, NSA_WIDTH), lambda i, j: (i, j, 0)),
        out_shape=jax.ShapeDtypeStruct((b, t, NSA_WIDTH), BF16),
        scratch_shapes=[pltpu.VMEM((NSA_KV_HEADS, t // SEL_CHUNK, Q_BLOCK, LANES), BF16)],
        compiler_params=_cparams(("arbitrary", "arbitrary")),
        name="nsa_attn",
    )(qpad, kcmp, vcmp, ks, vs, kw, vw, ba, ovl, ng_lane)


_IN_SIZES = (3 * GDN_WIDTH, GDN_WIDTH, GDN_HEADS, GDN_HEADS, NSA_WIDTH) + (NSA_KV_WIDTH,) * 6 + (3 * NSA_HEADS,)


def _prep_inproj_weight(w):
    offs = np.concatenate([[0], np.cumsum(_IN_SIZES)])
    seg = lambda i: w[:, int(offs[i]):int(offs[i + 1])]
    small = jnp.concatenate([seg(2), seg(3), seg(11)], axis=1)
    pad = lambda n: jnp.zeros((w.shape[0], n), w.dtype)
    big = jnp.concatenate([seg(0), seg(1)] + [seg(i) for i in range(4, 11)]
                          + [small, pad(_GATE_COPY - _SMALL_ROWS), seg(11),
                             pad(LANES - _GATE_COPY - 3 * NSA_HEADS)], axis=1)
    return big.astype(BF16)


def _prep_compress(pe, w1, w2):
    hid = CMP_HIDDEN
    g, dh = NSA_KV_HEADS, NSA_HEAD_DIM
    half = CMP_LEN // 2
    w1r = w1.reshape(CMP_LEN, dh, hid)
    eye = jnp.eye(g, dtype=w1.dtype)
    expand = lambda part: jnp.einsum("ldh,gk->lgdkh", part, eye).reshape(half * g * dh, g * hid)
    wa, wb = expand(w1r[:half]), expand(w1r[half:])
    pe_row = lambda part: jnp.broadcast_to(part[:, None, :], (half, g, dh)).reshape(1, half * g * dh)
    pea, peb = pe_row(pe[:half]), pe_row(pe[half:])
    w2bd = jnp.einsum("hd,gk->ghkd", w2, eye).reshape(g * hid, g * dh)
    return pea, peb, wa.astype(BF16), wb.astype(BF16), w2bd.astype(BF16)


def _overlap_matrix(n_rows, seq):
    n_cmp = (seq - CMP_LEN) // CMP_STRIDE + 1
    n_blk = seq // SEL_BLOCK
    n = np.arange(n_rows)[:, None]
    j = np.arange(LANES)[None, :]
    start, end = n * CMP_STRIDE, n * CMP_STRIDE + CMP_LEN - 1
    ovl = (start < j * SEL_BLOCK + SEL_BLOCK) & (end >= j * SEL_BLOCK) & (n < n_cmp) & (j < n_blk)
    return jnp.asarray(ovl.astype(np.float32), dtype=BF16)


def kernel(x, c, positions, ada_w, ada_b, norm_g, ffn_w_in, ffn_w_out, mix_w_in, gdn_conv_w, gdn_a_log, gdn_dt_bias, gdn_norm_g, cmp_pe_k, cmp_w1_k, cmp_w2_k, cmp_pe_v, cmp_w1_v, cmp_w2_v, nsa_norm_g, mix_w_out, final_norm_g):
    b, t, d = x.shape
    depth = ada_w.shape[0]
    assert t % SEL_CHUNK == 0 and t >= WINDOW + Q_BLOCK and b <= 8

    c_pad = jnp.zeros((8, d), F32).at[:b].set(c.astype(F32))
    mod = _ada_mod(c_pad, ada_w.astype(F32), ada_b.astype(F32))
    mod = mod[:, :b].reshape(depth, b, 3, 3, d)
    mod8 = jnp.concatenate([mod, jnp.zeros((depth, b, 3, 5, d), F32)], axis=3)

    half = ROT_DIM // 2
    inv_freq = jnp.power(ROPE_THETA, -jnp.arange(half, dtype=F32) * (2.0 / ROT_DIM))
    dim = np.arange(LANES) % NSA_HEAD_DIM
    invf_lane = jnp.where(jnp.asarray(dim < ROT_DIM), inv_freq[dim % half], 0.0).reshape(1, LANES)
    cosf, sint = _rope_tables(positions.astype(F32).reshape(b, t, 1), invf_lane, tm=1024)

    n_rows = t // CMP_STRIDE
    ovl = _overlap_matrix(n_rows, t)
    fg = final_norm_g.reshape(1, d).astype(F32)

    w_in_all = ffn_w_in.astype(BF16)
    w_out_all = ffn_w_out.astype(BF16)
    w_mix_all = mix_w_out.astype(BF16)

    gain_all = norm_g.astype(F32).reshape(depth * 3, 1, d)

    for l in range(depth):
        x = _ffn(x, mod8, gain_all, w_in_all, w_out_all, l, 0, fg, final_norm=False)

        (qkv, z, ba, bat, qpad, kc, vc, ks, vs, kw, vw) = _inproj(
            x, mod8, gain_all, l, _prep_inproj_weight(mix_w_in[l]), cosf, sint)

        neg_a = -jnp.exp(gdn_a_log[l].astype(F32))
        dtb = gdn_dt_bias[l].astype(F32)
        avec = jnp.zeros((8, LANES), F32).at[0, GDN_HEADS:2 * GDN_HEADS].set(neg_a)
        avec = avec.at[1, GDN_HEADS:2 * GDN_HEADS].set(dtb)
        avect = jnp.zeros((_SMALL_ROWS, LANES), F32).at[GDN_HEADS:2 * GDN_HEADS, 0].set(neg_a)
        avect = avect.at[GDN_HEADS:2 * GDN_HEADS, 1].set(dtb)
        y_gdn = _gdn(qkv, z, ba, bat, gdn_conv_w[l].astype(F32), avec, avect,
                     gdn_norm_g[l].reshape(1, GDN_HEAD_DIM).astype(F32))

        kcmp = _compress(kc, *_prep_compress(cmp_pe_k[l], cmp_w1_k[l], cmp_w2_k[l]))
        vcmp = _compress(vc, *_prep_compress(cmp_pe_v[l], cmp_w1_v[l], cmp_w2_v[l]))
        ng_lane = jnp.tile(nsa_norm_g[l].astype(F32), NSA_KV_HEADS).reshape(1, LANES)
        y_nsa = _nsa(qpad, kcmp, vcmp, ks, vs, kw, vw, ba, ovl, ng_lane)

        x = _ffn(x, mod8, gain_all, w_in_all, w_out_all, l, 1, fg, final_norm=(l == depth - 1),
                 mix=(y_gdn, y_nsa, w_mix_all))
    return x
```
